```python
import jax
import jax.numpy as jnp
from jax import lax
import numpy as np

D_MODEL = 1024
BATCH = 8
SEQ = 4096
DEPTH = 1

HEAD_DIM = 64
N_HEADS_DIL = 8
DILATED_CONFIGS = ((128, 1), (512, 4), (2048, 16))
N_HEADS_SWA = 8
N_KV_SWA = 2
SWA_WINDOW = 128
ATTN_BLOCK = 128
ROPE_THETA = 10000.0
N_EXPERTS = 32
TOP_K = 4
D_FF = D_MODEL
SWIGLU_ALPHA = 1.702
SWIGLU_LIMIT = 7.0
EXPERT_BLOCK = 512
NORM_EPS = 1e-5

W_DIL = N_HEADS_DIL * HEAD_DIM
W_SWA = N_HEADS_SWA * HEAD_DIM
W_KV_SWA = N_KV_SWA * HEAD_DIM
D_MIX = W_DIL + W_SWA
D_IN = 3 * W_DIL + W_SWA + 2 * W_KV_SWA
SPLITS = [W_DIL, 2 * W_DIL, 3 * W_DIL, 3 * W_DIL + W_SWA, 3 * W_DIL + W_SWA + W_KV_SWA]

kernel_name = 'hybrid_dilated_swa_sink_moe_block'


def rmsnorm(x, g):
    xf = x.astype(jnp.float32)
    y = xf * lax.rsqrt(jnp.mean(xf * xf, axis=-1, keepdims=True) + NORM_EPS)
    return (y * g.astype(jnp.float32)).astype(x.dtype)


def rope_tables(positions):
    inv = ROPE_THETA ** (-jnp.arange(0, HEAD_DIM, 2, dtype=jnp.float32) / HEAD_DIM)
    ang = positions.astype(jnp.float32)[..., None] * inv
    return jnp.cos(ang)[:, :, None, :], jnp.sin(ang)[:, :, None, :]


def apply_rope(t, cos, sin):
    tf = t.astype(jnp.float32)
    half = HEAD_DIM // 2
    t1, t2 = tf[..., :half], tf[..., half:]
    return jnp.concatenate([t1 * cos - t2 * sin, t2 * cos + t1 * sin], axis=-1).astype(t.dtype)


def banded_attention(q, k, v, max_dist):
    bsz, nk, ng, length, hd = q.shape
    nb = -(-length // ATTN_BLOCK)
    pad = nb * ATTN_BLOCK - length
    if pad:
        q = jnp.pad(q, ((0, 0), (0, 0), (0, 0), (0, pad), (0, 0)))
        k = jnp.pad(k, ((0, 0), (0, 0), (0, pad), (0, 0)))
        v = jnp.pad(v, ((0, 0), (0, 0), (0, pad), (0, 0)))
    lp = nb * ATTN_BLOCK

    def two_blocks(t):
        prev = jnp.pad(t, ((0, 0), (0, 0), (ATTN_BLOCK, 0), (0, 0)))[:, :, :lp]
        return jnp.concatenate([prev.reshape(bsz, nk, nb, ATTN_BLOCK, hd),
                                t.reshape(bsz, nk, nb, ATTN_BLOCK, hd)], axis=3)

    kb, vb = two_blocks(k), two_blocks(v)
    qb = q.reshape(bsz, nk, ng, nb, ATTN_BLOCK, hd)
    s = jnp.einsum('bkgnqd,bknsd->bkgnqs', qb, kb,
                   preferred_element_type=jnp.float32) * (hd ** -0.5)
    qi = jnp.arange(ATTN_BLOCK)[:, None]
    kj = jnp.arange(2 * ATTN_BLOCK)[None, :]
    dist = qi - kj + ATTN_BLOCK
    blk = jnp.arange(nb)[:, None, None]
    mask = (dist >= 0) & (dist <= max_dist) & ((blk > 0) | (kj >= ATTN_BLOCK))
    s = jnp.where(mask, s, -jnp.inf)
    m = jnp.max(s, axis=-1, keepdims=True)
    p = jnp.exp(s - m)
    l = jnp.sum(p, axis=-1, keepdims=True)
    o = jnp.einsum('bkgnqs,bknsd->bkgnqd', p, vb.astype(jnp.float32)) / l
    lse = (m + jnp.log(l))[..., 0]
    o = o.reshape(bsz, nk, ng, lp, hd)[:, :, :, :length]
    lse = lse.reshape(bsz, nk, ng, lp)[:, :, :, :length]
    return o, lse


def dilated_attention(q, k, v):
    bsz, nh, seq, hd = q.shape
    outs, lses = [], []
    for window, r in DILATED_CONFIGS:
        length = seq // r

        def by_residue(t):
            return t.reshape(bsz, nh, length, r, hd).transpose(0, 1, 3, 2, 4).reshape(bsz, nh * r, length, hd)

        o, lse = banded_attention(by_residue(q)[:, :, None], by_residue(k), by_residue(v), window // r)
        outs.append(o[:, :, 0].reshape(bsz, nh, r, length, hd).transpose(0, 1, 3, 2, 4).reshape(bsz, nh, seq, hd))
        lses.append(lse[:, :, 0].reshape(bsz, nh, r, length).transpose(0, 1, 3, 2).reshape(bsz, nh, seq))
    wts = jax.nn.softmax(jnp.stack(lses), axis=0)
    return jnp.sum(wts[..., None] * jnp.stack(outs), axis=0).astype(q.dtype)


def sink_sliding_attention(q, k, v, sinks):
    bsz, nq, seq, hd = q.shape
    g = nq // N_KV_SWA
    o, lse = banded_attention(q.reshape(bsz, N_KV_SWA, g, seq, hd), k, v, SWA_WINDOW - 1)
    sink = sinks.astype(jnp.float32).reshape(N_KV_SWA, g)[None, :, :, None]
    o = o * jax.nn.sigmoid(lse - sink)[..., None]
    return o.reshape(bsz, nq, seq, hd).astype(q.dtype)


def hybrid_mixer(h, cos, sin, w_in, b_in, sinks, gn_dil, gn_swa, w_out, b_out):
    bsz, seq, _ = h.shape
    proj = jnp.dot(h, w_in) + b_in
    qa, ka, va, qb, kb, vb = jnp.split(proj, SPLITS, axis=-1)

    def heads(t, n, rot):
        t = t.reshape(bsz, seq, n, HEAD_DIM)
        if rot:
            t = apply_rope(t, cos, sin)
        return t.transpose(0, 2, 1, 3)

    o_dil = dilated_attention(heads(qa, N_HEADS_DIL, True), heads(ka, N_HEADS_DIL, True),
                              heads(va, N_HEADS_DIL, False))
    o_swa = sink_sliding_attention(heads(qb, N_HEADS_SWA, True), heads(kb, N_KV_SWA, True),
                                   heads(vb, N_KV_SWA, False), sinks)
    o_dil = o_dil.transpose(0, 2, 1, 3).reshape(bsz, seq, W_DIL)
    o_swa = o_swa.transpose(0, 2, 1, 3).reshape(bsz, seq, W_SWA)
    mixed = jnp.concatenate([rmsnorm(o_dil, gn_dil), rmsnorm(o_swa, gn_swa)], axis=-1).astype(h.dtype)
    return jnp.dot(mixed, w_out) + b_out


def clamped_swiglu(u):
    glu, lin = u[..., ::2], u[..., 1::2]
    glu = jnp.minimum(glu, SWIGLU_LIMIT)
    lin = jnp.clip(lin, -SWIGLU_LIMIT, SWIGLU_LIMIT)
    return glu * jax.nn.sigmoid(SWIGLU_ALPHA * glu) * (lin + 1.0)


def moe(h, w_router, b_router, w1, b1, w2, b2):
    bsz, seq, d = h.shape
    t = h.reshape(-1, d)
    ntok = t.shape[0]
    logits = jnp.dot(t, w_router, preferred_element_type=jnp.float32) + b_router.astype(jnp.float32)
    top_val, top_idx = lax.top_k(logits, TOP_K)
    gates = jax.nn.softmax(top_val, axis=-1)
    n_assign = ntok * TOP_K
    flat_e = top_idx.reshape(-1)
    flat_tok = jnp.arange(n_assign, dtype=jnp.int32) // TOP_K
    flat_g = gates.reshape(-1)
    order = jnp.argsort(flat_e)
    sorted_e = flat_e[order]
    counts = jnp.bincount(flat_e, length=N_EXPERTS)
    padded = (counts + EXPERT_BLOCK - 1) // EXPERT_BLOCK * EXPERT_BLOCK
    starts = jnp.cumsum(counts) - counts
    pad_ends = jnp.cumsum(padded)
    pad_starts = pad_ends - padded
    dest = pad_starts[sorted_e] + (jnp.arange(n_assign, dtype=jnp.int32) - starts[sorted_e])
    n_slots = (n_assign + N_EXPERTS * EXPERT_BLOCK + EXPERT_BLOCK - 1) // EXPERT_BLOCK * EXPERT_BLOCK
    n_blocks = n_slots // EXPERT_BLOCK
    slot_tok = jnp.zeros((n_slots,), jnp.int32).at[dest].set(flat_tok[order])
    slot_gate = jnp.zeros((n_slots,), jnp.float32).at[dest].set(flat_g[order])
    block_e = jnp.minimum(jnp.searchsorted(pad_ends, jnp.arange(n_blocks) * EXPERT_BLOCK, side='right'),
                          N_EXPERTS - 1)
    xs = t[slot_tok].reshape(n_blocks, EXPERT_BLOCK, d)

    def expert_block(args):
        xb, e = args
        hid = clamped_swiglu(jnp.dot(xb, w1[e]) + b1[e])
        return jnp.dot(hid, w2[e]) + b2[e]

    ys = lax.map(expert_block, (xs, block_e)).reshape(n_slots, d)
    out = jax.ops.segment_sum(ys.astype(jnp.float32) * slot_gate[:, None], slot_tok, num_segments=ntok)
    return out.reshape(bsz, seq, d).astype(h.dtype)


def setup_inputs(seed: int = 0) -> dict:
    key = jax.random.key(seed)
    ks = jax.random.split(key, 20)
    f32 = jnp.float32

    def nrm(k, shape, scale):
        return jax.random.normal(k, shape, f32) * scale

    x = nrm(ks[0], (BATCH, SEQ, D_MODEL), 1.0)
    offset = jax.random.randint(ks[1], (BATCH, 1), 0, 1024, dtype=jnp.int32)
    positions = offset + jnp.arange(SEQ, dtype=jnp.int32)[None, :]
    return {
        'x': x,
        'positions': positions,
        'norm1_g': 1.0 + nrm(ks[2], (DEPTH, D_MODEL), 0.02),
        'w_in': nrm(ks[3], (DEPTH, D_MODEL, D_IN), D_MODEL ** -0.5),
        'b_in': nrm(ks[4], (DEPTH, D_IN), 0.02),
        'sinks': nrm(ks[5], (DEPTH, N_HEADS_SWA), 1.0),
        'gn_dil': 1.0 + nrm(ks[6], (DEPTH, W_DIL), 0.02),
        'gn_swa': 1.0 + nrm(ks[7], (DEPTH, W_SWA), 0.02),
        'w_out': nrm(ks[8], (DEPTH, D_MIX, D_MODEL), D_MIX ** -0.5),
        'b_out': nrm(ks[9], (DEPTH, D_MODEL), 0.02),
        'norm2_g': 1.0 + nrm(ks[10], (DEPTH, D_MODEL), 0.02),
        'w_router': nrm(ks[11], (DEPTH, D_MODEL, N_EXPERTS), D_MODEL ** -0.5),
        'b_router': nrm(ks[12], (DEPTH, N_EXPERTS), 0.01),
        'w_mlp1': nrm(ks[13], (DEPTH, N_EXPERTS, D_MODEL, 2 * D_FF), D_MODEL ** -0.5),
        'b_mlp1': nrm(ks[14], (DEPTH, N_EXPERTS, 2 * D_FF), 0.02),
        'w_mlp2': nrm(ks[15], (DEPTH, N_EXPERTS, D_FF, D_MODEL), D_FF ** -0.5),
        'b_mlp2': nrm(ks[16], (DEPTH, N_EXPERTS, D_MODEL), 0.02),
        'norm_f_g': 1.0 + nrm(ks[17], (D_MODEL,), 0.02),
    }


def reference(x, positions, norm1_g, w_in, b_in, sinks, gn_dil, gn_swa, w_out, b_out,
              norm2_g, w_router, b_router, w_mlp1, b_mlp1, w_mlp2, b_mlp2, norm_f_g):
    cos, sin = rope_tables(positions)
    for layer in range(DEPTH):
        h = rmsnorm(x, norm1_g[layer])
        x = x + hybrid_mixer(h, cos, sin, w_in[layer], b_in[layer], sinks[layer], gn_dil[layer],
                             gn_swa[layer], w_out[layer], b_out[layer]).astype(x.dtype)
        h = rmsnorm(x, norm2_g[layer])
        x = x + moe(h, w_router[layer], b_router[layer], w_mlp1[layer], b_mlp1[layer],
                    w_mlp2[layer], b_mlp2[layer]).astype(x.dtype)
    return rmsnorm(x, norm_f_g)
```

```python
import functools

import jax
import jax.numpy as jnp
from jax import lax
from jax.experimental import pallas as pl
from jax.experimental.pallas import tpu as pltpu

D_MODEL = 1024
HEAD_DIM = 64
N_HEADS_DIL = 8
DILATED_CONFIGS = ((128, 1), (512, 4), (2048, 16))
N_HEADS_SWA = 8
N_KV_SWA = 2
SWA_WINDOW = 128
ATTN_BLOCK = 128
ROPE_THETA = 10000.0
N_EXPERTS = 32
TOP_K = 4
D_FF = D_MODEL
SWIGLU_ALPHA = 1.702
SWIGLU_LIMIT = 7.0
NORM_EPS = 1e-5

W_DIL = N_HEADS_DIL * HEAD_DIM
W_SWA = N_HEADS_SWA * HEAD_DIM
W_KV_SWA = N_KV_SWA * HEAD_DIM
D_IN = 3 * W_DIL + W_SWA + 2 * W_KV_SWA

LANES = 128
PAIR = LANES // HEAD_DIM
TOKEN_TILE = 512
EXPERT_TILE = 512
COMBINE_TILE = 256
GLU_GROUP = 256
VMEM_LIMIT = 56 * 1024 * 1024

_f32 = jnp.float32
_bf16 = jnp.bfloat16


def _rms(x, g):
    return x * lax.rsqrt(jnp.mean(x * x, axis=-1, keepdims=True) + NORM_EPS) * g


def _inproj_kernel(x_ref, pos_ref, inv_ref, g_ref, w_ref, b_ref,
                   qa_ref, ka_ref, va_ref, qb_ref, kb_ref, vb_ref):
    h = _rms(x_ref[...], g_ref[...]).astype(_bf16)
    proj = jnp.dot(h, w_ref[...], preferred_element_type=_f32) + b_ref[...]

    ang = pos_ref[...].astype(_f32) * inv_ref[...]
    lane = lax.broadcasted_iota(jnp.int32, ang.shape, 1)
    first_half = (lane % HEAD_DIM) < (HEAD_DIM // 2)
    cos = jnp.cos(ang)
    sin = jnp.where(first_half, -jnp.sin(ang), jnp.sin(ang))

    def rope(t, scale):
        outs = []
        for c in range(t.shape[1] // LANES):
            tc = t[:, c * LANES:(c + 1) * LANES]
            partner = jnp.where(first_half, pltpu.roll(tc, LANES - HEAD_DIM // 2, 1),
                                pltpu.roll(tc, HEAD_DIM // 2, 1))
            outs.append((tc * cos + partner * sin) * scale)
        return jnp.concatenate(outs, axis=1) if len(outs) > 1 else outs[0]

    scale = HEAD_DIM ** -0.5
    o = 0
    qa_ref[...] = rope(proj[:, o:o + W_DIL], scale).astype(_bf16); o += W_DIL
    ka_ref[...] = rope(proj[:, o:o + W_DIL], 1.0).astype(_bf16); o += W_DIL
    va_ref[...] = proj[:, o:o + W_DIL].astype(_bf16); o += W_DIL
    qb_ref[...] = rope(proj[:, o:o + W_SWA], scale).astype(_bf16); o += W_SWA
    kb = rope(proj[:, o:o + W_KV_SWA], 1.0); o += W_KV_SWA
    vb = proj[:, o:o + W_KV_SWA]

    def dup_heads(t):
        sw = pltpu.roll(t, HEAD_DIM, 1)
        lo = lane < HEAD_DIM
        return jnp.concatenate([jnp.where(lo, t, sw), jnp.where(lo, sw, t)], axis=1)

    kb_ref[...] = dup_heads(kb).astype(_bf16)
    vb_ref[...] = dup_heads(vb).astype(_bf16)


def _inproj_call(x2, pos, inv, g, w, b):
    T = x2.shape[0]
    tm = TOKEN_TILE
    row = lambda i: (i, 0)
    fix = lambda i: (0, 0)
    widths = (W_DIL, W_DIL, W_DIL, W_SWA, PAIR * W_KV_SWA, PAIR * W_KV_SWA)
    return pl.pallas_call(
        _inproj_kernel,
        grid=(T // tm,),
        in_specs=[pl.BlockSpec((tm, D_MODEL), row), pl.BlockSpec((tm, 1), row),
                  pl.BlockSpec((1, LANES), fix), pl.BlockSpec((1, D_MODEL), fix),
                  pl.BlockSpec((D_MODEL, D_IN), fix), pl.BlockSpec((1, D_IN), fix)],
        out_specs=[pl.BlockSpec((tm, wd), row) for wd in widths],
        out_shape=[jax.ShapeDtypeStruct((T, wd), _bf16) for wd in widths],
        compiler_params=pltpu.CompilerParams(dimension_semantics=("parallel",),
                                             vmem_limit_bytes=VMEM_LIMIT),
        name="inproj",
    )(x2, pos, inv, g, w, b)


def _attn_kernel(*refs, configs, seq, with_sink):
    if with_sink:
        q_ref, k_ref, v_ref, sink_ref, o_ref = refs[:5]
        scratch = refs[5:]
    else:
        q_ref, k_ref, v_ref, o_ref = refs[:4]
        scratch = refs[4:]
    qf, kf, vf = scratch[:3]
    obufs = scratch[3:3 + len(configs)]
    lbufs = scratch[3 + len(configs):]
    blk = ATTN_BLOCK

    qf[...] = q_ref[...].astype(_f32)
    kf[...] = k_ref[...].astype(_f32)
    vf[...] = v_ref[...].astype(_f32)

    lane = lax.broadcasted_iota(jnp.int32, (blk, LANES), 1)
    head0 = lane < HEAD_DIM
    qi = lax.broadcasted_iota(jnp.int32, (blk, 2 * blk), 0)
    kj = lax.broadcasted_iota(jnp.int32, (blk, 2 * blk), 1)
    dist = qi - kj + blk

    for ci, (max_dist, r) in enumerate(configs):
        nb = seq // r // blk
        band = (dist >= 0) & (dist <= max_dist)
        obuf, lbuf = obufs[ci], lbufs[ci]

        def rows(start, r=r):
            if r == 1:
                return pl.ds(pl.multiple_of(start, blk), blk)
            return pl.ds(start, blk, stride=r)

        def body(n, carry, r=r, nb=nb, band=band, obuf=obuf, lbuf=lbuf, rows=rows):
            c = n // nb
            j = n % nb
            start = c + (r * blk) * j
            prev = jnp.maximum(start - r * blk, c)
            q = qf[rows(start), :]
            k2 = jnp.concatenate([kf[rows(prev), :], kf[rows(start), :]], axis=0).astype(_bf16)
            v2 = jnp.concatenate([vf[rows(prev), :], vf[rows(start), :]], axis=0).astype(_bf16)
            mask = band & (kj >= jnp.where(j > 0, 0, blk))
            outs, lses = [], []
            for hm in (head0, jnp.logical_not(head0)):
                qh = jnp.where(hm, q, 0.0).astype(_bf16)
                s = lax.dot_general(qh, k2, (((1,), (1,)), ((), ())), preferred_element_type=_f32)
                s = jnp.where(mask, s, -jnp.inf)
                m = jnp.max(s, axis=-1, keepdims=True)
                p = jnp.exp(s - m)
                l = jnp.sum(p, axis=-1, keepdims=True)
                o = jnp.dot(p.astype(_bf16), v2, preferred_element_type=_f32) / l
                outs.append(o)
                lses.append(jnp.broadcast_to(m + jnp.log(l), (blk, LANES)))
            obuf[rows(start), :] = jnp.where(head0, outs[0], outs[1])
            lbuf[rows(start), :] = jnp.where(head0, lses[0], lses[1])
            return carry

        lax.fori_loop(0, seq // blk, body, 0)

    chunk = 4 * blk

    def finish(i, carry):
        sl = pl.ds(pl.multiple_of(i * chunk, chunk), chunk)
        if with_sink:
            o = obufs[0][sl, :] * (1.0 / (1.0 + jnp.exp(sink_ref[...] - lbufs[0][sl, :])))
        else:
            ls = [lb[sl, :] for lb in lbufs]
            mx = functools.reduce(jnp.maximum, ls)
            es = [jnp.exp(x - mx) for x in ls]
            den = functools.reduce(lambda a, b: a + b, es)
            o = functools.reduce(lambda a, b: a + b,
                                 [e / den * ob[sl, :] for e, ob in zip(es, obufs)])
        o_ref[sl, :] = o
        return carry

    lax.fori_loop(0, seq // chunk, finish, 0)


def _attn_call(q, k, v, sinks, *, batch, seq, configs, kv_group):
    n_pairs = q.shape[1] // LANES
    with_sink = sinks is not None
    qmap = lambda b, p: (b, p)
    kvmap = lambda b, p: (b, p // kv_group)
    in_specs = [pl.BlockSpec((seq, LANES), qmap), pl.BlockSpec((seq, LANES), kvmap),
                pl.BlockSpec((seq, LANES), kvmap)]
    args = [q, k, v]
    if with_sink:
        in_specs.append(pl.BlockSpec((1, LANES), lambda b, p: (0, p)))
        args.append(sinks)
    n_buf = 3 + 2 * len(configs)
    return pl.pallas_call(
        functools.partial(_attn_kernel, configs=configs, seq=seq, with_sink=with_sink),
        grid=(batch, n_pairs),
        in_specs=in_specs,
        out_specs=pl.BlockSpec((seq, LANES), qmap),
        out_shape=jax.ShapeDtypeStruct(q.shape, _f32),
        scratch_shapes=[pltpu.VMEM((seq, LANES), _f32) for _ in range(n_buf)],
        compiler_params=pltpu.CompilerParams(dimension_semantics=("parallel", "parallel"),
                                             vmem_limit_bytes=VMEM_LIMIT),
        name="attn_sink" if with_sink else "attn_dilated",
    )(*args)


def _outproj_kernel(od_ref, os_ref, x_ref, gd_ref, gs_ref, w_ref, b_ref, g2_ref, wr_ref, br_ref,
                    x1_ref, ri_ref, rg_ref, cnt_ref, carry_ref):
    tm = x_ref.shape[0]

    @pl.when(pl.program_id(0) == 0)
    def _():
        carry_ref[...] = jnp.zeros_like(carry_ref)

    mixed = jnp.concatenate([_rms(od_ref[...], gd_ref[...]), _rms(os_ref[...], gs_ref[...])], axis=1)
    x1 = x_ref[...] + jnp.dot(mixed.astype(_bf16), w_ref[...], preferred_element_type=_f32) + b_ref[...]
    x1_ref[...] = x1

    h2 = _rms(x1, g2_ref[...])
    logits = lax.dot_general(wr_ref[...], h2, (((1,), (1,)), ((), ())),
                             precision=lax.Precision.HIGHEST, preferred_element_type=_f32) + br_ref[...]
    eiota = lax.broadcasted_iota(jnp.int32, logits.shape, 0).astype(_f32)
    vals, idxs, sels = [], [], []
    l = logits
    for _ in range(TOP_K):
        m = jnp.max(l, axis=0, keepdims=True)
        idx = jnp.min(jnp.where(l == m, eiota, float(N_EXPERTS)), axis=0, keepdims=True)
        sel = eiota == idx
        vals.append(m); idxs.append(idx); sels.append(sel)
        l = jnp.where(sel, -jnp.inf, l)
    exps = [jnp.exp(v - vals[0]) for v in vals]
    den = functools.reduce(lambda a, b: a + b, exps)
    gates = [e / den for e in exps]

    chosen = functools.reduce(jnp.logical_or, sels).astype(_f32)
    ti = lax.broadcasted_iota(jnp.int32, (tm, tm), 0)
    tj = lax.broadcasted_iota(jnp.int32, (tm, tm), 1)
    before = (ti < tj).astype(_bf16)
    prefix = jnp.dot(chosen.astype(_bf16), before, preferred_element_type=_f32) + carry_ref[...]
    ranks = [jnp.sum(jnp.where(s, prefix, 0.0), axis=0, keepdims=True).astype(jnp.int32) for s in sels]
    carry_ref[...] = carry_ref[...] + jnp.sum(chosen, axis=1, keepdims=True)

    ri_ref[...] = jnp.concatenate([ix.astype(jnp.int32) for ix in idxs] + ranks, axis=0)
    rg_ref[...] = jnp.concatenate(gates + gates, axis=0)
    cnt_ref[...] = jnp.broadcast_to(carry_ref[...], cnt_ref.shape)


def _outproj_call(od, osw, x2, gd, gs, w, b, g2, wr_t, br):
    T = x2.shape[0]
    tm = TOKEN_TILE
    row = lambda i: (i, 0)
    col = lambda i: (0, i)
    fix = lambda i: (0, 0)
    return pl.pallas_call(
        _outproj_kernel,
        grid=(T // tm,),
        in_specs=[pl.BlockSpec((tm, W_DIL), row), pl.BlockSpec((tm, W_SWA), row),
                  pl.BlockSpec((tm, D_MODEL), row),
                  pl.BlockSpec((1, W_DIL), fix), pl.BlockSpec((1, W_SWA), fix),
                  pl.BlockSpec((D_MODEL, D_MODEL), fix), pl.BlockSpec((1, D_MODEL), fix),
                  pl.BlockSpec((1, D_MODEL), fix),
                  pl.BlockSpec((N_EXPERTS, D_MODEL), fix), pl.BlockSpec((N_EXPERTS, 1), fix)],
        out_specs=[pl.BlockSpec((tm, D_MODEL), row), pl.BlockSpec((2 * TOP_K, tm), col),
                   pl.BlockSpec((2 * TOP_K, tm), col), pl.BlockSpec((N_EXPERTS, LANES), fix)],
        out_shape=[jax.ShapeDtypeStruct((T, D_MODEL), _f32),
                   jax.ShapeDtypeStruct((2 * TOP_K, T), jnp.int32),
                   jax.ShapeDtypeStruct((2 * TOP_K, T), _f32),
                   jax.ShapeDtypeStruct((N_EXPERTS, LANES), _f32)],
        scratch_shapes=[pltpu.VMEM((N_EXPERTS, 1), _f32)],
        compiler_params=pltpu.CompilerParams(dimension_semantics=("arbitrary",),
                                             vmem_limit_bytes=VMEM_LIMIT),
        name="outproj_router",
    )(od, osw, x2, gd, gs, w, b, g2, wr_t, br)


def _moe_kernel(be_ref, nv_ref, tk_ref, x1_hbm, g2_ref, w1_ref, b1_ref, w2_ref, b2_ref, y_hbm,
                xbuf, ybuf, w1p, w2b, sem):
    i = pl.program_id(0)
    nv = nv_ref[i]

    @pl.when(i == 0)
    def _():
        xbuf[...] = jnp.zeros_like(xbuf)

    def gather(r):
        tok = tk_ref[0, r] // TOP_K
        return pltpu.make_async_copy(x1_hbm.at[pl.ds(tok, 1)], xbuf.at[pl.ds(r, 1)], sem.at[0])

    def scatter(r):
        return pltpu.make_async_copy(ybuf.at[pl.ds(r, 1)], y_hbm.at[pl.ds(tk_ref[0, r], 1)], sem.at[1])

    @pl.when(nv > 0)
    def _():
        def start_g(r, c):
            gather(r).start()
            return c
        lax.fori_loop(0, nv, start_g, 0)

        changed = jnp.logical_or(i == 0, be_ref[i] != be_ref[jnp.maximum(i - 1, 0)])

        @pl.when(changed)
        def _():
            src = lax.broadcasted_iota(jnp.int32, (GLU_GROUP, GLU_GROUP), 0)
            dst = lax.broadcasted_iota(jnp.int32, (GLU_GROUP, GLU_GROUP), 1)
            half = GLU_GROUP // 2
            perm = (src == jnp.where(dst < half, 2 * dst, 2 * (dst - half) + 1)).astype(_bf16)
            for g in range(2 * D_FF // GLU_GROUP):
                cols = slice(g * GLU_GROUP, (g + 1) * GLU_GROUP)
                w1p[:, cols] = jnp.dot(w1_ref[0, :, cols].astype(_bf16), perm,
                                       preferred_element_type=_f32).astype(_bf16)
            w2b[...] = w2_ref[0].astype(_bf16)

        def wait_g(r, c):
            gather(r).wait()
            return c
        lax.fori_loop(0, nv, wait_g, 0)

        h = _rms(xbuf[...], g2_ref[...]).astype(_bf16)
        u = jnp.dot(h, w1p[...], preferred_element_type=_f32) + b1_ref[0]
        half = GLU_GROUP // 2
        hid = []
        for g in range(2 * D_FF // GLU_GROUP):
            glu = jnp.minimum(u[:, g * GLU_GROUP:g * GLU_GROUP + half], SWIGLU_LIMIT)
            lin = jnp.clip(u[:, g * GLU_GROUP + half:(g + 1) * GLU_GROUP], -SWIGLU_LIMIT, SWIGLU_LIMIT)
            hid.append(glu * (1.0 / (1.0 + jnp.exp(-SWIGLU_ALPHA * glu))) * (lin + 1.0))
        hid = jnp.concatenate(hid, axis=1).astype(_bf16)
        ybuf[...] = jnp.dot(hid, w2b[...], preferred_element_type=_f32) + b2_ref[0]

        def start_s(r, c):
            scatter(r).start()
            return c
        lax.fori_loop(0, nv, start_s, 0)

        def wait_s(r, c):
            scatter(r).wait()
            return c
        lax.fori_loop(0, nv, wait_s, 0)


def _moe_call(block_e, n_valid, slot_tk, x1, g2, w1, b1p, w2, b2):
    T = x1.shape[0]
    nblk = block_e.shape[0]
    tm = EXPERT_TILE
    by_e = lambda i, be, nv: (be[i], 0, 0)
    grid_spec = pltpu.PrefetchScalarGridSpec(
        num_scalar_prefetch=2,
        grid=(nblk,),
        in_specs=[pl.BlockSpec((None, 1, tm), lambda i, be, nv: (i, 0, 0), memory_space=pltpu.SMEM),
                  pl.BlockSpec(memory_space=pl.ANY),
                  pl.BlockSpec((1, D_MODEL), lambda i, be, nv: (0, 0)),
                  pl.BlockSpec((1, D_MODEL, 2 * D_FF), by_e), pl.BlockSpec((1, 1, 2 * D_FF), by_e),
                  pl.BlockSpec((1, D_FF, D_MODEL), by_e), pl.BlockSpec((1, 1, D_MODEL), by_e)],
        out_specs=pl.BlockSpec(memory_space=pl.ANY),
        scratch_shapes=[pltpu.VMEM((tm, D_MODEL), _f32), pltpu.VMEM((tm, D_MODEL), _f32),
                        pltpu.VMEM((D_MODEL, 2 * D_FF), _bf16), pltpu.VMEM((D_FF, D_MODEL), _bf16),
                        pltpu.SemaphoreType.DMA((2,))],
    )
    return pl.pallas_call(
        _moe_kernel,
        grid_spec=grid_spec,
        out_shape=jax.ShapeDtypeStruct((T * TOP_K, D_MODEL), _f32),
        compiler_params=pltpu.CompilerParams(dimension_semantics=("arbitrary",),
                                             vmem_limit_bytes=VMEM_LIMIT),
        name="moe_experts",
    )(block_e, n_valid, slot_tk, x1, g2, w1, b1p, w2, b2)


def _combine_kernel(x1_ref, y_ref, g_ref, gf_ref, o_ref):
    g = g_ref[...]
    acc = x1_ref[...]
    for k in range(TOP_K):
        acc = acc + g[:, k:k + 1] * y_ref[:, k * D_MODEL:(k + 1) * D_MODEL]
    o_ref[...] = _rms(acc, gf_ref[...])


def _combine_call(x1, y, gates_t, gf):
    T = x1.shape[0]
    tm = COMBINE_TILE
    row = lambda i: (i, 0)
    return pl.pallas_call(
        _combine_kernel,
        grid=(T // tm,),
        in_specs=[pl.BlockSpec((tm, D_MODEL), row), pl.BlockSpec((tm, TOP_K * D_MODEL), row),
                  pl.BlockSpec((tm, TOP_K), row), pl.BlockSpec((1, D_MODEL), lambda i: (0, 0))],
        out_specs=pl.BlockSpec((tm, D_MODEL), row),
        out_shape=jax.ShapeDtypeStruct((T, D_MODEL), _f32),
        compiler_params=pltpu.CompilerParams(dimension_semantics=("parallel",),
                                             vmem_limit_bytes=VMEM_LIMIT),
        name="combine_norm",
    )(x1, y, gates_t, gf)


def _plan_blocks(route_i, counts, n_tokens):
    tm = EXPERT_TILE
    n_slots = n_tokens * TOP_K + N_EXPERTS * tm
    nblk = n_slots // tm
    counts = counts.astype(jnp.int32)
    padded = (counts + tm - 1) // tm * tm
    pad_end = jnp.cumsum(padded)
    pad_start = pad_end - padded
    e = route_i[:TOP_K]
    dest = pad_start[e] + route_i[TOP_K:]
    tk = jnp.arange(n_tokens, dtype=jnp.int32)[None, :] * TOP_K + jnp.arange(TOP_K, dtype=jnp.int32)[:, None]
    slot_tk = jnp.zeros((n_slots,), jnp.int32).at[dest.reshape(-1)].set(tk.reshape(-1), unique_indices=True)
    blk_start = jnp.arange(nblk, dtype=jnp.int32) * tm
    n_used = pad_end[-1] // tm
    last = jnp.maximum(n_used - 1, 0)
    be = jnp.minimum(jnp.searchsorted(pad_end, blk_start, side="right"), N_EXPERTS - 1).astype(jnp.int32)
    be = jnp.where(jnp.arange(nblk) < n_used, be, be[last])
    n_valid = jnp.clip(counts[be] - (blk_start - pad_start[be]), 0, tm)
    n_valid = jnp.where(jnp.arange(nblk) < n_used, n_valid, 0).astype(jnp.int32)
    return be, n_valid, slot_tk.reshape(nblk, 1, tm)


def kernel(x, positions, norm1_g, w_in, b_in, sinks, gn_dil, gn_swa, w_out, b_out,
           norm2_g, w_router, b_router, w_mlp1, b_mlp1, w_mlp2, b_mlp2, norm_f_g):
    batch, seq, d = x.shape
    T = batch * seq
    x2 = x.reshape(T, d)
    depth = norm1_g.shape[0]
    assert depth == 1, "the final norm is fused into the last layer's combine kernel"
    inv = ROPE_THETA ** (-jnp.arange(0, HEAD_DIM, 2, dtype=_f32) / HEAD_DIM)
    inv = jnp.tile(inv, LANES // (HEAD_DIM // 2)).reshape(1, LANES)
    pos = positions.reshape(T, 1)
    for layer in range(depth):
        qa, ka, va, qb, kb, vb = _inproj_call(
            x2, pos, inv, norm1_g[layer].reshape(1, d), w_in[layer].astype(_bf16), b_in[layer].reshape(1, D_IN))
        o_dil = _attn_call(qa, ka, va, None, batch=batch, seq=seq,
                           configs=tuple((w // r, r) for w, r in DILATED_CONFIGS), kv_group=1)
        sink_lanes = jnp.repeat(sinks[layer].astype(_f32), HEAD_DIM).reshape(1, W_SWA)
        o_swa = _attn_call(qb, kb, vb, sink_lanes, batch=batch, seq=seq,
                           configs=((SWA_WINDOW - 1, 1),), kv_group=N_HEADS_SWA // N_KV_SWA // PAIR)
        x1, route_i, route_g, counts = _outproj_call(
            o_dil, o_swa, x2, gn_dil[layer].reshape(1, W_DIL), gn_swa[layer].reshape(1, W_SWA),
            w_out[layer].astype(_bf16), b_out[layer].reshape(1, d), norm2_g[layer].reshape(1, d),
            w_router[layer].T, b_router[layer].reshape(N_EXPERTS, 1))
        block_e, n_valid, slot_tk = _plan_blocks(route_i, counts[:, 0], T)
        half = GLU_GROUP // 2
        b1p = b_mlp1[layer].reshape(N_EXPERTS, 2 * D_FF // GLU_GROUP, half, 2)
        b1p = jnp.swapaxes(b1p, 2, 3).reshape(N_EXPERTS, 1, 2 * D_FF)
        y = _moe_call(block_e, n_valid, slot_tk, x1, norm2_g[layer].reshape(1, d),
                      w_mlp1[layer], b1p, w_mlp2[layer], b_mlp2[layer].reshape(N_EXPERTS, 1, d))
        x2 = _combine_call(x1, y.reshape(T, TOP_K * d), route_g[:TOP_K].T, norm_f_g.reshape(1, d))
    return x2.reshape(batch, seq, d)
```

```python
import functools

import jax
import jax.numpy as jnp
from jax import lax
from jax.experimental import pallas as pl
from jax.experimental.pallas import tpu as pltpu

D_MODEL = 1024
HEAD_DIM = 64
N_HEADS_DIL = 8
DILATED_CONFIGS = ((128, 1), (512, 4), (2048, 16))
N_HEADS_SWA = 8
N_KV_SWA = 2
SWA_WINDOW = 128
ATTN_BLOCK = 128
ROPE_THETA = 10000.0
N_EXPERTS = 32
TOP_K = 4
D_FF = D_MODEL
SWIGLU_ALPHA = 1.702
SWIGLU_LIMIT = 7.0
NORM_EPS = 1e-5

W_DIL = N_HEADS_DIL * HEAD_DIM
W_SWA = N_HEADS_SWA * HEAD_DIM
W_KV_SWA = N_KV_SWA * HEAD_DIM
D_IN = 3 * W_DIL + W_SWA + 2 * W_KV_SWA

LANES = 128
PAIR = LANES // HEAD_DIM
SUBLANES = 8
ROW_TILES = D_MODEL // LANES
TOKEN_TILE = 512
EXPERT_TILE = 512
DISPATCH_TILE = 1024
COMBINE_TILE = 256
DMA_GROUP = 8
ATTN_UNROLL = 4
GLU_GROUP = 256
VMEM_LIMIT = 56 * 1024 * 1024

_f32 = jnp.float32
_bf16 = jnp.bfloat16


def _rms(x, g):
    return x * lax.rsqrt(jnp.mean(x * x, axis=-1, keepdims=True) + NORM_EPS) * g


def _store_token_rows(ref, x):
    tm = x.shape[0]
    for c in range(ROW_TILES):
        ref[pl.ds(c, tm, stride=ROW_TILES), :] = x[:, c * LANES:(c + 1) * LANES]


def _load_token_rows(ref, tm):
    return jnp.concatenate([ref[pl.ds(c, tm, stride=ROW_TILES), :] for c in range(ROW_TILES)], axis=1)


def _token_row(ref, tok):
    return ref.at[pl.ds(pl.multiple_of(tok * ROW_TILES, ROW_TILES), ROW_TILES)]


def _inproj_kernel(x_ref, pos_ref, inv_ref, g_ref, w_ref, b_ref,
                   qa_ref, ka_ref, va_ref, qb_ref, kb_ref, vb_ref):
    h = _rms(x_ref[...], g_ref[...]).astype(_bf16)
    proj = jnp.dot(h, w_ref[...], preferred_element_type=_f32) + b_ref[...]

    ang = pos_ref[...].astype(_f32) * inv_ref[...]
    lane = lax.broadcasted_iota(jnp.int32, ang.shape, 1)
    first_half = (lane % HEAD_DIM) < (HEAD_DIM // 2)
    cos = jnp.cos(ang)
    sin = jnp.where(first_half, -jnp.sin(ang), jnp.sin(ang))

    def rope(t, scale):
        outs = []
        for c in range(t.shape[1] // LANES):
            tc = t[:, c * LANES:(c + 1) * LANES]
            partner = jnp.where(first_half, pltpu.roll(tc, LANES - HEAD_DIM // 2, 1),
                                pltpu.roll(tc, HEAD_DIM // 2, 1))
            outs.append((tc * cos + partner * sin) * scale)
        return jnp.concatenate(outs, axis=1) if len(outs) > 1 else outs[0]

    scale = HEAD_DIM ** -0.5
    o = 0
    qa_ref[...] = rope(proj[:, o:o + W_DIL], scale).astype(_bf16); o += W_DIL
    ka_ref[...] = rope(proj[:, o:o + W_DIL], 1.0).astype(_bf16); o += W_DIL
    va_ref[...] = proj[:, o:o + W_DIL].astype(_bf16); o += W_DIL
    qb_ref[...] = rope(proj[:, o:o + W_SWA], scale).astype(_bf16); o += W_SWA
    kb = rope(proj[:, o:o + W_KV_SWA], 1.0); o += W_KV_SWA
    vb = proj[:, o:o + W_KV_SWA]

    def dup_heads(t):
        sw = pltpu.roll(t, HEAD_DIM, 1)
        lo = lane < HEAD_DIM
        return jnp.concatenate([jnp.where(lo, t, sw), jnp.where(lo, sw, t)], axis=1)

    kb_ref[...] = dup_heads(kb).astype(_bf16)
    vb_ref[...] = dup_heads(vb).astype(_bf16)


def _inproj_call(x2, pos, inv, g, w, b):
    T = x2.shape[0]
    tm = TOKEN_TILE
    row = lambda i: (i, 0)
    fix = lambda i: (0, 0)
    widths = (W_DIL, W_DIL, W_DIL, W_SWA, PAIR * W_KV_SWA, PAIR * W_KV_SWA)
    return pl.pallas_call(
        _inproj_kernel,
        grid=(T // tm,),
        in_specs=[pl.BlockSpec((tm, D_MODEL), row), pl.BlockSpec((tm, 1), row),
                  pl.BlockSpec((1, LANES), fix), pl.BlockSpec((1, D_MODEL), fix),
                  pl.BlockSpec((D_MODEL, D_IN), fix), pl.BlockSpec((1, D_IN), fix)],
        out_specs=[pl.BlockSpec((tm, wd), row) for wd in widths],
        out_shape=[jax.ShapeDtypeStruct((T, wd), _bf16) for wd in widths],
        compiler_params=pltpu.CompilerParams(dimension_semantics=("parallel",),
                                             vmem_limit_bytes=VMEM_LIMIT),
        name="inproj",
    )(x2, pos, inv, g, w, b)


def _attn_kernel(*refs, configs, seq, with_sink):
    if with_sink:
        q_ref, k_ref, v_ref, sink_ref, o_ref = refs[:5]
        scratch = refs[5:]
    else:
        q_ref, k_ref, v_ref, o_ref = refs[:4]
        scratch = refs[4:]
    qf, kf, vf = scratch[:3]
    obufs = scratch[3:3 + len(configs)]
    lbufs = scratch[3 + len(configs):]
    blk = ATTN_BLOCK

    qf[...] = q_ref[...].astype(_f32)
    kf[...] = k_ref[...].astype(_f32)
    vf[...] = v_ref[...].astype(_f32)

    lane = lax.broadcasted_iota(jnp.int32, (blk, LANES), 1)
    head0 = lane < HEAD_DIM
    qi = lax.broadcasted_iota(jnp.int32, (blk, 2 * blk), 0)
    kj = lax.broadcasted_iota(jnp.int32, (blk, 2 * blk), 1)
    dist = qi - kj + blk

    for ci, (max_dist, r) in enumerate(configs):
        nb = seq // r // blk
        band = (dist >= 0) & (dist <= max_dist)
        obuf, lbuf = obufs[ci], lbufs[ci]

        def rows(start, r=r):
            if r == 1:
                return pl.ds(pl.multiple_of(start, blk), blk)
            return pl.ds(start, blk, stride=r)

        def body(n, carry, r=r, nb=nb, band=band, obuf=obuf, lbuf=lbuf, rows=rows):
            c = n // nb
            j = n % nb
            start = c + (r * blk) * j
            prev = jnp.maximum(start - r * blk, c)
            q = qf[rows(start), :]
            k2 = jnp.concatenate([kf[rows(prev), :], kf[rows(start), :]], axis=0).astype(_bf16)
            v2 = jnp.concatenate([vf[rows(prev), :], vf[rows(start), :]], axis=0).astype(_bf16)
            mask = band & (kj >= jnp.where(j > 0, 0, blk))
            outs, lses = [], []
            for hm in (head0, jnp.logical_not(head0)):
                qh = jnp.where(hm, q, 0.0).astype(_bf16)
                s = lax.dot_general(qh, k2, (((1,), (1,)), ((), ())), preferred_element_type=_f32)
                s = jnp.where(mask, s, -jnp.inf)
                m = jnp.max(s, axis=-1, keepdims=True)
                p = jnp.exp(s - m)
                l = jnp.sum(p, axis=-1, keepdims=True)
                o = jnp.dot(p.astype(_bf16), v2, preferred_element_type=_f32) / l
                outs.append(o)
                lses.append(jnp.broadcast_to(m + jnp.log(l), (blk, LANES)))
            obuf[rows(start), :] = jnp.where(head0, outs[0], outs[1])
            lbuf[rows(start), :] = jnp.where(head0, lses[0], lses[1])
            return carry

        lax.fori_loop(0, seq // blk, body, 0, unroll=ATTN_UNROLL)

    chunk = 4 * blk

    def finish(i, carry):
        sl = pl.ds(pl.multiple_of(i * chunk, chunk), chunk)
        if with_sink:
            o = obufs[0][sl, :] * (1.0 / (1.0 + jnp.exp(sink_ref[...] - lbufs[0][sl, :])))
        else:
            ls = [lb[sl, :] for lb in lbufs]
            mx = functools.reduce(jnp.maximum, ls)
            es = [jnp.exp(x - mx) for x in ls]
            den = functools.reduce(lambda a, b: a + b, es)
            o = functools.reduce(lambda a, b: a + b,
                                 [e / den * ob[sl, :] for e, ob in zip(es, obufs)])
        o_ref[sl, :] = o
        return carry

    lax.fori_loop(0, seq // chunk, finish, 0)


def _attn_call(q, k, v, sinks, *, batch, seq, configs, kv_group):
    n_pairs = q.shape[1] // LANES
    with_sink = sinks is not None
    qmap = lambda b, p: (b, p)
    kvmap = lambda b, p: (b, p // kv_group)
    in_specs = [pl.BlockSpec((seq, LANES), qmap), pl.BlockSpec((seq, LANES), kvmap),
                pl.BlockSpec((seq, LANES), kvmap)]
    args = [q, k, v]
    if with_sink:
        in_specs.append(pl.BlockSpec((1, LANES), lambda b, p: (0, p)))
        args.append(sinks)
    n_buf = 3 + 2 * len(configs)
    return pl.pallas_call(
        functools.partial(_attn_kernel, configs=configs, seq=seq, with_sink=with_sink),
        grid=(batch, n_pairs),
        in_specs=in_specs,
        out_specs=pl.BlockSpec((seq, LANES), qmap),
        out_shape=jax.ShapeDtypeStruct(q.shape, _f32),
        scratch_shapes=[pltpu.VMEM((seq, LANES), _f32) for _ in range(n_buf)],
        compiler_params=pltpu.CompilerParams(dimension_semantics=("parallel", "parallel"),
                                             vmem_limit_bytes=VMEM_LIMIT),
        name="attn_sink" if with_sink else "attn_dilated",
    )(*args)


def _outproj_kernel(od_ref, os_ref, x_ref, gd_ref, gs_ref, w_ref, b_ref, g2_ref, wr_ref, br_ref,
                    x1_ref, ri_ref, rg_ref, cnt_ref, carry_ref):
    tm = x_ref.shape[0]

    @pl.when(pl.program_id(0) == 0)
    def _():
        carry_ref[...] = jnp.zeros_like(carry_ref)

    mixed = jnp.concatenate([_rms(od_ref[...], gd_ref[...]), _rms(os_ref[...], gs_ref[...])], axis=1)
    x1 = x_ref[...] + jnp.dot(mixed.astype(_bf16), w_ref[...], preferred_element_type=_f32) + b_ref[...]
    _store_token_rows(x1_ref, x1)

    h2 = _rms(x1, g2_ref[...])
    logits = lax.dot_general(wr_ref[...], h2, (((1,), (1,)), ((), ())),
                             precision=lax.Precision.HIGHEST, preferred_element_type=_f32) + br_ref[...]
    eiota = lax.broadcasted_iota(jnp.int32, logits.shape, 0).astype(_f32)
    vals, idxs, sels = [], [], []
    l = logits
    for _ in range(TOP_K):
        m = jnp.max(l, axis=0, keepdims=True)
        idx = jnp.min(jnp.where(l == m, eiota, float(N_EXPERTS)), axis=0, keepdims=True)
        sel = eiota == idx
        vals.append(m); idxs.append(idx); sels.append(sel)
        l = jnp.where(sel, -jnp.inf, l)
    exps = [jnp.exp(v - vals[0]) for v in vals]
    den = functools.reduce(lambda a, b: a + b, exps)
    gates = [e / den for e in exps]

    chosen = functools.reduce(jnp.logical_or, sels).astype(_f32)
    ti = lax.broadcasted_iota(jnp.int32, (tm, tm), 0)
    tj = lax.broadcasted_iota(jnp.int32, (tm, tm), 1)
    before = (ti < tj).astype(_bf16)
    prefix = jnp.dot(chosen.astype(_bf16), before, preferred_element_type=_f32) + carry_ref[...]
    ranks = [jnp.sum(jnp.where(s, prefix, 0.0), axis=0, keepdims=True).astype(jnp.int32) for s in sels]
    carry_ref[...] = carry_ref[...] + jnp.sum(chosen, axis=1, keepdims=True)

    ri_ref[...] = jnp.concatenate([ix.astype(jnp.int32) for ix in idxs] + ranks, axis=0)
    rg_ref[...] = jnp.concatenate(gates + gates, axis=0)
    cnt_ref[...] = jnp.broadcast_to(carry_ref[...], cnt_ref.shape)


def _outproj_call(od, osw, x2, gd, gs, w, b, g2, wr_t, br):
    T = x2.shape[0]
    tm = TOKEN_TILE
    row = lambda i: (i, 0)
    col = lambda i: (0, i)
    fix = lambda i: (0, 0)
    return pl.pallas_call(
        _outproj_kernel,
        grid=(T // tm,),
        in_specs=[pl.BlockSpec((tm, W_DIL), row), pl.BlockSpec((tm, W_SWA), row),
                  pl.BlockSpec((tm, D_MODEL), row),
                  pl.BlockSpec((1, W_DIL), fix), pl.BlockSpec((1, W_SWA), fix),
                  pl.BlockSpec((D_MODEL, D_MODEL), fix), pl.BlockSpec((1, D_MODEL), fix),
                  pl.BlockSpec((1, D_MODEL), fix),
                  pl.BlockSpec((N_EXPERTS, D_MODEL), fix), pl.BlockSpec((N_EXPERTS, 1), fix)],
        out_specs=[pl.BlockSpec((tm * ROW_TILES, LANES), row), pl.BlockSpec((2 * TOP_K, tm), col),
                   pl.BlockSpec((2 * TOP_K, tm), col), pl.BlockSpec((N_EXPERTS, LANES), fix)],
        out_shape=[jax.ShapeDtypeStruct((T * ROW_TILES, LANES), _f32),
                   jax.ShapeDtypeStruct((2 * TOP_K, T), jnp.int32),
                   jax.ShapeDtypeStruct((2 * TOP_K, T), _f32),
                   jax.ShapeDtypeStruct((N_EXPERTS, LANES), _f32)],
        scratch_shapes=[pltpu.VMEM((N_EXPERTS, 1), _f32)],
        compiler_params=pltpu.CompilerParams(dimension_semantics=("arbitrary",),
                                             vmem_limit_bytes=VMEM_LIMIT),
        name="outproj_router",
    )(od, osw, x2, gd, gs, w, b, g2, wr_t, br)


def _dispatch_kernel(d_ref, x1_hbm, xs_hbm, sem):
    td = DISPATCH_TILE
    base = pl.program_id(0) * td

    def start_group(g, c):
        for rr in range(DMA_GROUP):
            r = g * DMA_GROUP + rr
            src = _token_row(x1_hbm, base + r)
            for k in range(TOP_K):
                pltpu.make_async_copy(src, _token_row(xs_hbm, d_ref[0, k * td + r]), sem).start()
        return c

    lax.fori_loop(0, td // DMA_GROUP, start_group, 0)

    def wait_group(g, c):
        for _ in range(DMA_GROUP * TOP_K):
            pltpu.make_async_copy(_token_row(x1_hbm, 0), _token_row(xs_hbm, 0), sem).wait()
        return c

    lax.fori_loop(0, td // DMA_GROUP, wait_group, 0)


def _dispatch_call(dest_tiles, x1r, n_slots):
    nt = dest_tiles.shape[0]
    return pl.pallas_call(
        _dispatch_kernel,
        grid=(nt,),
        in_specs=[pl.BlockSpec((None, 1, TOP_K * DISPATCH_TILE), lambda i: (i, 0, 0), memory_space=pltpu.SMEM),
                  pl.BlockSpec(memory_space=pl.ANY)],
        out_specs=pl.BlockSpec(memory_space=pl.ANY),
        out_shape=jax.ShapeDtypeStruct((n_slots * ROW_TILES, LANES), _f32),
        scratch_shapes=[pltpu.SemaphoreType.DMA(())],
        compiler_params=pltpu.CompilerParams(dimension_semantics=("arbitrary",)),
        name="dispatch_rows",
    )(dest_tiles, x1r)


def _moe_kernel(be_ref, nv_ref, src_ref, xs_ref, g2_ref, w1_ref, b1_ref, w2_ref, b2_ref, ys_ref, w1p, w2b):
    i = pl.program_id(0)
    nv = nv_ref[i]
    tm = EXPERT_TILE

    @pl.when(nv > 0)
    def _():
        changed = jnp.logical_or(i == 0, be_ref[i] != be_ref[jnp.maximum(i - 1, 0)])

        @pl.when(changed)
        def _():
            src = lax.broadcasted_iota(jnp.int32, (GLU_GROUP, GLU_GROUP), 0)
            dst = lax.broadcasted_iota(jnp.int32, (GLU_GROUP, GLU_GROUP), 1)
            half = GLU_GROUP // 2
            perm = (src == jnp.where(dst < half, 2 * dst, 2 * (dst - half) + 1)).astype(_bf16)
            for g in range(2 * D_FF // GLU_GROUP):
                cols = slice(g * GLU_GROUP, (g + 1) * GLU_GROUP)
                w1p[:, cols] = jnp.dot(w1_ref[0, :, cols].astype(_bf16), perm,
                                       preferred_element_type=_f32).astype(_bf16)
            w2b[...] = w2_ref[0].astype(_bf16)

        x = _load_token_rows(xs_ref, tm)
        x = jnp.where(lax.broadcasted_iota(jnp.int32, (tm, 1), 0) < nv, x, 0.0)
        h = _rms(x, g2_ref[...]).astype(_bf16)
        u = jnp.dot(h, w1p[...], preferred_element_type=_f32) + b1_ref[0]
        half = GLU_GROUP // 2
        hid = []
        for g in range(2 * D_FF // GLU_GROUP):
            glu = jnp.minimum(u[:, g * GLU_GROUP:g * GLU_GROUP + half], SWIGLU_LIMIT)
            lin = jnp.clip(u[:, g * GLU_GROUP + half:(g + 1) * GLU_GROUP], -SWIGLU_LIMIT, SWIGLU_LIMIT)
            hid.append(glu * (1.0 / (1.0 + jnp.exp(-SWIGLU_ALPHA * glu))) * (lin + 1.0))
        hid = jnp.concatenate(hid, axis=1).astype(_bf16)
        y = jnp.dot(hid, w2b[...], preferred_element_type=_f32) + b2_ref[0]
        _store_token_rows(ys_ref, y)


def _moe_call(block_e, n_valid, block_src, xs, g2, w1, b1p, w2, b2):
    nblk = block_e.shape[0]
    tm = EXPERT_TILE
    by_e = lambda i, be, nv, src: (be[i], 0, 0)
    by_src = lambda i, be, nv, src: (src[i], 0)
    fix = lambda i, be, nv, src: (0, 0)
    grid_spec = pltpu.PrefetchScalarGridSpec(
        num_scalar_prefetch=3,
        grid=(nblk,),
        in_specs=[pl.BlockSpec((tm * ROW_TILES, LANES), by_src), pl.BlockSpec((1, D_MODEL), fix),
                  pl.BlockSpec((1, D_MODEL, 2 * D_FF), by_e), pl.BlockSpec((1, 1, 2 * D_FF), by_e),
                  pl.BlockSpec((1, D_FF, D_MODEL), by_e), pl.BlockSpec((1, 1, D_MODEL), by_e)],
        out_specs=pl.BlockSpec((tm * ROW_TILES, LANES), by_src),
        scratch_shapes=[pltpu.VMEM((D_MODEL, 2 * D_FF), _bf16), pltpu.VMEM((D_FF, D_MODEL), _bf16)],
    )
    return pl.pallas_call(
        _moe_kernel,
        grid_spec=grid_spec,
        out_shape=jax.ShapeDtypeStruct(xs.shape, _f32),
        compiler_params=pltpu.CompilerParams(dimension_semantics=("arbitrary",),
                                             vmem_limit_bytes=VMEM_LIMIT),
        name="moe_experts",
    )(block_e, n_valid, block_src, xs, g2, w1, b1p, w2, b2)


def _combine_kernel(dcur_ref, dnext_ref, x1_ref, ys_hbm, g_ref, gf_ref, o_ref, gbuf, sem):
    tc = COMBINE_TILE
    i = pl.program_id(0)
    slot = i % 2

    def row_copy(d_ref, s, r, k):
        dst = gbuf.at[s, k, pl.ds(r * ROW_TILES, ROW_TILES)]
        return pltpu.make_async_copy(_token_row(ys_hbm, d_ref[0, k * tc + r]), dst, sem.at[s])

    def issue(d_ref, s):
        def group(g, c):
            for rr in range(DMA_GROUP):
                for k in range(TOP_K):
                    row_copy(d_ref, s, g * DMA_GROUP + rr, k).start()
            return c
        lax.fori_loop(0, tc // DMA_GROUP, group, 0)

    @pl.when(i == 0)
    def _():
        issue(dcur_ref, 0)

    @pl.when(i + 1 < pl.num_programs(0))
    def _():
        issue(dnext_ref, 1 - slot)

    def wait_group(g, c):
        for _ in range(DMA_GROUP * TOP_K):
            row_copy(dcur_ref, slot, 0, 0).wait()
        return c

    lax.fori_loop(0, tc // DMA_GROUP, wait_group, 0)

    g = g_ref[...]
    acc = _load_token_rows(x1_ref, tc)
    for k in range(TOP_K):
        acc = acc + g[:, k:k + 1] * _load_token_rows(gbuf.at[slot, k], tc)
    o_ref[...] = _rms(acc, gf_ref[...])


def _combine_call(dest_tiles, x1r, ys, gates_t, gf):
    nt = dest_tiles.shape[0]
    tc = COMBINE_TILE
    row = lambda i: (i, 0)
    return pl.pallas_call(
        _combine_kernel,
        grid=(nt,),
        in_specs=[pl.BlockSpec((None, 1, TOP_K * tc), lambda i: (i, 0, 0), memory_space=pltpu.SMEM),
                  pl.BlockSpec((None, 1, TOP_K * tc), lambda i: (jnp.minimum(i + 1, nt - 1), 0, 0),
                               memory_space=pltpu.SMEM),
                  pl.BlockSpec((tc * ROW_TILES, LANES), row), pl.BlockSpec(memory_space=pl.ANY),
                  pl.BlockSpec((tc, TOP_K), row), pl.BlockSpec((1, D_MODEL), lambda i: (0, 0))],
        out_specs=pl.BlockSpec((tc, D_MODEL), row),
        out_shape=jax.ShapeDtypeStruct((nt * tc, D_MODEL), _f32),
        scratch_shapes=[pltpu.VMEM((2, TOP_K, tc * ROW_TILES, LANES), _f32), pltpu.SemaphoreType.DMA((2,))],
        compiler_params=pltpu.CompilerParams(dimension_semantics=("arbitrary",),
                                             vmem_limit_bytes=VMEM_LIMIT),
        name="combine_norm",
    )(dest_tiles, dest_tiles, x1r, ys, gates_t, gf)


def _plan_blocks(route_i, counts, n_tokens):
    tm = EXPERT_TILE
    nblk = (n_tokens * TOP_K + N_EXPERTS * tm) // tm
    counts = counts.astype(jnp.int32)
    blocks_e = (counts + tm - 1) // tm
    blk_end = jnp.cumsum(blocks_e)
    pad_start = (blk_end - blocks_e) * tm
    experts = jnp.arange(N_EXPERTS, dtype=jnp.int32)
    hot = route_i[:TOP_K, :, None] == experts
    dest = jnp.sum(jnp.where(hot, pad_start, 0), axis=-1) + route_i[TOP_K:]
    n_used = blk_end[-1]
    blk = jnp.arange(nblk, dtype=jnp.int32)
    src = jnp.minimum(blk, jnp.maximum(n_used - 1, 0))
    be = jnp.minimum(jnp.sum((blk_end[None, :] <= src[:, None]).astype(jnp.int32), axis=1), N_EXPERTS - 1)
    hot_b = be[:, None] == experts
    cnt_b = jnp.sum(jnp.where(hot_b, counts, 0), axis=1)
    start_b = jnp.sum(jnp.where(hot_b, pad_start, 0), axis=1)
    n_valid = jnp.where(blk < n_used, jnp.clip(cnt_b - (src * tm - start_b), 0, tm), 0)
    return be, n_valid.astype(jnp.int32), src, dest


def _tile_dest(dest, tile):
    k, t = dest.shape
    return dest.reshape(k, t // tile, tile).transpose(1, 0, 2).reshape(t // tile, 1, k * tile)


def kernel(x, positions, norm1_g, w_in, b_in, sinks, gn_dil, gn_swa, w_out, b_out,
           norm2_g, w_router, b_router, w_mlp1, b_mlp1, w_mlp2, b_mlp2, norm_f_g):
    batch, seq, d = x.shape
    T = batch * seq
    x2 = x.reshape(T, d)
    depth = norm1_g.shape[0]
    assert depth == 1, "the final norm is fused into the last layer's combine kernel"
    inv = ROPE_THETA ** (-jnp.arange(0, HEAD_DIM, 2, dtype=_f32) / HEAD_DIM)
    inv = jnp.tile(inv, LANES // (HEAD_DIM // 2)).reshape(1, LANES)
    pos = positions.reshape(T, 1)
    for layer in range(depth):
        qa, ka, va, qb, kb, vb = _inproj_call(
            x2, pos, inv, norm1_g[layer].reshape(1, d), w_in[layer].astype(_bf16), b_in[layer].reshape(1, D_IN))
        o_dil = _attn_call(qa, ka, va, None, batch=batch, seq=seq,
                           configs=tuple((w // r, r) for w, r in DILATED_CONFIGS), kv_group=1)
        sink_lanes = jnp.repeat(sinks[layer].astype(_f32), HEAD_DIM).reshape(1, W_SWA)
        o_swa = _attn_call(qb, kb, vb, sink_lanes, batch=batch, seq=seq,
                           configs=((SWA_WINDOW - 1, 1),), kv_group=N_HEADS_SWA // N_KV_SWA // PAIR)
        x1r, route_i, route_g, counts = _outproj_call(
            o_dil, o_swa, x2, gn_dil[layer].reshape(1, W_DIL), gn_swa[layer].reshape(1, W_SWA),
            w_out[layer].astype(_bf16), b_out[layer].reshape(1, d), norm2_g[layer].reshape(1, d),
            w_router[layer].T, b_router[layer].reshape(N_EXPERTS, 1))
        block_e, n_valid, block_src, dest = _plan_blocks(route_i, counts[:, 0], T)
        xs = _dispatch_call(_tile_dest(dest, DISPATCH_TILE), x1r, block_e.shape[0] * EXPERT_TILE)
        half = GLU_GROUP // 2
        b1p = b_mlp1[layer].reshape(N_EXPERTS, 2 * D_FF // GLU_GROUP, half, 2)
        b1p = jnp.swapaxes(b1p, 2, 3).reshape(N_EXPERTS, 1, 2 * D_FF)
        ys = _moe_call(block_e, n_valid, block_src, xs, norm2_g[layer].reshape(1, d),
                       w_mlp1[layer], b1p, w_mlp2[layer], b_mlp2[layer].reshape(N_EXPERTS, 1, d))
        x2 = _combine_call(_tile_dest(dest, COMBINE_TILE), x1r, ys, route_g[:TOP_K].T, norm_f_g.reshape(1, d))
    return x2.reshape(batch, seq, d)
```

```python
import functools

import jax
import jax.numpy as jnp
from jax import lax
from jax.experimental import pallas as pl
from jax.experimental.pallas import tpu as pltpu

D_MODEL = 1024
HEAD_DIM = 64
N_HEADS_DIL = 8
DILATED_CONFIGS = ((128, 1), (512, 4), (2048, 16))
N_HEADS_SWA = 8
N_KV_SWA = 2
SWA_WINDOW = 128
ATTN_BLOCK = 128
ROPE_THETA = 10000.0
N_EXPERTS = 32
TOP_K = 4
D_FF = D_MODEL
SWIGLU_ALPHA = 1.702
SWIGLU_LIMIT = 7.0
NORM_EPS = 1e-5

W_DIL = N_HEADS_DIL * HEAD_DIM
W_SWA = N_HEADS_SWA * HEAD_DIM
W_KV_SWA = N_KV_SWA * HEAD_DIM
D_IN = 3 * W_DIL + W_SWA + 2 * W_KV_SWA

LANES = 128
PAIR = LANES // HEAD_DIM
SUBLANES = 8
ROW_TILES = D_MODEL // LANES
TOKEN_TILE = 512
EXPERT_TILE = 512
DISPATCH_TILE = 1024
COMBINE_TILE = 256
DMA_GROUP = 8
ATTN_UNROLL = 8
GLU_GROUP = 256
VMEM_LIMIT = 56 * 1024 * 1024

_f32 = jnp.float32
_bf16 = jnp.bfloat16


def _rms(x, g):
    return x * lax.rsqrt(jnp.mean(x * x, axis=-1, keepdims=True) + NORM_EPS) * g


def _store_token_rows(ref, x):
    tm = x.shape[0]
    for c in range(ROW_TILES):
        ref[pl.ds(c, tm, stride=ROW_TILES), :] = x[:, c * LANES:(c + 1) * LANES]


def _load_token_rows(ref, tm):
    return jnp.concatenate([ref[pl.ds(c, tm, stride=ROW_TILES), :] for c in range(ROW_TILES)], axis=1)


def _token_row(ref, tok):
    return ref.at[pl.ds(pl.multiple_of(tok * ROW_TILES, ROW_TILES), ROW_TILES)]


def _inproj_kernel(x_ref, pos_ref, inv_ref, g_ref, w_ref, b_ref,
                   qa_ref, ka_ref, va_ref, qb_ref, kb_ref, vb_ref):
    h = _rms(x_ref[...], g_ref[...]).astype(_bf16)
    proj = jnp.dot(h, w_ref[...], preferred_element_type=_f32) + b_ref[...]

    ang = pos_ref[...].astype(_f32) * inv_ref[...]
    lane = lax.broadcasted_iota(jnp.int32, ang.shape, 1)
    first_half = (lane % HEAD_DIM) < (HEAD_DIM // 2)
    cos = jnp.cos(ang)
    sin = jnp.where(first_half, -jnp.sin(ang), jnp.sin(ang))

    def rope(t, scale):
        outs = []
        for c in range(t.shape[1] // LANES):
            tc = t[:, c * LANES:(c + 1) * LANES]
            partner = jnp.where(first_half, pltpu.roll(tc, LANES - HEAD_DIM // 2, 1),
                                pltpu.roll(tc, HEAD_DIM // 2, 1))
            outs.append((tc * cos + partner * sin) * scale)
        return jnp.concatenate(outs, axis=1) if len(outs) > 1 else outs[0]

    scale = HEAD_DIM ** -0.5
    o = 0
    qa_ref[...] = rope(proj[:, o:o + W_DIL], scale).astype(_bf16); o += W_DIL
    ka_ref[...] = rope(proj[:, o:o + W_DIL], 1.0).astype(_bf16); o += W_DIL
    va_ref[...] = proj[:, o:o + W_DIL].astype(_bf16); o += W_DIL
    qb_ref[...] = rope(proj[:, o:o + W_SWA], scale).astype(_bf16); o += W_SWA
    kb = rope(proj[:, o:o + W_KV_SWA], 1.0); o += W_KV_SWA
    vb = proj[:, o:o + W_KV_SWA]

    def dup_heads(t):
        sw = pltpu.roll(t, HEAD_DIM, 1)
        lo = lane < HEAD_DIM
        return jnp.concatenate([jnp.where(lo, t, sw), jnp.where(lo, sw, t)], axis=1)

    kb_ref[...] = dup_heads(kb).astype(_bf16)
    vb_ref[...] = dup_heads(vb).astype(_bf16)


def _inproj_call(x2, pos, inv, g, w, b):
    T = x2.shape[0]
    tm = TOKEN_TILE
    row = lambda i: (i, 0)
    fix = lambda i: (0, 0)
    widths = (W_DIL, W_DIL, W_DIL, W_SWA, PAIR * W_KV_SWA, PAIR * W_KV_SWA)
    return pl.pallas_call(
        _inproj_kernel,
        grid=(T // tm,),
        in_specs=[pl.BlockSpec((tm, D_MODEL), row), pl.BlockSpec((tm, 1), row),
                  pl.BlockSpec((1, LANES), fix), pl.BlockSpec((1, D_MODEL), fix),
                  pl.BlockSpec((D_MODEL, D_IN), fix), pl.BlockSpec((1, D_IN), fix)],
        out_specs=[pl.BlockSpec((tm, wd), row) for wd in widths],
        out_shape=[jax.ShapeDtypeStruct((T, wd), _bf16) for wd in widths],
        compiler_params=pltpu.CompilerParams(dimension_semantics=("parallel",),
                                             vmem_limit_bytes=VMEM_LIMIT),
        name="inproj",
    )(x2, pos, inv, g, w, b)


def _attn_kernel(*refs, configs, seq, with_sink):
    if with_sink:
        q_ref, k_ref, v_ref, sink_ref, o_ref = refs[:5]
        scratch = refs[5:]
    else:
        q_ref, k_ref, v_ref, o_ref = refs[:4]
        scratch = refs[4:]
    qf, kf, vf = scratch[:3]
    obufs = scratch[3:3 + len(configs)]
    lbufs = scratch[3 + len(configs):]
    blk = ATTN_BLOCK

    qf[...] = q_ref[...].astype(_f32)
    kf[...] = k_ref[...].astype(_f32)
    vf[...] = v_ref[...].astype(_f32)

    lane = lax.broadcasted_iota(jnp.int32, (blk, LANES), 1)
    head0 = lane < HEAD_DIM
    qi = lax.broadcasted_iota(jnp.int32, (blk, 2 * blk), 0)
    kj = lax.broadcasted_iota(jnp.int32, (blk, 2 * blk), 1)
    dist = qi - kj + blk

    for ci, (max_dist, r) in enumerate(configs):
        nb = seq // r // blk
        band = (dist >= 0) & (dist <= max_dist)
        obuf, lbuf = obufs[ci], lbufs[ci]

        def rows(start, r=r):
            if r == 1:
                return pl.ds(pl.multiple_of(start, blk), blk)
            return pl.ds(start, blk, stride=r)

        def body(n, carry, r=r, nb=nb, band=band, obuf=obuf, lbuf=lbuf, rows=rows):
            c = n // nb
            j = n % nb
            start = c + (r * blk) * j
            prev = jnp.maximum(start - r * blk, c)
            q = qf[rows(start), :]
            k2 = jnp.concatenate([kf[rows(prev), :], kf[rows(start), :]], axis=0).astype(_bf16)
            v2 = jnp.concatenate([vf[rows(prev), :], vf[rows(start), :]], axis=0).astype(_bf16)
            mask = band & (kj >= jnp.where(j > 0, 0, blk))
            outs, lses = [], []
            for hm in (head0, jnp.logical_not(head0)):
                qh = jnp.where(hm, q, 0.0).astype(_bf16)
                s = lax.dot_general(qh, k2, (((1,), (1,)), ((), ())), preferred_element_type=_f32)
                s = jnp.where(mask, s, -jnp.inf)
                m = jnp.max(s, axis=-1, keepdims=True)
                p = jnp.exp(s - m)
                l = jnp.sum(p, axis=-1, keepdims=True)
                o = jnp.dot(p.astype(_bf16), v2, preferred_element_type=_f32) / l
                outs.append(o)
                lses.append(jnp.broadcast_to(m + jnp.log(l), (blk, LANES)))
            obuf[rows(start), :] = jnp.where(head0, outs[0], outs[1])
            lbuf[rows(start), :] = jnp.where(head0, lses[0], lses[1])
            return carry

        lax.fori_loop(0, seq // blk, body, 0, unroll=ATTN_UNROLL)

    chunk = 4 * blk

    def finish(i, carry):
        sl = pl.ds(pl.multiple_of(i * chunk, chunk), chunk)
        if with_sink:
            o = obufs[0][sl, :] * (1.0 / (1.0 + jnp.exp(sink_ref[...] - lbufs[0][sl, :])))
        else:
            ls = [lb[sl, :] for lb in lbufs]
            mx = functools.reduce(jnp.maximum, ls)
            es = [jnp.exp(x - mx) for x in ls]
            den = functools.reduce(lambda a, b: a + b, es)
            o = functools.reduce(lambda a, b: a + b,
                                 [e / den * ob[sl, :] for e, ob in zip(es, obufs)])
        o_ref[sl, :] = o
        return carry

    lax.fori_loop(0, seq // chunk, finish, 0)


def _attn_call(q, k, v, sinks, *, batch, seq, configs, kv_group):
    n_pairs = q.shape[1] // LANES
    with_sink = sinks is not None
    qmap = lambda b, p: (b, p)
    kvmap = lambda b, p: (b, p // kv_group)
    in_specs = [pl.BlockSpec((seq, LANES), qmap), pl.BlockSpec((seq, LANES), kvmap),
                pl.BlockSpec((seq, LANES), kvmap)]
    args = [q, k, v]
    if with_sink:
        in_specs.append(pl.BlockSpec((1, LANES), lambda b, p: (0, p)))
        args.append(sinks)
    n_buf = 3 + 2 * len(configs)
    return pl.pallas_call(
        functools.partial(_attn_kernel, configs=configs, seq=seq, with_sink=with_sink),
        grid=(batch, n_pairs),
        in_specs=in_specs,
        out_specs=pl.BlockSpec((seq, LANES), qmap),
        out_shape=jax.ShapeDtypeStruct(q.shape, _f32),
        scratch_shapes=[pltpu.VMEM((seq, LANES), _f32) for _ in range(n_buf)],
        compiler_params=pltpu.CompilerParams(dimension_semantics=("parallel", "parallel"),
                                             vmem_limit_bytes=VMEM_LIMIT),
        name="attn_sink" if with_sink else "attn_dilated",
    )(*args)


def _outproj_kernel(od_ref, os_ref, x_ref, gd_ref, gs_ref, w_ref, b_ref, g2_ref, wr_ref, br_ref,
                    x1_ref, ri_ref, rg_ref, cnt_ref, carry_ref):
    tm = x_ref.shape[0]

    @pl.when(pl.program_id(0) == 0)
    def _():
        carry_ref[...] = jnp.zeros_like(carry_ref)

    mixed = jnp.concatenate([_rms(od_ref[...], gd_ref[...]), _rms(os_ref[...], gs_ref[...])], axis=1)
    x1 = x_ref[...] + jnp.dot(mixed.astype(_bf16), w_ref[...], preferred_element_type=_f32) + b_ref[...]
    _store_token_rows(x1_ref, x1)

    h2 = _rms(x1, g2_ref[...])
    logits = lax.dot_general(wr_ref[...], h2, (((1,), (1,)), ((), ())),
                             precision=lax.Precision.HIGHEST, preferred_element_type=_f32) + br_ref[...]
    eiota = lax.broadcasted_iota(jnp.int32, logits.shape, 0).astype(_f32)
    vals, idxs, sels = [], [], []
    l = logits
    for _ in range(TOP_K):
        m = jnp.max(l, axis=0, keepdims=True)
        idx = jnp.min(jnp.where(l == m, eiota, float(N_EXPERTS)), axis=0, keepdims=True)
        sel = eiota == idx
        vals.append(m); idxs.append(idx); sels.append(sel)
        l = jnp.where(sel, -jnp.inf, l)
    exps = [jnp.exp(v - vals[0]) for v in vals]
    den = functools.reduce(lambda a, b: a + b, exps)
    gates = [e / den for e in exps]

    chosen = functools.reduce(jnp.logical_or, sels).astype(_f32)
    ti = lax.broadcasted_iota(jnp.int32, (tm, tm), 0)
    tj = lax.broadcasted_iota(jnp.int32, (tm, tm), 1)
    before = (ti < tj).astype(_bf16)
    prefix = jnp.dot(chosen.astype(_bf16), before, preferred_element_type=_f32) + carry_ref[...]
    ranks = [jnp.sum(jnp.where(s, prefix, 0.0), axis=0, keepdims=True).astype(jnp.int32) for s in sels]
    carry_ref[...] = carry_ref[...] + jnp.sum(chosen, axis=1, keepdims=True)

    ri_ref[...] = jnp.concatenate([ix.astype(jnp.int32) for ix in idxs] + ranks, axis=0)
    rg_ref[...] = jnp.concatenate(gates + gates, axis=0)
    cnt_ref[...] = jnp.broadcast_to(carry_ref[...], cnt_ref.shape)


def _outproj_call(od, osw, x2, gd, gs, w, b, g2, wr_t, br):
    T = x2.shape[0]
    tm = TOKEN_TILE
    row = lambda i: (i, 0)
    col = lambda i: (0, i)
    fix = lambda i: (0, 0)
    return pl.pallas_call(
        _outproj_kernel,
        grid=(T // tm,),
        in_specs=[pl.BlockSpec((tm, W_DIL), row), pl.BlockSpec((tm, W_SWA), row),
                  pl.BlockSpec((tm, D_MODEL), row),
                  pl.BlockSpec((1, W_DIL), fix), pl.BlockSpec((1, W_SWA), fix),
                  pl.BlockSpec((D_MODEL, D_MODEL), fix), pl.BlockSpec((1, D_MODEL), fix),
                  pl.BlockSpec((1, D_MODEL), fix),
                  pl.BlockSpec((N_EXPERTS, D_MODEL), fix), pl.BlockSpec((N_EXPERTS, 1), fix)],
        out_specs=[pl.BlockSpec((tm * ROW_TILES, LANES), row), pl.BlockSpec((2 * TOP_K, tm), col),
                   pl.BlockSpec((2 * TOP_K, tm), col), pl.BlockSpec((N_EXPERTS, LANES), fix)],
        out_shape=[jax.ShapeDtypeStruct((T * ROW_TILES, LANES), _f32),
                   jax.ShapeDtypeStruct((2 * TOP_K, T), jnp.int32),
                   jax.ShapeDtypeStruct((2 * TOP_K, T), _f32),
                   jax.ShapeDtypeStruct((N_EXPERTS, LANES), _f32)],
        scratch_shapes=[pltpu.VMEM((N_EXPERTS, 1), _f32)],
        compiler_params=pltpu.CompilerParams(dimension_semantics=("arbitrary",),
                                             vmem_limit_bytes=VMEM_LIMIT),
        name="outproj_router",
    )(od, osw, x2, gd, gs, w, b, g2, wr_t, br)


def _dispatch_kernel(d_ref, x1_ref, xs_hbm, sem):
    td = DISPATCH_TILE

    def start_group(g, c):
        for rr in range(DMA_GROUP):
            r = g * DMA_GROUP + rr
            src = _token_row(x1_ref, r)
            for k in range(TOP_K):
                pltpu.make_async_copy(src, _token_row(xs_hbm, d_ref[0, k * td + r]), sem).start(priority=k % 2)
        return c

    lax.fori_loop(0, td // DMA_GROUP, start_group, 0)

    def wait_group(g, c):
        for _ in range(DMA_GROUP * TOP_K):
            pltpu.make_async_copy(_token_row(x1_ref, 0), _token_row(xs_hbm, 0), sem).wait()
        return c

    lax.fori_loop(0, td // DMA_GROUP, wait_group, 0)


def _dispatch_call(dest_tiles, x1r, n_slots):
    nt = dest_tiles.shape[0]
    return pl.pallas_call(
        _dispatch_kernel,
        grid=(nt,),
        in_specs=[pl.BlockSpec((None, 1, TOP_K * DISPATCH_TILE), lambda i: (i, 0, 0), memory_space=pltpu.SMEM),
                  pl.BlockSpec((DISPATCH_TILE * ROW_TILES, LANES), lambda i: (i, 0))],
        out_specs=pl.BlockSpec(memory_space=pl.ANY),
        out_shape=jax.ShapeDtypeStruct((n_slots * ROW_TILES, LANES), _f32),
        scratch_shapes=[pltpu.SemaphoreType.DMA(())],
        compiler_params=pltpu.CompilerParams(dimension_semantics=("arbitrary",)),
        name="dispatch_rows",
    )(dest_tiles, x1r)


def _moe_kernel(be_ref, nv_ref, src_ref, xs_ref, g2_ref, w1_ref, b1_ref, w2_ref, b2_ref, ys_ref, w1p, w2b):
    i = pl.program_id(0)
    nv = nv_ref[i]
    tm = EXPERT_TILE

    @pl.when(nv > 0)
    def _():
        changed = jnp.logical_or(i == 0, be_ref[i] != be_ref[jnp.maximum(i - 1, 0)])

        @pl.when(changed)
        def _():
            src = lax.broadcasted_iota(jnp.int32, (GLU_GROUP, GLU_GROUP), 0)
            dst = lax.broadcasted_iota(jnp.int32, (GLU_GROUP, GLU_GROUP), 1)
            half = GLU_GROUP // 2
            perm = (src == jnp.where(dst < half, 2 * dst, 2 * (dst - half) + 1)).astype(_bf16)
            for g in range(2 * D_FF // GLU_GROUP):
                cols = slice(g * GLU_GROUP, (g + 1) * GLU_GROUP)
                w1p[:, cols] = jnp.dot(w1_ref[0, :, cols].astype(_bf16), perm,
                                       preferred_element_type=_f32).astype(_bf16)
            w2b[...] = w2_ref[0].astype(_bf16)

        x = _load_token_rows(xs_ref, tm)
        x = jnp.where(lax.broadcasted_iota(jnp.int32, (tm, 1), 0) < nv, x, 0.0)
        h = _rms(x, g2_ref[...]).astype(_bf16)
        u = jnp.dot(h, w1p[...], preferred_element_type=_f32) + b1_ref[0]
        half = GLU_GROUP // 2
        hid = []
        for g in range(2 * D_FF // GLU_GROUP):
            glu = jnp.minimum(u[:, g * GLU_GROUP:g * GLU_GROUP + half], SWIGLU_LIMIT)
            lin = jnp.clip(u[:, g * GLU_GROUP + half:(g + 1) * GLU_GROUP], -SWIGLU_LIMIT, SWIGLU_LIMIT)
            hid.append(glu * (1.0 / (1.0 + jnp.exp(-SWIGLU_ALPHA * glu))) * (lin + 1.0))
        hid = jnp.concatenate(hid, axis=1).astype(_bf16)
        y = jnp.dot(hid, w2b[...], preferred_element_type=_f32) + b2_ref[0]
        _store_token_rows(ys_ref, y)


def _moe_call(block_e, n_valid, block_src, xs, g2, w1, b1p, w2, b2):
    nblk = block_e.shape[0]
    tm = EXPERT_TILE
    by_e = lambda i, be, nv, src: (be[i], 0, 0)
    by_src = lambda i, be, nv, src: (src[i], 0)
    fix = lambda i, be, nv, src: (0, 0)
    grid_spec = pltpu.PrefetchScalarGridSpec(
        num_scalar_prefetch=3,
        grid=(nblk,),
        in_specs=[pl.BlockSpec((tm * ROW_TILES, LANES), by_src), pl.BlockSpec((1, D_MODEL), fix),
                  pl.BlockSpec((1, D_MODEL, 2 * D_FF), by_e), pl.BlockSpec((1, 1, 2 * D_FF), by_e),
                  pl.BlockSpec((1, D_FF, D_MODEL), by_e), pl.BlockSpec((1, 1, D_MODEL), by_e)],
        out_specs=pl.BlockSpec((tm * ROW_TILES, LANES), by_src),
        scratch_shapes=[pltpu.VMEM((D_MODEL, 2 * D_FF), _bf16), pltpu.VMEM((D_FF, D_MODEL), _bf16)],
    )
    return pl.pallas_call(
        _moe_kernel,
        grid_spec=grid_spec,
        out_shape=jax.ShapeDtypeStruct(xs.shape, _f32),
        compiler_params=pltpu.CompilerParams(dimension_semantics=("arbitrary",),
                                             vmem_limit_bytes=VMEM_LIMIT),
        name="moe_experts",
    )(block_e, n_valid, block_src, xs, g2, w1, b1p, w2, b2)


def _combine_kernel(dcur_ref, dnext_ref, x1_ref, ys_hbm, g_ref, gf_ref, o_ref, gbuf, sem):
    tc = COMBINE_TILE
    i = pl.program_id(0)
    slot = i % 2

    def row_copy(d_ref, s, r, k):
        dst = gbuf.at[s, k, pl.ds(r * ROW_TILES, ROW_TILES)]
        return pltpu.make_async_copy(_token_row(ys_hbm, d_ref[0, k * tc + r]), dst, sem.at[s])

    def issue(d_ref, s):
        def group(g, c):
            for rr in range(DMA_GROUP):
                for k in range(TOP_K):
                    row_copy(d_ref, s, g * DMA_GROUP + rr, k).start(priority=k % 2)
            return c
        lax.fori_loop(0, tc // DMA_GROUP, group, 0)

    @pl.when(i == 0)
    def _():
        issue(dcur_ref, 0)

    @pl.when(i + 1 < pl.num_programs(0))
    def _():
        issue(dnext_ref, 1 - slot)

    def wait_group(g, c):
        for _ in range(DMA_GROUP * TOP_K):
            row_copy(dcur_ref, slot, 0, 0).wait()
        return c

    lax.fori_loop(0, tc // DMA_GROUP, wait_group, 0)

    g = g_ref[...]
    acc = _load_token_rows(x1_ref, tc)
    for k in range(TOP_K):
        acc = acc + g[:, k:k + 1] * _load_token_rows(gbuf.at[slot, k], tc)
    o_ref[...] = _rms(acc, gf_ref[...])


def _combine_call(dest_tiles, x1r, ys, gates_t, gf):
    nt = dest_tiles.shape[0]
    tc = COMBINE_TILE
    row = lambda i: (i, 0)
    return pl.pallas_call(
        _combine_kernel,
        grid=(nt,),
        in_specs=[pl.BlockSpec((None, 1, TOP_K * tc), lambda i: (i, 0, 0), memory_space=pltpu.SMEM),
                  pl.BlockSpec((None, 1, TOP_K * tc), lambda i: (jnp.minimum(i + 1, nt - 1), 0, 0),
                               memory_space=pltpu.SMEM),
                  pl.BlockSpec((tc * ROW_TILES, LANES), row), pl.BlockSpec(memory_space=pl.ANY),
                  pl.BlockSpec((tc, TOP_K), row), pl.BlockSpec((1, D_MODEL), lambda i: (0, 0))],
        out_specs=pl.BlockSpec((tc, D_MODEL), row),
        out_shape=jax.ShapeDtypeStruct((nt * tc, D_MODEL), _f32),
        scratch_shapes=[pltpu.VMEM((2, TOP_K, tc * ROW_TILES, LANES), _f32), pltpu.SemaphoreType.DMA((2,))],
        compiler_params=pltpu.CompilerParams(dimension_semantics=("arbitrary",),
                                             vmem_limit_bytes=VMEM_LIMIT),
        name="combine_norm",
    )(dest_tiles, dest_tiles, x1r, ys, gates_t, gf)


def _plan_blocks(route_i, counts, n_tokens):
    tm = EXPERT_TILE
    nblk = (n_tokens * TOP_K + N_EXPERTS * tm) // tm
    counts = counts.astype(jnp.int32)
    blocks_e = (counts + tm - 1) // tm
    blk_end = jnp.cumsum(blocks_e)
    pad_start = (blk_end - blocks_e) * tm
    experts = jnp.arange(N_EXPERTS, dtype=jnp.int32)
    hot = route_i[:TOP_K, :, None] == experts
    dest = jnp.sum(jnp.where(hot, pad_start, 0), axis=-1) + route_i[TOP_K:]
    n_used = blk_end[-1]
    blk = jnp.arange(nblk, dtype=jnp.int32)
    src = jnp.minimum(blk, jnp.maximum(n_used - 1, 0))
    be = jnp.minimum(jnp.sum((blk_end[None, :] <= src[:, None]).astype(jnp.int32), axis=1), N_EXPERTS - 1)
    hot_b = be[:, None] == experts
    cnt_b = jnp.sum(jnp.where(hot_b, counts, 0), axis=1)
    start_b = jnp.sum(jnp.where(hot_b, pad_start, 0), axis=1)
    n_valid = jnp.where(blk < n_used, jnp.clip(cnt_b - (src * tm - start_b), 0, tm), 0)
    return be, n_valid.astype(jnp.int32), src, dest


def _tile_dest(dest, tile):
    k, t = dest.shape
    return dest.reshape(k, t // tile, tile).transpose(1, 0, 2).reshape(t // tile, 1, k * tile)


def kernel(x, positions, norm1_g, w_in, b_in, sinks, gn_dil, gn_swa, w_out, b_out,
           norm2_g, w_router, b_router, w_mlp1, b_mlp1, w_mlp2, b_mlp2, norm_f_g):
    batch, seq, d = x.shape
    T = batch * seq
    x2 = x.reshape(T, d)
    depth = norm1_g.shape[0]
    assert depth == 1, "the final norm is fused into the last layer's combine kernel"
    inv = ROPE_THETA ** (-jnp.arange(0, HEAD_DIM, 2, dtype=_f32) / HEAD_DIM)
    inv = jnp.tile(inv, LANES // (HEAD_DIM // 2)).reshape(1, LANES)
    pos = positions.reshape(T, 1)
    for layer in range(depth):
        qa, ka, va, qb, kb, vb = _inproj_call(
            x2, pos, inv, norm1_g[layer].reshape(1, d), w_in[layer].astype(_bf16), b_in[layer].reshape(1, D_IN))
        o_dil = _attn_call(qa, ka, va, None, batch=batch, seq=seq,
                           configs=tuple((w // r, r) for w, r in DILATED_CONFIGS), kv_group=1)
        sink_lanes = jnp.repeat(sinks[layer].astype(_f32), HEAD_DIM).reshape(1, W_SWA)
        o_swa = _attn_call(qb, kb, vb, sink_lanes, batch=batch, seq=seq,
                           configs=((SWA_WINDOW - 1, 1),), kv_group=N_HEADS_SWA // N_KV_SWA // PAIR)
        x1r, route_i, route_g, counts = _outproj_call(
            o_dil, o_swa, x2, gn_dil[layer].reshape(1, W_DIL), gn_swa[layer].reshape(1, W_SWA),
            w_out[layer].astype(_bf16), b_out[layer].reshape(1, d), norm2_g[layer].reshape(1, d),
            w_router[layer].T, b_router[layer].reshape(N_EXPERTS, 1))
        block_e, n_valid, block_src, dest = _plan_blocks(route_i, counts[:, 0], T)
        xs = _dispatch_call(_tile_dest(dest, DISPATCH_TILE), x1r, block_e.shape[0] * EXPERT_TILE)
        half = GLU_GROUP // 2
        b1p = b_mlp1[layer].reshape(N_EXPERTS, 2 * D_FF // GLU_GROUP, half, 2)
        b1p = jnp.swapaxes(b1p, 2, 3).reshape(N_EXPERTS, 1, 2 * D_FF)
        ys = _moe_call(block_e, n_valid, block_src, xs, norm2_g[layer].reshape(1, d),
                       w_mlp1[layer], b1p, w_mlp2[layer], b_mlp2[layer].reshape(N_EXPERTS, 1, d))
        x2 = _combine_call(_tile_dest(dest, COMBINE_TILE), x1r, ys, route_g[:TOP_K].T, norm_f_g.reshape(1, d))
    return x2.reshape(batch, seq, d)
```

```python
import functools

import jax
import jax.numpy as jnp
from jax import lax
from jax.experimental import pallas as pl
from jax.experimental.pallas import tpu as pltpu

D_MODEL = 1024
HEAD_DIM = 64
N_HEADS_DIL = 8
DILATED_CONFIGS = ((128, 1), (512, 4), (2048, 16))
N_HEADS_SWA = 8
N_KV_SWA = 2
SWA_WINDOW = 128
ATTN_BLOCK = 128
ROPE_THETA = 10000.0
N_EXPERTS = 32
TOP_K = 4
D_FF = D_MODEL
SWIGLU_ALPHA = 1.702
SWIGLU_LIMIT = 7.0
NORM_EPS = 1e-5

W_DIL = N_HEADS_DIL * HEAD_DIM
W_SWA = N_HEADS_SWA * HEAD_DIM
W_KV_SWA = N_KV_SWA * HEAD_DIM
D_IN = 3 * W_DIL + W_SWA + 2 * W_KV_SWA

LANES = 128
PAIR = LANES // HEAD_DIM
SUBLANES = 8
ROW_TILES = D_MODEL // LANES
TOKEN_TILE = 512
EXPERT_TILE = 512
DISPATCH_TILE = 1024
COMBINE_TILE = 256
DMA_GROUP = 8
COMBINE_GROUP = 64
ATTN_UNROLL = 8
MOE_CHAINS = 2
INPROJ_CHAINS = 2
OUTPROJ_CHAINS = 2
GLU_GROUP = 256
VMEM_LIMIT = 56 * 1024 * 1024

_f32 = jnp.float32
_bf16 = jnp.bfloat16


def _rms(x, g):
    return x * lax.rsqrt(jnp.mean(x * x, axis=-1, keepdims=True) + NORM_EPS) * g


def _store_token_rows(ref, x):
    tm = x.shape[0]
    for c in range(ROW_TILES):
        ref[pl.ds(c, tm, stride=ROW_TILES), :] = x[:, c * LANES:(c + 1) * LANES]


def _load_token_rows(ref, tm):
    return jnp.concatenate([ref[pl.ds(c, tm, stride=ROW_TILES), :] for c in range(ROW_TILES)], axis=1)


def _token_row(ref, tok):
    return ref.at[pl.ds(pl.multiple_of(tok * ROW_TILES, ROW_TILES), ROW_TILES)]


def _inproj_kernel(x_ref, pos_ref, inv_ref, g_ref, w_ref, b_ref, *out_refs):
    sub = x_ref.shape[0] // INPROJ_CHAINS
    for s in range(INPROJ_CHAINS):
        rows = slice(s * sub, (s + 1) * sub)
        _inproj_rows(x_ref[rows, :], pos_ref[rows, :], inv_ref, g_ref, w_ref, b_ref,
                     [r.at[rows, :] for r in out_refs])


def _inproj_rows(x, pos, inv_ref, g_ref, w_ref, b_ref, out_refs):
    qa_ref, ka_ref, va_ref, qb_ref, kb_ref, vb_ref = out_refs
    h = _rms(x, g_ref[...]).astype(_bf16)
    proj = jnp.dot(h, w_ref[...], preferred_element_type=_f32) + b_ref[...]

    ang = pos.astype(_f32) * inv_ref[...]
    lane = lax.broadcasted_iota(jnp.int32, ang.shape, 1)
    first_half = (lane % HEAD_DIM) < (HEAD_DIM // 2)
    cos = jnp.cos(ang)
    sin = jnp.where(first_half, -jnp.sin(ang), jnp.sin(ang))

    def rope(t, scale):
        outs = []
        for c in range(t.shape[1] // LANES):
            tc = t[:, c * LANES:(c + 1) * LANES]
            partner = jnp.where(first_half, pltpu.roll(tc, LANES - HEAD_DIM // 2, 1),
                                pltpu.roll(tc, HEAD_DIM // 2, 1))
            outs.append((tc * cos + partner * sin) * scale)
        return jnp.concatenate(outs, axis=1) if len(outs) > 1 else outs[0]

    scale = HEAD_DIM ** -0.5
    o = 0
    qa_ref[...] = rope(proj[:, o:o + W_DIL], scale).astype(_bf16); o += W_DIL
    ka_ref[...] = rope(proj[:, o:o + W_DIL], 1.0).astype(_bf16); o += W_DIL
    va_ref[...] = proj[:, o:o + W_DIL].astype(_bf16); o += W_DIL
    qb_ref[...] = rope(proj[:, o:o + W_SWA], scale).astype(_bf16); o += W_SWA
    kb = rope(proj[:, o:o + W_KV_SWA], 1.0); o += W_KV_SWA
    vb = proj[:, o:o + W_KV_SWA]

    def dup_heads(t):
        sw = pltpu.roll(t, HEAD_DIM, 1)
        lo = lane < HEAD_DIM
        return jnp.concatenate([jnp.where(lo, t, sw), jnp.where(lo, sw, t)], axis=1)

    kb_ref[...] = dup_heads(kb).astype(_bf16)
    vb_ref[...] = dup_heads(vb).astype(_bf16)


def _inproj_call(x2, pos, inv, g, w, b):
    T = x2.shape[0]
    tm = TOKEN_TILE
    row = lambda i: (i, 0)
    fix = lambda i: (0, 0)
    widths = (W_DIL, W_DIL, W_DIL, W_SWA, PAIR * W_KV_SWA, PAIR * W_KV_SWA)
    return pl.pallas_call(
        _inproj_kernel,
        grid=(T // tm,),
        in_specs=[pl.BlockSpec((tm, D_MODEL), row), pl.BlockSpec((tm, 1), row),
                  pl.BlockSpec((1, LANES), fix), pl.BlockSpec((1, D_MODEL), fix),
                  pl.BlockSpec((D_MODEL, D_IN), fix), pl.BlockSpec((1, D_IN), fix)],
        out_specs=[pl.BlockSpec((tm, wd), row) for wd in widths],
        out_shape=[jax.ShapeDtypeStruct((T, wd), _bf16) for wd in widths],
        compiler_params=pltpu.CompilerParams(dimension_semantics=("parallel",),
                                             vmem_limit_bytes=VMEM_LIMIT),
        name="inproj",
    )(x2, pos, inv, g, w, b)


def _attn_kernel(*refs, configs, seq, with_sink):
    if with_sink:
        q_ref, k_ref, v_ref, sink_ref, o_ref = refs[:5]
        scratch = refs[5:]
    else:
        q_ref, k_ref, v_ref, o_ref = refs[:4]
        scratch = refs[4:]
    qf, kf, vf = scratch[:3]
    obufs = scratch[3:3 + len(configs)]
    lbufs = scratch[3 + len(configs):]
    blk = ATTN_BLOCK

    qf[...] = q_ref[...].astype(_f32)
    kf[...] = k_ref[...].astype(_f32)
    vf[...] = v_ref[...].astype(_f32)

    lane = lax.broadcasted_iota(jnp.int32, (blk, LANES), 1)
    head0 = lane < HEAD_DIM
    qi = lax.broadcasted_iota(jnp.int32, (blk, 2 * blk), 0)
    kj = lax.broadcasted_iota(jnp.int32, (blk, 2 * blk), 1)
    dist = qi - kj + blk

    for ci, (max_dist, r) in enumerate(configs):
        nb = seq // r // blk
        band = (dist >= 0) & (dist <= max_dist)
        obuf, lbuf = obufs[ci], lbufs[ci]

        def rows(start, r=r):
            if r == 1:
                return pl.ds(pl.multiple_of(start, blk), blk)
            return pl.ds(start, blk, stride=r)

        def body(n, carry, r=r, nb=nb, band=band, obuf=obuf, lbuf=lbuf, rows=rows):
            c = n // nb
            j = n % nb
            start = c + (r * blk) * j
            prev = jnp.maximum(start - r * blk, c)
            q = qf[rows(start), :]
            k2 = jnp.concatenate([kf[rows(prev), :], kf[rows(start), :]], axis=0).astype(_bf16)
            v2 = jnp.concatenate([vf[rows(prev), :], vf[rows(start), :]], axis=0).astype(_bf16)
            mask = band & (kj >= jnp.where(j > 0, 0, blk))
            outs, lses = [], []
            for hm in (head0, jnp.logical_not(head0)):
                qh = jnp.where(hm, q, 0.0).astype(_bf16)
                s = lax.dot_general(qh, k2, (((1,), (1,)), ((), ())), preferred_element_type=_f32)
                s = jnp.where(mask, s, -jnp.inf)
                m = jnp.max(s, axis=-1, keepdims=True)
                p = jnp.exp(s - m)
                l = jnp.sum(p, axis=-1, keepdims=True)
                o = jnp.dot(p.astype(_bf16), v2, preferred_element_type=_f32) / l
                outs.append(o)
                lses.append(jnp.broadcast_to(m + jnp.log(l), (blk, LANES)))
            obuf[rows(start), :] = jnp.where(head0, outs[0], outs[1])
            lbuf[rows(start), :] = jnp.where(head0, lses[0], lses[1])
            return carry

        lax.fori_loop(0, seq // blk, body, 0, unroll=ATTN_UNROLL)

    chunk = 4 * blk

    def finish(i, carry):
        sl = pl.ds(pl.multiple_of(i * chunk, chunk), chunk)
        if with_sink:
            o = obufs[0][sl, :] * (1.0 / (1.0 + jnp.exp(sink_ref[...] - lbufs[0][sl, :])))
        else:
            ls = [lb[sl, :] for lb in lbufs]
            mx = functools.reduce(jnp.maximum, ls)
            es = [jnp.exp(x - mx) for x in ls]
            den = functools.reduce(lambda a, b: a + b, es)
            o = functools.reduce(lambda a, b: a + b,
                                 [e / den * ob[sl, :] for e, ob in zip(es, obufs)])
        o_ref[sl, :] = o
        return carry

    lax.fori_loop(0, seq // chunk, finish, 0)


def _attn_call(q, k, v, sinks, *, batch, seq, configs, kv_group):
    n_pairs = q.shape[1] // LANES
    with_sink = sinks is not None
    qmap = lambda b, p: (b, p)
    kvmap = lambda b, p: (b, p // kv_group)
    in_specs = [pl.BlockSpec((seq, LANES), qmap), pl.BlockSpec((seq, LANES), kvmap),
                pl.BlockSpec((seq, LANES), kvmap)]
    args = [q, k, v]
    if with_sink:
        in_specs.append(pl.BlockSpec((1, LANES), lambda b, p: (0, p)))
        args.append(sinks)
    n_buf = 3 + 2 * len(configs)
    return pl.pallas_call(
        functools.partial(_attn_kernel, configs=configs, seq=seq, with_sink=with_sink),
        grid=(batch, n_pairs),
        in_specs=in_specs,
        out_specs=pl.BlockSpec((seq, LANES), qmap),
        out_shape=jax.ShapeDtypeStruct(q.shape, _f32),
        scratch_shapes=[pltpu.VMEM((seq, LANES), _f32) for _ in range(n_buf)],
        compiler_params=pltpu.CompilerParams(dimension_semantics=("parallel", "parallel"),
                                             vmem_limit_bytes=VMEM_LIMIT),
        name="attn_sink" if with_sink else "attn_dilated",
    )(*args)


def _outproj_kernel(od_ref, os_ref, x_ref, gd_ref, gs_ref, w_ref, b_ref, g2_ref, wr_ref, br_ref,
                    x1_ref, ri_ref, rg_ref, cnt_ref, carry_ref):
    tm = x_ref.shape[0]

    @pl.when(pl.program_id(0) == 0)
    def _():
        carry_ref[...] = jnp.zeros_like(carry_ref)

    sub = tm // OUTPROJ_CHAINS
    h2 = []
    for s in range(OUTPROJ_CHAINS):
        rows = slice(s * sub, (s + 1) * sub)
        mixed = jnp.concatenate([_rms(od_ref[rows, :], gd_ref[...]), _rms(os_ref[rows, :], gs_ref[...])], axis=1)
        x1 = x_ref[rows, :] + jnp.dot(mixed.astype(_bf16), w_ref[...], preferred_element_type=_f32) + b_ref[...]
        _store_token_rows(x1_ref.at[pl.ds(s * sub * ROW_TILES, sub * ROW_TILES)], x1)
        h2.append(_rms(x1, g2_ref[...]))
    h2 = jnp.concatenate(h2, axis=0)
    logits = lax.dot_general(wr_ref[...], h2, (((1,), (1,)), ((), ())),
                             precision=lax.Precision.HIGHEST, preferred_element_type=_f32) + br_ref[...]
    eiota = lax.broadcasted_iota(jnp.int32, logits.shape, 0).astype(_f32)
    vals, idxs, sels = [], [], []
    l = logits
    for _ in range(TOP_K):
        m = jnp.max(l, axis=0, keepdims=True)
        idx = jnp.min(jnp.where(l == m, eiota, float(N_EXPERTS)), axis=0, keepdims=True)
        sel = eiota == idx
        vals.append(m); idxs.append(idx); sels.append(sel)
        l = jnp.where(sel, -jnp.inf, l)
    exps = [jnp.exp(v - vals[0]) for v in vals]
    den = functools.reduce(lambda a, b: a + b, exps)
    gates = [e / den for e in exps]

    chosen = functools.reduce(jnp.logical_or, sels).astype(_f32)
    ti = lax.broadcasted_iota(jnp.int32, (tm, tm), 0)
    tj = lax.broadcasted_iota(jnp.int32, (tm, tm), 1)
    before = (ti < tj).astype(_bf16)
    prefix = jnp.dot(chosen.astype(_bf16), before, preferred_element_type=_f32) + carry_ref[...]
    ranks = [jnp.sum(jnp.where(s, prefix, 0.0), axis=0, keepdims=True).astype(jnp.int32) for s in sels]
    carry_ref[...] = carry_ref[...] + jnp.sum(chosen, axis=1, keepdims=True)

    ri_ref[...] = jnp.concatenate([ix.astype(jnp.int32) for ix in idxs] + ranks, axis=0)
    rg_ref[...] = jnp.concatenate(gates + gates, axis=0)
    cnt_ref[...] = jnp.broadcast_to(carry_ref[...], cnt_ref.shape)


def _outproj_call(od, osw, x2, gd, gs, w, b, g2, wr_t, br):
    T = x2.shape[0]
    tm = TOKEN_TILE
    row = lambda i: (i, 0)
    col = lambda i: (0, i)
    fix = lambda i: (0, 0)
    return pl.pallas_call(
        _outproj_kernel,
        grid=(T // tm,),
        in_specs=[pl.BlockSpec((tm, W_DIL), row), pl.BlockSpec((tm, W_SWA), row),
                  pl.BlockSpec((tm, D_MODEL), row),
                  pl.BlockSpec((1, W_DIL), fix), pl.BlockSpec((1, W_SWA), fix),
                  pl.BlockSpec((D_MODEL, D_MODEL), fix), pl.BlockSpec((1, D_MODEL), fix),
                  pl.BlockSpec((1, D_MODEL), fix),
                  pl.BlockSpec((N_EXPERTS, D_MODEL), fix), pl.BlockSpec((N_EXPERTS, 1), fix)],
        out_specs=[pl.BlockSpec((tm * ROW_TILES, LANES), row), pl.BlockSpec((2 * TOP_K, tm), col),
                   pl.BlockSpec((2 * TOP_K, tm), col), pl.BlockSpec((N_EXPERTS, LANES), fix)],
        out_shape=[jax.ShapeDtypeStruct((T * ROW_TILES, LANES), _f32),
                   jax.ShapeDtypeStruct((2 * TOP_K, T), jnp.int32),
                   jax.ShapeDtypeStruct((2 * TOP_K, T), _f32),
                   jax.ShapeDtypeStruct((N_EXPERTS, LANES), _f32)],
        scratch_shapes=[pltpu.VMEM((N_EXPERTS, 1), _f32)],
        compiler_params=pltpu.CompilerParams(dimension_semantics=("arbitrary",),
                                             vmem_limit_bytes=VMEM_LIMIT),
        name="outproj_router",
    )(od, osw, x2, gd, gs, w, b, g2, wr_t, br)


def _dispatch_kernel(d_ref, x1_ref, xs_hbm, sem):
    td = DISPATCH_TILE

    def start_group(g, c):
        for rr in range(DMA_GROUP):
            r = g * DMA_GROUP + rr
            src = _token_row(x1_ref, r)
            for k in range(TOP_K):
                pltpu.make_async_copy(src, _token_row(xs_hbm, d_ref[0, k * td + r]), sem).start(priority=k % 2)
        return c

    lax.fori_loop(0, td // DMA_GROUP, start_group, 0)

    def wait_group(g, c):
        for _ in range(DMA_GROUP * TOP_K):
            pltpu.make_async_copy(_token_row(x1_ref, 0), _token_row(xs_hbm, 0), sem).wait()
        return c

    lax.fori_loop(0, td // DMA_GROUP, wait_group, 0)


def _dispatch_call(dest_tiles, x1r, n_slots):
    nt = dest_tiles.shape[0]
    return pl.pallas_call(
        _dispatch_kernel,
        grid=(nt,),
        in_specs=[pl.BlockSpec((None, 1, TOP_K * DISPATCH_TILE), lambda i: (i, 0, 0), memory_space=pltpu.SMEM),
                  pl.BlockSpec((DISPATCH_TILE * ROW_TILES, LANES), lambda i: (i, 0))],
        out_specs=pl.BlockSpec(memory_space=pl.ANY),
        out_shape=jax.ShapeDtypeStruct((n_slots * ROW_TILES, LANES), _f32),
        scratch_shapes=[pltpu.SemaphoreType.DMA(())],
        compiler_params=pltpu.CompilerParams(dimension_semantics=("arbitrary",)),
        name="dispatch_rows",
    )(dest_tiles, x1r)


def _moe_kernel(be_ref, nv_ref, src_ref, grp_ref, nxt_ref, xs_ref, g2_ref, w1_hbm, b1_ref, w2_hbm, b2_ref,
                ys_ref, st1, st2, w1p, w2b, sem):
    i = pl.program_id(0)
    nv = nv_ref[i]
    tm = EXPERT_TILE

    def weight_copies(e, slot):
        return (pltpu.make_async_copy(w1_hbm.at[e], st1.at[slot], sem.at[0, slot]),
                pltpu.make_async_copy(w2_hbm.at[e], st2.at[slot], sem.at[1, slot]))

    @pl.when(nv > 0)
    def _():
        first_of_expert = jnp.logical_or(i == 0, be_ref[i] != be_ref[jnp.maximum(i - 1, 0)])

        @pl.when(first_of_expert)
        def _():
            slot = grp_ref[i] % 2

            @pl.when(i == 0)
            def _():
                for c in weight_copies(be_ref[0], 0):
                    c.start()

            for c in weight_copies(be_ref[i], slot):
                c.wait()

            @pl.when(nxt_ref[i] >= 0)
            def _():
                for c in weight_copies(nxt_ref[i], 1 - slot):
                    c.start()

            src = lax.broadcasted_iota(jnp.int32, (GLU_GROUP, GLU_GROUP), 0)
            dst = lax.broadcasted_iota(jnp.int32, (GLU_GROUP, GLU_GROUP), 1)
            half = GLU_GROUP // 2
            perm = (src == jnp.where(dst < half, 2 * dst, 2 * (dst - half) + 1)).astype(_bf16)
            for g in range(2 * D_FF // GLU_GROUP):
                cols = slice(g * GLU_GROUP, (g + 1) * GLU_GROUP)
                w1p[:, cols] = jnp.dot(st1[slot, :, cols].astype(_bf16), perm,
                                       preferred_element_type=_f32).astype(_bf16)
            w2b[...] = st2[slot].astype(_bf16)

        hm = tm // MOE_CHAINS
        for s in range(MOE_CHAINS):
            rows = lambda c, s=s: pl.ds(s * hm * ROW_TILES + c, hm, stride=ROW_TILES)
            x = jnp.concatenate([xs_ref[rows(c), :] for c in range(ROW_TILES)], axis=1)
            x = jnp.where(lax.broadcasted_iota(jnp.int32, (hm, 1), 0) < nv - s * hm, x, 0.0)
            h = _rms(x, g2_ref[...]).astype(_bf16)
            u = jnp.dot(h, w1p[...], preferred_element_type=_f32) + b1_ref[0]
            half = GLU_GROUP // 2
            hid = []
            for g in range(2 * D_FF // GLU_GROUP):
                glu = jnp.minimum(u[:, g * GLU_GROUP:g * GLU_GROUP + half], SWIGLU_LIMIT)
                lin = jnp.clip(u[:, g * GLU_GROUP + half:(g + 1) * GLU_GROUP], -SWIGLU_LIMIT, SWIGLU_LIMIT)
                hid.append(glu * (1.0 / (1.0 + jnp.exp(-SWIGLU_ALPHA * glu))) * (lin + 1.0))
            hid = jnp.concatenate(hid, axis=1).astype(_bf16)
            y = jnp.dot(hid, w2b[...], preferred_element_type=_f32) + b2_ref[0]
            for c in range(ROW_TILES):
                ys_ref[rows(c), :] = y[:, c * LANES:(c + 1) * LANES]


def _moe_call(block_e, n_valid, block_src, block_grp, next_e, xs, g2, w1, b1p, w2, b2):
    nblk = block_e.shape[0]
    tm = EXPERT_TILE
    by_e = lambda i, be, nv, src, grp, nxt: (be[i], 0, 0)
    by_src = lambda i, be, nv, src, grp, nxt: (src[i], 0)
    fix = lambda i, be, nv, src, grp, nxt: (0, 0)
    grid_spec = pltpu.PrefetchScalarGridSpec(
        num_scalar_prefetch=5,
        grid=(nblk,),
        in_specs=[pl.BlockSpec((tm * ROW_TILES, LANES), by_src), pl.BlockSpec((1, D_MODEL), fix),
                  pl.BlockSpec(memory_space=pl.ANY), pl.BlockSpec((1, 1, 2 * D_FF), by_e),
                  pl.BlockSpec(memory_space=pl.ANY), pl.BlockSpec((1, 1, D_MODEL), by_e)],
        out_specs=pl.BlockSpec((tm * ROW_TILES, LANES), by_src),
        scratch_shapes=[pltpu.VMEM((2, D_MODEL, 2 * D_FF), _f32), pltpu.VMEM((2, D_FF, D_MODEL), _f32),
                        pltpu.VMEM((D_MODEL, 2 * D_FF), _bf16), pltpu.VMEM((D_FF, D_MODEL), _bf16),
                        pltpu.SemaphoreType.DMA((2, 2))],
    )
    return pl.pallas_call(
        _moe_kernel,
        grid_spec=grid_spec,
        out_shape=jax.ShapeDtypeStruct(xs.shape, _f32),
        compiler_params=pltpu.CompilerParams(dimension_semantics=("arbitrary",),
                                             vmem_limit_bytes=VMEM_LIMIT),
        name="moe_experts",
    )(block_e, n_valid, block_src, block_grp, next_e, xs, g2, w1, b1p, w2, b2)


def _combine_kernel(dcur_ref, dnext_ref, x1_ref, ys_hbm, g_ref, gf_ref, o_ref, gbuf, sem):
    tc = COMBINE_TILE
    i = pl.program_id(0)
    slot = i % 2

    def row_copy(d_ref, s, r, k):
        dst = gbuf.at[s, k, pl.ds(r * ROW_TILES, ROW_TILES)]
        return pltpu.make_async_copy(_token_row(ys_hbm, d_ref[0, k * tc + r]), dst, sem.at[s])

    grp = COMBINE_GROUP

    def issue_group(d_ref, s, g):
        for rr in range(grp):
            for k in range(TOP_K):
                row_copy(d_ref, s, g * grp + rr, k).start(priority=k % 2)

    def compute_group(g):
        def rows(ref):
            base = g * (grp * ROW_TILES)
            return jnp.concatenate([ref[pl.ds(base + c, grp, stride=ROW_TILES), :]
                                    for c in range(ROW_TILES)], axis=1)
        toks = pl.ds(pl.multiple_of(g * grp, grp), grp)
        gate = g_ref[toks, :]
        acc = rows(x1_ref)
        for k in range(TOP_K):
            acc = acc + gate[:, k:k + 1] * rows(gbuf.at[slot, k])
        o_ref[toks, :] = _rms(acc, gf_ref[...])

    @pl.when(i == 0)
    def _():
        def group(g, c):
            issue_group(dcur_ref, 0, g)
            return c
        lax.fori_loop(0, tc // grp, group, 0)

    def wait_group(g, c):
        for _ in range(grp * TOP_K):
            row_copy(dcur_ref, slot, 0, 0).wait()
        return c

    lax.fori_loop(0, tc // grp, wait_group, 0)

    @pl.when(i + 1 < pl.num_programs(0))
    def _():
        def group(g, c):
            compute_group(g)
            issue_group(dnext_ref, 1 - slot, g)
            return c
        lax.fori_loop(0, tc // grp, group, 0)

    @pl.when(i + 1 == pl.num_programs(0))
    def _():
        def group(g, c):
            compute_group(g)
            return c
        lax.fori_loop(0, tc // grp, group, 0)


def _combine_call(dest_tiles, x1r, ys, gates_t, gf):
    nt = dest_tiles.shape[0]
    tc = COMBINE_TILE
    row = lambda i: (i, 0)
    return pl.pallas_call(
        _combine_kernel,
        grid=(nt,),
        in_specs=[pl.BlockSpec((None, 1, TOP_K * tc), lambda i: (i, 0, 0), memory_space=pltpu.SMEM),
                  pl.BlockSpec((None, 1, TOP_K * tc), lambda i: (jnp.minimum(i + 1, nt - 1), 0, 0),
                               memory_space=pltpu.SMEM),
                  pl.BlockSpec((tc * ROW_TILES, LANES), row), pl.BlockSpec(memory_space=pl.ANY),
                  pl.BlockSpec((tc, TOP_K), row), pl.BlockSpec((1, D_MODEL), lambda i: (0, 0))],
        out_specs=pl.BlockSpec((tc, D_MODEL), row),
        out_shape=jax.ShapeDtypeStruct((nt * tc, D_MODEL), _f32),
        scratch_shapes=[pltpu.VMEM((2, TOP_K, tc * ROW_TILES, LANES), _f32), pltpu.SemaphoreType.DMA((2,))],
        compiler_params=pltpu.CompilerParams(dimension_semantics=("arbitrary",),
                                             vmem_limit_bytes=VMEM_LIMIT),
        name="combine_norm",
    )(dest_tiles, dest_tiles, x1r, ys, gates_t, gf)


def _plan_blocks(route_i, counts, n_tokens):
    tm = EXPERT_TILE
    nblk = (n_tokens * TOP_K + N_EXPERTS * tm) // tm
    counts = counts.astype(jnp.int32)
    blocks_e = (counts + tm - 1) // tm
    blk_end = jnp.cumsum(blocks_e)
    pad_start = (blk_end - blocks_e) * tm
    experts = jnp.arange(N_EXPERTS, dtype=jnp.int32)
    hot = route_i[:TOP_K, :, None] == experts
    dest = jnp.sum(jnp.where(hot, pad_start, 0), axis=-1) + route_i[TOP_K:]
    n_used = blk_end[-1]
    blk = jnp.arange(nblk, dtype=jnp.int32)
    src = jnp.minimum(blk, jnp.maximum(n_used - 1, 0))
    be = jnp.minimum(jnp.sum((blk_end[None, :] <= src[:, None]).astype(jnp.int32), axis=1), N_EXPERTS - 1)
    hot_b = be[:, None] == experts
    pick = lambda table: jnp.sum(jnp.where(hot_b, table, 0), axis=1)
    n_valid = jnp.where(blk < n_used, jnp.clip(pick(counts) - (src * tm - pick(pad_start)), 0, tm), 0)
    used = blocks_e > 0
    grp_e = jnp.cumsum(used.astype(jnp.int32)) - 1
    later = jnp.where((experts[None, :] > experts[:, None]) & used[None, :], experts[None, :], N_EXPERTS)
    nxt_e = jnp.min(later, axis=1)
    nxt_e = jnp.where(nxt_e < N_EXPERTS, nxt_e, -1)
    return be, n_valid.astype(jnp.int32), src, pick(grp_e), pick(nxt_e), dest


def _tile_dest(dest, tile):
    k, t = dest.shape
    return dest.reshape(k, t // tile, tile).transpose(1, 0, 2).reshape(t // tile, 1, k * tile)


def kernel(x, positions, norm1_g, w_in, b_in, sinks, gn_dil, gn_swa, w_out, b_out,
           norm2_g, w_router, b_router, w_mlp1, b_mlp1, w_mlp2, b_mlp2, norm_f_g):
    batch, seq, d = x.shape
    T = batch * seq
    x2 = x.reshape(T, d)
    depth = norm1_g.shape[0]
    assert depth == 1, "the final norm is fused into the last layer's combine kernel"
    inv = ROPE_THETA ** (-jnp.arange(0, HEAD_DIM, 2, dtype=_f32) / HEAD_DIM)
    inv = jnp.tile(inv, LANES // (HEAD_DIM // 2)).reshape(1, LANES)
    pos = positions.reshape(T, 1)
    for layer in range(depth):
        qa, ka, va, qb, kb, vb = _inproj_call(
            x2, pos, inv, norm1_g[layer].reshape(1, d), w_in[layer].astype(_bf16), b_in[layer].reshape(1, D_IN))
        o_dil = _attn_call(qa, ka, va, None, batch=batch, seq=seq,
                           configs=tuple((w // r, r) for w, r in DILATED_CONFIGS), kv_group=1)
        sink_lanes = jnp.repeat(sinks[layer].astype(_f32), HEAD_DIM).reshape(1, W_SWA)
        o_swa = _attn_call(qb, kb, vb, sink_lanes, batch=batch, seq=seq,
                           configs=((SWA_WINDOW - 1, 1),), kv_group=N_HEADS_SWA // N_KV_SWA // PAIR)
        x1r, route_i, route_g, counts = _outproj_call(
            o_dil, o_swa, x2, gn_dil[layer].reshape(1, W_DIL), gn_swa[layer].reshape(1, W_SWA),
            w_out[layer].astype(_bf16), b_out[layer].reshape(1, d), norm2_g[layer].reshape(1, d),
            w_router[layer].T, b_router[layer].reshape(N_EXPERTS, 1))
        block_e, n_valid, block_src, block_grp, next_e, dest = _plan_blocks(route_i, counts[:, 0], T)
        xs = _dispatch_call(_tile_dest(dest, DISPATCH_TILE), x1r, block_e.shape[0] * EXPERT_TILE)
        half = GLU_GROUP // 2
        b1p = b_mlp1[layer].reshape(N_EXPERTS, 2 * D_FF // GLU_GROUP, half, 2)
        b1p = jnp.swapaxes(b1p, 2, 3).reshape(N_EXPERTS, 1, 2 * D_FF)
        ys = _moe_call(block_e, n_valid, block_src, block_grp, next_e, xs, norm2_g[layer].reshape(1, d),
                       w_mlp1[layer], b1p, w_mlp2[layer], b_mlp2[layer].reshape(N_EXPERTS, 1, d))
        x2 = _combine_call(_tile_dest(dest, COMBINE_TILE), x1r, ys, route_g[:TOP_K].T, norm_f_g.reshape(1, d))
    return x2.reshape(batch, seq, d)
```

```python
import functools

import jax
import jax.numpy as jnp
from jax import lax
from jax.experimental import pallas as pl
from jax.experimental.pallas import tpu as pltpu

D_MODEL = 1024
HEAD_DIM = 64
N_HEADS_DIL = 8
DILATED_CONFIGS = ((128, 1), (512, 4), (2048, 16))
N_HEADS_SWA = 8
N_KV_SWA = 2
SWA_WINDOW = 128
ATTN_BLOCK = 128
ROPE_THETA = 10000.0
N_EXPERTS = 32
TOP_K = 4
D_FF = D_MODEL
SWIGLU_ALPHA = 1.702
SWIGLU_LIMIT = 7.0
NORM_EPS = 1e-5
LOG2E = 1.4426950408889634

W_DIL = N_HEADS_DIL * HEAD_DIM
W_SWA = N_HEADS_SWA * HEAD_DIM
W_KV_SWA = N_KV_SWA * HEAD_DIM
D_IN = 3 * W_DIL + W_SWA + 2 * W_KV_SWA

LANES = 128
PAIR = LANES // HEAD_DIM
SUBLANES = 8
ROW_TILES = D_MODEL // LANES
TOKEN_TILE = 512
EXPERT_TILE = 512
DISPATCH_TILE = 1024
COMBINE_TILE = 256
DMA_GROUP = 8
ATTN_UNROLL = 8
MOE_CHAINS = 2
INPROJ_CHAINS = 2
OUTPROJ_CHAINS = 2
GLU_GROUP = 256
VMEM_LIMIT = 56 * 1024 * 1024

_f32 = jnp.float32
_bf16 = jnp.bfloat16


def _rms(x, g):
    return x * lax.rsqrt(jnp.mean(x * x, axis=-1, keepdims=True) + NORM_EPS) * g


def _store_token_rows(ref, x):
    tm = x.shape[0]
    for c in range(ROW_TILES):
        ref[pl.ds(c, tm, stride=ROW_TILES), :] = x[:, c * LANES:(c + 1) * LANES]


def _load_token_rows(ref, tm):
    return jnp.concatenate([ref[pl.ds(c, tm, stride=ROW_TILES), :] for c in range(ROW_TILES)], axis=1)


def _token_row(ref, tok):
    return ref.at[pl.ds(pl.multiple_of(tok * ROW_TILES, ROW_TILES), ROW_TILES)]


def _inproj_kernel(x_ref, pos_ref, inv_ref, g_ref, w_ref, b_ref, *out_refs):
    sub = x_ref.shape[0] // INPROJ_CHAINS
    for s in range(INPROJ_CHAINS):
        rows = slice(s * sub, (s + 1) * sub)
        _inproj_rows(x_ref[rows, :], pos_ref[rows, :], inv_ref, g_ref, w_ref, b_ref,
                     [r.at[rows, :] for r in out_refs])


def _inproj_rows(x, pos, inv_ref, g_ref, w_ref, b_ref, out_refs):
    qa_ref, ka_ref, va_ref, qb_ref, kb_ref, vb_ref = out_refs
    h = _rms(x, g_ref[...]).astype(_bf16)
    proj = jnp.dot(h, w_ref[...], preferred_element_type=_f32) + b_ref[...]

    ang = pos.astype(_f32) * inv_ref[...]
    lane = lax.broadcasted_iota(jnp.int32, ang.shape, 1)
    first_half = (lane % HEAD_DIM) < (HEAD_DIM // 2)
    cos = jnp.cos(ang)
    sin = jnp.where(first_half, -jnp.sin(ang), jnp.sin(ang))

    def rope(t, scale):
        outs = []
        for c in range(t.shape[1] // LANES):
            tc = t[:, c * LANES:(c + 1) * LANES]
            partner = jnp.where(first_half, pltpu.roll(tc, LANES - HEAD_DIM // 2, 1),
                                pltpu.roll(tc, HEAD_DIM // 2, 1))
            outs.append((tc * cos + partner * sin) * scale)
        return jnp.concatenate(outs, axis=1) if len(outs) > 1 else outs[0]

    scale = HEAD_DIM ** -0.5 * LOG2E
    o = 0
    qa_ref[...] = rope(proj[:, o:o + W_DIL], scale).astype(_bf16); o += W_DIL
    ka_ref[...] = rope(proj[:, o:o + W_DIL], 1.0).astype(_bf16); o += W_DIL
    va_ref[...] = proj[:, o:o + W_DIL].astype(_bf16); o += W_DIL
    qb_ref[...] = rope(proj[:, o:o + W_SWA], scale).astype(_bf16); o += W_SWA
    kb = rope(proj[:, o:o + W_KV_SWA], 1.0); o += W_KV_SWA
    vb = proj[:, o:o + W_KV_SWA]

    def dup_heads(t):
        sw = pltpu.roll(t, HEAD_DIM, 1)
        lo = lane < HEAD_DIM
        return jnp.concatenate([jnp.where(lo, t, sw), jnp.where(lo, sw, t)], axis=1)

    kb_ref[...] = dup_heads(kb).astype(_bf16)
    vb_ref[...] = dup_heads(vb).astype(_bf16)


def _inproj_call(x2, pos, inv, g, w, b):
    T = x2.shape[0]
    tm = TOKEN_TILE
    row = lambda i: (i, 0)
    fix = lambda i: (0, 0)
    widths = (W_DIL, W_DIL, W_DIL, W_SWA, PAIR * W_KV_SWA, PAIR * W_KV_SWA)
    return pl.pallas_call(
        _inproj_kernel,
        grid=(T // tm,),
        in_specs=[pl.BlockSpec((tm, D_MODEL), row), pl.BlockSpec((tm, 1), row),
                  pl.BlockSpec((1, LANES), fix), pl.BlockSpec((1, D_MODEL), fix),
                  pl.BlockSpec((D_MODEL, D_IN), fix), pl.BlockSpec((1, D_IN), fix)],
        out_specs=[pl.BlockSpec((tm, wd), row) for wd in widths],
        out_shape=[jax.ShapeDtypeStruct((T, wd), _bf16) for wd in widths],
        compiler_params=pltpu.CompilerParams(dimension_semantics=("parallel",),
                                             vmem_limit_bytes=VMEM_LIMIT),
        name="inproj",
    )(x2, pos, inv, g, w, b)


def _attn_kernel(*refs, configs, seq, with_sink):
    if with_sink:
        q_ref, k_ref, v_ref, sink_ref, o_ref = refs[:5]
        scratch = refs[5:]
    else:
        q_ref, k_ref, v_ref, o_ref = refs[:4]
        scratch = refs[4:]
    n_cfg = len(configs)
    obufs = scratch[:n_cfg]
    lbufs = scratch[n_cfg:2 * n_cfg]
    extra = list(scratch[2 * n_cfg:])
    blk = ATTN_BLOCK
    dilated = [r for _, r in configs if r > 1]
    if dilated:
        qf, kf, vf = extra[:3]
        qf[...] = q_ref[...].astype(_f32)
        kf[...] = k_ref[...].astype(_f32)
        vf[...] = v_ref[...].astype(_f32)
        extra = extra[3:]

    lane = lax.broadcasted_iota(jnp.int32, (blk, LANES), 1)
    head0 = lane < HEAD_DIM
    lane_row = lax.broadcasted_iota(jnp.int32, (1, LANES), 1)
    head_rows = [(lane_row < HEAD_DIM).astype(_bf16), (lane_row >= HEAD_DIM).astype(_bf16)]
    qi = lax.broadcasted_iota(jnp.int32, (blk, 2 * blk), 0)
    kj = lax.broadcasted_iota(jnp.int32, (blk, 2 * blk), 1)
    dist = qi - kj + blk

    base_perm = None
    for ci, (max_dist, r) in enumerate(configs):
        nb = seq // r // blk
        band = (dist >= 0) & (dist <= max_dist)
        obuf, lbuf = obufs[ci], lbufs[ci]
        if r == 1:
            qp, kp, vp = q_ref, k_ref, v_ref
        else:
            perm_bf16 = extra[:3]
            extra = extra[3:]
            later = [r2 for _, r2 in configs[ci + 1:] if r2 % r == 0]
            perm_f32 = []
            if later:
                perm_f32 = extra[:3]
                extra = extra[3:]
            r_src, srcs = 1, (qf, kf, vf)
            if base_perm is not None and r % base_perm[0] == 0:
                r_src, srcs = base_perm
            if perm_f32:
                base_perm = (r, perm_f32)
            qp, kp, vp = perm_bf16
            piece = 2 * blk
            step = r // r_src
            per_res = seq // r // piece
            len_src = seq // r_src

            def permute(idx, carry, r_src=r_src, srcs=srcs, step=step, per_res=per_res, len_src=len_src,
                        perm_bf16=perm_bf16, perm_f32=perm_f32):
                c = idx // per_res
                start = (c % r_src) * len_src + c // r_src + (step * piece) * (idx % per_res)
                src = pl.ds(start, piece, stride=step)
                dst = pl.ds(pl.multiple_of(idx * piece, piece), piece)
                for t in range(3):
                    val = srcs[t][src, :]
                    perm_bf16[t][dst, :] = val.astype(_bf16)
                    if perm_f32:
                        perm_f32[t][dst, :] = val
                return carry

            lax.fori_loop(0, seq // piece, permute, 0, unroll=4)

        def body(n, carry, r=r, nb=nb, band=band, obuf=obuf, lbuf=lbuf, qp=qp, kp=kp, vp=vp):
            j = n % nb
            cur = pl.ds(pl.multiple_of(n * blk, blk), blk)
            prev = pl.ds(pl.multiple_of(jnp.maximum(n - 1, 0) * blk, blk), blk)
            q = qp[cur, :]
            k2 = jnp.concatenate([kp[prev, :], kp[cur, :]], axis=0)
            v2 = jnp.concatenate([vp[prev, :], vp[cur, :]], axis=0)
            mask = band & (kj >= jnp.where(j > 0, 0, blk))
            outs, lses = [], []
            for hm in head_rows:
                qh = q * hm
                s = lax.dot_general(qh, k2, (((1,), (1,)), ((), ())), preferred_element_type=_f32)
                s = jnp.where(mask, s, -jnp.inf)
                m = jnp.max(s, axis=-1, keepdims=True)
                p = jnp.exp2(s - m)
                l = jnp.sum(p, axis=-1, keepdims=True)
                o = jnp.dot(p.astype(_bf16), v2, preferred_element_type=_f32) / l
                outs.append(o)
                lses.append(jnp.broadcast_to(m + jnp.log2(l), (blk, LANES)))
            start = n // nb + (r * blk) * j
            dst = pl.ds(pl.multiple_of(start, blk), blk) if r == 1 else pl.ds(start, blk, stride=r)
            obuf[dst, :] = jnp.where(head0, outs[0], outs[1])
            lbuf[dst, :] = jnp.where(head0, lses[0], lses[1])
            return carry

        lax.fori_loop(0, seq // blk, body, 0, unroll=ATTN_UNROLL)

    chunk = 4 * blk

    def finish(i, carry):
        sl = pl.ds(pl.multiple_of(i * chunk, chunk), chunk)
        if with_sink:
            o = obufs[0][sl, :] * (1.0 / (1.0 + jnp.exp2(sink_ref[...] * LOG2E - lbufs[0][sl, :])))
        else:
            ls = [lb[sl, :] for lb in lbufs]
            mx = functools.reduce(jnp.maximum, ls)
            es = [jnp.exp2(x - mx) for x in ls]
            den = functools.reduce(lambda a, b: a + b, es)
            o = functools.reduce(lambda a, b: a + b,
                                 [e / den * ob[sl, :] for e, ob in zip(es, obufs)])
        o_ref[sl, :] = o
        return carry

    lax.fori_loop(0, seq // chunk, finish, 0)


def _attn_call(q, k, v, sinks, *, batch, seq, configs, kv_group):
    n_pairs = q.shape[1] // LANES
    with_sink = sinks is not None
    qmap = lambda b, p: (b, p)
    kvmap = lambda b, p: (b, p // kv_group)
    in_specs = [pl.BlockSpec((seq, LANES), qmap), pl.BlockSpec((seq, LANES), kvmap),
                pl.BlockSpec((seq, LANES), kvmap)]
    args = [q, k, v]
    if with_sink:
        in_specs.append(pl.BlockSpec((1, LANES), lambda b, p: (0, p)))
        args.append(sinks)
    buf = lambda dt: pltpu.VMEM((seq, LANES), dt)
    scratch = [buf(_f32) for _ in range(2 * len(configs))]
    if any(r > 1 for _, r in configs):
        scratch += [buf(_f32) for _ in range(3)]
    for ci, (_, r) in enumerate(configs):
        if r > 1:
            scratch += [buf(_bf16) for _ in range(3)]
            if any(r2 % r == 0 for _, r2 in configs[ci + 1:]):
                scratch += [buf(_f32) for _ in range(3)]
    return pl.pallas_call(
        functools.partial(_attn_kernel, configs=configs, seq=seq, with_sink=with_sink),
        grid=(batch, n_pairs),
        in_specs=in_specs,
        out_specs=pl.BlockSpec((seq, LANES), qmap),
        out_shape=jax.ShapeDtypeStruct(q.shape, _f32),
        scratch_shapes=scratch,
        compiler_params=pltpu.CompilerParams(dimension_semantics=("parallel", "parallel"),
                                             vmem_limit_bytes=VMEM_LIMIT),
        name="attn_sink" if with_sink else "attn_dilated",
    )(*args)


def _outproj_kernel(od_ref, os_ref, x_ref, gd_ref, gs_ref, w_ref, b_ref, g2_ref, wr_ref, br_ref,
                    x1_ref, ri_ref, rg_ref, cnt_ref, carry_ref):
    tm = x_ref.shape[0]

    @pl.when(pl.program_id(0) == 0)
    def _():
        carry_ref[...] = jnp.zeros_like(carry_ref)

    sub = tm // OUTPROJ_CHAINS
    h2 = []
    for s in range(OUTPROJ_CHAINS):
        rows = slice(s * sub, (s + 1) * sub)
        mixed = jnp.concatenate([_rms(od_ref[rows, :], gd_ref[...]), _rms(os_ref[rows, :], gs_ref[...])], axis=1)
        x1 = x_ref[rows, :] + jnp.dot(mixed.astype(_bf16), w_ref[...], preferred_element_type=_f32) + b_ref[...]
        _store_token_rows(x1_ref.at[pl.ds(s * sub * ROW_TILES, sub * ROW_TILES)], x1)
        h2.append(_rms(x1, g2_ref[...]))
    h2 = jnp.concatenate(h2, axis=0)
    logits = lax.dot_general(wr_ref[...], h2, (((1,), (1,)), ((), ())),
                             precision=lax.Precision.HIGHEST, preferred_element_type=_f32) + br_ref[...]
    eiota = lax.broadcasted_iota(jnp.int32, logits.shape, 0).astype(_f32)
    vals, idxs, sels = [], [], []
    l = logits
    for _ in range(TOP_K):
        m = jnp.max(l, axis=0, keepdims=True)
        idx = jnp.min(jnp.where(l == m, eiota, float(N_EXPERTS)), axis=0, keepdims=True)
        sel = eiota == idx
        vals.append(m); idxs.append(idx); sels.append(sel)
        l = jnp.where(sel, -jnp.inf, l)
    exps = [jnp.exp(v - vals[0]) for v in vals]
    den = functools.reduce(lambda a, b: a + b, exps)
    gates = [e / den for e in exps]

    chosen = functools.reduce(jnp.logical_or, sels).astype(_f32)
    ti = lax.broadcasted_iota(jnp.int32, (tm, tm), 0)
    tj = lax.broadcasted_iota(jnp.int32, (tm, tm), 1)
    before = (ti < tj).astype(_bf16)
    prefix = jnp.dot(chosen.astype(_bf16), before, preferred_element_type=_f32) + carry_ref[...]
    ranks = [jnp.sum(jnp.where(s, prefix, 0.0), axis=0, keepdims=True).astype(jnp.int32) for s in sels]
    carry_ref[...] = carry_ref[...] + jnp.sum(chosen, axis=1, keepdims=True)

    ri_ref[...] = jnp.concatenate([ix.astype(jnp.int32) for ix in idxs] + ranks, axis=0)
    rg_ref[...] = jnp.concatenate(gates + gates, axis=0)
    cnt_ref[...] = jnp.broadcast_to(carry_ref[...], cnt_ref.shape)


def _outproj_call(od, osw, x2, gd, gs, w, b, g2, wr_t, br):
    T = x2.shape[0]
    tm = TOKEN_TILE
    row = lambda i: (i, 0)
    col = lambda i: (0, i)
    fix = lambda i: (0, 0)
    return pl.pallas_call(
        _outproj_kernel,
        grid=(T // tm,),
        in_specs=[pl.BlockSpec((tm, W_DIL), row), pl.BlockSpec((tm, W_SWA), row),
                  pl.BlockSpec((tm, D_MODEL), row),
                  pl.BlockSpec((1, W_DIL), fix), pl.BlockSpec((1, W_SWA), fix),
                  pl.BlockSpec((D_MODEL, D_MODEL), fix), pl.BlockSpec((1, D_MODEL), fix),
                  pl.BlockSpec((1, D_MODEL), fix),
                  pl.BlockSpec((N_EXPERTS, D_MODEL), fix), pl.BlockSpec((N_EXPERTS, 1), fix)],
        out_specs=[pl.BlockSpec((tm * ROW_TILES, LANES), row), pl.BlockSpec((2 * TOP_K, tm), col),
                   pl.BlockSpec((2 * TOP_K, tm), col), pl.BlockSpec((N_EXPERTS, LANES), fix)],
        out_shape=[jax.ShapeDtypeStruct((T * ROW_TILES, LANES), _f32),
                   jax.ShapeDtypeStruct((2 * TOP_K, T), jnp.int32),
                   jax.ShapeDtypeStruct((2 * TOP_K, T), _f32),
                   jax.ShapeDtypeStruct((N_EXPERTS, LANES), _f32)],
        scratch_shapes=[pltpu.VMEM((N_EXPERTS, 1), _f32)],
        compiler_params=pltpu.CompilerParams(dimension_semantics=("arbitrary",),
                                             vmem_limit_bytes=VMEM_LIMIT),
        name="outproj_router",
    )(od, osw, x2, gd, gs, w, b, g2, wr_t, br)


def _dispatch_kernel(d_ref, x1_ref, xs_hbm, sem):
    td = DISPATCH_TILE

    def start_group(g, c):
        for rr in range(DMA_GROUP):
            r = g * DMA_GROUP + rr
            src = _token_row(x1_ref, r)
            for k in range(TOP_K):
                pltpu.make_async_copy(src, _token_row(xs_hbm, d_ref[0, k * td + r]), sem).start(priority=k % 2)
        return c

    lax.fori_loop(0, td // DMA_GROUP, start_group, 0)

    def wait_group(g, c):
        for _ in range(DMA_GROUP * TOP_K):
            pltpu.make_async_copy(_token_row(x1_ref, 0), _token_row(xs_hbm, 0), sem).wait()
        return c

    lax.fori_loop(0, td // DMA_GROUP, wait_group, 0)


def _dispatch_call(dest_tiles, x1r, n_slots):
    nt = dest_tiles.shape[0]
    return pl.pallas_call(
        _dispatch_kernel,
        grid=(nt,),
        in_specs=[pl.BlockSpec((None, 1, TOP_K * DISPATCH_TILE), lambda i: (i, 0, 0), memory_space=pltpu.SMEM),
                  pl.BlockSpec((DISPATCH_TILE * ROW_TILES, LANES), lambda i: (i, 0))],
        out_specs=pl.BlockSpec(memory_space=pl.ANY),
        out_shape=jax.ShapeDtypeStruct((n_slots * ROW_TILES, LANES), _f32),
        scratch_shapes=[pltpu.SemaphoreType.DMA(())],
        compiler_params=pltpu.CompilerParams(dimension_semantics=("arbitrary",)),
        name="dispatch_rows",
    )(dest_tiles, x1r)


def _moe_kernel(be_ref, nv_ref, src_ref, grp_ref, nxt_ref, xs_ref, g2_ref, w1_hbm, b1_ref, w2_hbm, b2_ref,
                ys_ref, st1, st2, w1p, w2b, sem):
    i = pl.program_id(0)
    nv = nv_ref[i]
    tm = EXPERT_TILE

    def weight_copies(e, slot):
        return (pltpu.make_async_copy(w1_hbm.at[e], st1.at[slot], sem.at[0, slot]),
                pltpu.make_async_copy(w2_hbm.at[e], st2.at[slot], sem.at[1, slot]))

    @pl.when(nv > 0)
    def _():
        first_of_expert = jnp.logical_or(i == 0, be_ref[i] != be_ref[jnp.maximum(i - 1, 0)])

        @pl.when(first_of_expert)
        def _():
            slot = grp_ref[i] % 2

            @pl.when(i == 0)
            def _():
                for c in weight_copies(be_ref[0], 0):
                    c.start()

            for c in weight_copies(be_ref[i], slot):
                c.wait()

            @pl.when(nxt_ref[i] >= 0)
            def _():
                for c in weight_copies(nxt_ref[i], 1 - slot):
                    c.start()

            src = lax.broadcasted_iota(jnp.int32, (GLU_GROUP, GLU_GROUP), 0)
            dst = lax.broadcasted_iota(jnp.int32, (GLU_GROUP, GLU_GROUP), 1)
            half = GLU_GROUP // 2
            perm = (src == jnp.where(dst < half, 2 * dst, 2 * (dst - half) + 1)).astype(_bf16)
            for g in range(2 * D_FF // GLU_GROUP):
                cols = slice(g * GLU_GROUP, (g + 1) * GLU_GROUP)
                w1p[:, cols] = jnp.dot(st1[slot, :, cols].astype(_bf16), perm,
                                       preferred_element_type=_f32).astype(_bf16)
            w2b[...] = st2[slot].astype(_bf16)

        hm = tm // MOE_CHAINS
        for s in range(MOE_CHAINS):
            rows = lambda c, s=s: pl.ds(s * hm * ROW_TILES + c, hm, stride=ROW_TILES)
            x = jnp.concatenate([xs_ref[rows(c), :] for c in range(ROW_TILES)], axis=1)
            x = jnp.where(lax.broadcasted_iota(jnp.int32, (hm, 1), 0) < nv - s * hm, x, 0.0)
            h = _rms(x, g2_ref[...]).astype(_bf16)
            u = jnp.dot(h, w1p[...], preferred_element_type=_f32) + b1_ref[0]
            half = GLU_GROUP // 2
            hid = []
            for g in range(2 * D_FF // GLU_GROUP):
                glu = jnp.minimum(u[:, g * GLU_GROUP:g * GLU_GROUP + half], SWIGLU_LIMIT)
                lin = jnp.clip(u[:, g * GLU_GROUP + half:(g + 1) * GLU_GROUP], -SWIGLU_LIMIT, SWIGLU_LIMIT)
                hid.append(glu * (1.0 / (1.0 + jnp.exp(-SWIGLU_ALPHA * glu))) * (lin + 1.0))
            hid = jnp.concatenate(hid, axis=1).astype(_bf16)
            y = jnp.dot(hid, w2b[...], preferred_element_type=_f32) + b2_ref[0]
            for c in range(ROW_TILES):
                ys_ref[rows(c), :] = y[:, c * LANES:(c + 1) * LANES]


def _moe_call(block_e, n_valid, block_src, block_grp, next_e, xs, g2, w1, b1p, w2, b2):
    nblk = block_e.shape[0]
    tm = EXPERT_TILE
    by_e = lambda i, be, nv, src, grp, nxt: (be[i], 0, 0)
    by_src = lambda i, be, nv, src, grp, nxt: (src[i], 0)
    fix = lambda i, be, nv, src, grp, nxt: (0, 0)
    grid_spec = pltpu.PrefetchScalarGridSpec(
        num_scalar_prefetch=5,
        grid=(nblk,),
        in_specs=[pl.BlockSpec((tm * ROW_TILES, LANES), by_src), pl.BlockSpec((1, D_MODEL), fix),
                  pl.BlockSpec(memory_space=pl.ANY), pl.BlockSpec((1, 1, 2 * D_FF), by_e),
                  pl.BlockSpec(memory_space=pl.ANY), pl.BlockSpec((1, 1, D_MODEL), by_e)],
        out_specs=pl.BlockSpec((tm * ROW_TILES, LANES), by_src),
        scratch_shapes=[pltpu.VMEM((2, D_MODEL, 2 * D_FF), _f32), pltpu.VMEM((2, D_FF, D_MODEL), _f32),
                        pltpu.VMEM((D_MODEL, 2 * D_FF), _bf16), pltpu.VMEM((D_FF, D_MODEL), _bf16),
                        pltpu.SemaphoreType.DMA((2, 2))],
    )
    return pl.pallas_call(
        _moe_kernel,
        grid_spec=grid_spec,
        out_shape=jax.ShapeDtypeStruct(xs.shape, _f32),
        compiler_params=pltpu.CompilerParams(dimension_semantics=("arbitrary",),
                                             vmem_limit_bytes=VMEM_LIMIT),
        name="moe_experts",
    )(block_e, n_valid, block_src, block_grp, next_e, xs, g2, w1, b1p, w2, b2)


def _combine_kernel(dcur_ref, dnext_ref, x1_ref, ys_hbm, g_ref, gf_ref, o_ref, gbuf, sem):
    tc = COMBINE_TILE
    i = pl.program_id(0)
    slot = i % 2

    def row_copy(d_ref, s, r, k):
        dst = gbuf.at[s, k, pl.ds(r * ROW_TILES, ROW_TILES)]
        return pltpu.make_async_copy(_token_row(ys_hbm, d_ref[0, k * tc + r]), dst, sem.at[s])

    def issue(d_ref, s):
        def group(g, c):
            for rr in range(DMA_GROUP):
                for k in range(TOP_K):
                    row_copy(d_ref, s, g * DMA_GROUP + rr, k).start(priority=k % 2)
            return c
        lax.fori_loop(0, tc // DMA_GROUP, group, 0)

    @pl.when(i == 0)
    def _():
        issue(dcur_ref, 0)

    @pl.when(i + 1 < pl.num_programs(0))
    def _():
        issue(dnext_ref, 1 - slot)

    def wait_group(g, c):
        for _ in range(DMA_GROUP * TOP_K):
            row_copy(dcur_ref, slot, 0, 0).wait()
        return c

    lax.fori_loop(0, tc // DMA_GROUP, wait_group, 0)

    g = g_ref[...]
    acc = _load_token_rows(x1_ref, tc)
    for k in range(TOP_K):
        acc = acc + g[:, k:k + 1] * _load_token_rows(gbuf.at[slot, k], tc)
    o_ref[...] = _rms(acc, gf_ref[...])


def _combine_call(dest_tiles, x1r, ys, gates_t, gf):
    nt = dest_tiles.shape[0]
    tc = COMBINE_TILE
    row = lambda i: (i, 0)
    return pl.pallas_call(
        _combine_kernel,
        grid=(nt,),
        in_specs=[pl.BlockSpec((None, 1, TOP_K * tc), lambda i: (i, 0, 0), memory_space=pltpu.SMEM),
                  pl.BlockSpec((None, 1, TOP_K * tc), lambda i: (jnp.minimum(i + 1, nt - 1), 0, 0),
                               memory_space=pltpu.SMEM),
                  pl.BlockSpec((tc * ROW_TILES, LANES), row), pl.BlockSpec(memory_space=pl.ANY),
                  pl.BlockSpec((tc, TOP_K), row), pl.BlockSpec((1, D_MODEL), lambda i: (0, 0))],
        out_specs=pl.BlockSpec((tc, D_MODEL), row),
        out_shape=jax.ShapeDtypeStruct((nt * tc, D_MODEL), _f32),
        scratch_shapes=[pltpu.VMEM((2, TOP_K, tc * ROW_TILES, LANES), _f32), pltpu.SemaphoreType.DMA((2,))],
        compiler_params=pltpu.CompilerParams(dimension_semantics=("arbitrary",),
                                             vmem_limit_bytes=VMEM_LIMIT),
        name="combine_norm",
    )(dest_tiles, dest_tiles, x1r, ys, gates_t, gf)


def _plan_blocks(route_i, counts, n_tokens):
    tm = EXPERT_TILE
    nblk = (n_tokens * TOP_K + N_EXPERTS * tm) // tm
    counts = counts.astype(jnp.int32)
    blocks_e = (counts + tm - 1) // tm
    blk_end = jnp.cumsum(blocks_e)
    pad_start = (blk_end - blocks_e) * tm
    experts = jnp.arange(N_EXPERTS, dtype=jnp.int32)
    hot = route_i[:TOP_K, :, None] == experts
    dest = jnp.sum(jnp.where(hot, pad_start, 0), axis=-1) + route_i[TOP_K:]
    n_used = blk_end[-1]
    blk = jnp.arange(nblk, dtype=jnp.int32)
    src = jnp.minimum(blk, jnp.maximum(n_used - 1, 0))
    be = jnp.minimum(jnp.sum((blk_end[None, :] <= src[:, None]).astype(jnp.int32), axis=1), N_EXPERTS - 1)
    hot_b = be[:, None] == experts
    pick = lambda table: jnp.sum(jnp.where(hot_b, table, 0), axis=1)
    n_valid = jnp.where(blk < n_used, jnp.clip(pick(counts) - (src * tm - pick(pad_start)), 0, tm), 0)
    used = blocks_e > 0
    grp_e = jnp.cumsum(used.astype(jnp.int32)) - 1
    later = jnp.where((experts[None, :] > experts[:, None]) & used[None, :], experts[None, :], N_EXPERTS)
    nxt_e = jnp.min(later, axis=1)
    nxt_e = jnp.where(nxt_e < N_EXPERTS, nxt_e, -1)
    return be, n_valid.astype(jnp.int32), src, pick(grp_e), pick(nxt_e), dest


def _tile_dest(dest, tile):
    k, t = dest.shape
    return dest.reshape(k, t // tile, tile).transpose(1, 0, 2).reshape(t // tile, 1, k * tile)


def kernel(x, positions, norm1_g, w_in, b_in, sinks, gn_dil, gn_swa, w_out, b_out,
           norm2_g, w_router, b_router, w_mlp1, b_mlp1, w_mlp2, b_mlp2, norm_f_g):
    batch, seq, d = x.shape
    T = batch * seq
    x2 = x.reshape(T, d)
    depth = norm1_g.shape[0]
    assert depth == 1, "the final norm is fused into the last layer's combine kernel"
    inv = ROPE_THETA ** (-jnp.arange(0, HEAD_DIM, 2, dtype=_f32) / HEAD_DIM)
    inv = jnp.tile(inv, LANES // (HEAD_DIM // 2)).reshape(1, LANES)
    pos = positions.reshape(T, 1)
    for layer in range(depth):
        qa, ka, va, qb, kb, vb = _inproj_call(
            x2, pos, inv, norm1_g[layer].reshape(1, d), w_in[layer].astype(_bf16), b_in[layer].reshape(1, D_IN))
        o_dil = _attn_call(qa, ka, va, None, batch=batch, seq=seq,
                           configs=tuple((w // r, r) for w, r in DILATED_CONFIGS), kv_group=1)
        sink_lanes = jnp.repeat(sinks[layer].astype(_f32), HEAD_DIM).reshape(1, W_SWA)
        o_swa = _attn_call(qb, kb, vb, sink_lanes, batch=batch, seq=seq,
                           configs=((SWA_WINDOW - 1, 1),), kv_group=N_HEADS_SWA // N_KV_SWA // PAIR)
        x1r, route_i, route_g, counts = _outproj_call(
            o_dil, o_swa, x2, gn_dil[layer].reshape(1, W_DIL), gn_swa[layer].reshape(1, W_SWA),
            w_out[layer].astype(_bf16), b_out[layer].reshape(1, d), norm2_g[layer].reshape(1, d),
            w_router[layer].T, b_router[layer].reshape(N_EXPERTS, 1))
        block_e, n_valid, block_src, block_grp, next_e, dest = _plan_blocks(route_i, counts[:, 0], T)
        xs = _dispatch_call(_tile_dest(dest, DISPATCH_TILE), x1r, block_e.shape[0] * EXPERT_TILE)
        half = GLU_GROUP // 2
        b1p = b_mlp1[layer].reshape(N_EXPERTS, 2 * D_FF // GLU_GROUP, half, 2)
        b1p = jnp.swapaxes(b1p, 2, 3).reshape(N_EXPERTS, 1, 2 * D_FF)
        ys = _moe_call(block_e, n_valid, block_src, block_grp, next_e, xs, norm2_g[layer].reshape(1, d),
                       w_mlp1[layer], b1p, w_mlp2[layer], b_mlp2[layer].reshape(N_EXPERTS, 1, d))
        x2 = _combine_call(_tile_dest(dest, COMBINE_TILE), x1r, ys, route_g[:TOP_K].T, norm_f_g.reshape(1, d))
    return x2.reshape(batch, seq, d)
```

```python
import functools

import jax
import jax.numpy as jnp
from jax import lax
from jax.experimental import pallas as pl
from jax.experimental.pallas import tpu as pltpu

D_MODEL = 1024
HEAD_DIM = 64
N_HEADS_DIL = 8
DILATED_CONFIGS = ((128, 1), (512, 4), (2048, 16))
N_HEADS_SWA = 8
N_KV_SWA = 2
SWA_WINDOW = 128
ATTN_BLOCK = 128
ROPE_THETA = 10000.0
N_EXPERTS = 32
TOP_K = 4
D_FF = D_MODEL
SWIGLU_ALPHA = 1.702
SWIGLU_LIMIT = 7.0
NORM_EPS = 1e-5
LOG2E = 1.4426950408889634

W_DIL = N_HEADS_DIL * HEAD_DIM
W_SWA = N_HEADS_SWA * HEAD_DIM
W_KV_SWA = N_KV_SWA * HEAD_DIM
D_IN = 3 * W_DIL + W_SWA + 2 * W_KV_SWA

LANES = 128
PAIR = LANES // HEAD_DIM
SUBLANES = 8
ROW_TILES = D_MODEL // LANES
TOKEN_TILE = 512
EXPERT_TILE = 512
DISPATCH_TILE = 2048
COMBINE_TILE = 512
DMA_GROUP = 8
ATTN_UNROLL = 32
MOE_CHAINS = 2
INPROJ_CHAINS = 2
OUTPROJ_CHAINS = 2
GLU_GROUP = 256
VMEM_LIMIT = 56 * 1024 * 1024

_f32 = jnp.float32
_bf16 = jnp.bfloat16


def _rms(x, g):
    return x * lax.rsqrt(jnp.mean(x * x, axis=-1, keepdims=True) + NORM_EPS) * g


def _store_token_rows(ref, x):
    tm = x.shape[0]
    for c in range(ROW_TILES):
        ref[pl.ds(c, tm, stride=ROW_TILES), :] = x[:, c * LANES:(c + 1) * LANES]


def _load_token_rows(ref, tm):
    return jnp.concatenate([ref[pl.ds(c, tm, stride=ROW_TILES), :] for c in range(ROW_TILES)], axis=1)


def _token_row(ref, tok):
    return ref.at[pl.ds(pl.multiple_of(tok * ROW_TILES, ROW_TILES), ROW_TILES)]


def _inproj_kernel(x_ref, pos_ref, inv_ref, g_ref, w_ref, b_ref, *out_refs):
    sub = x_ref.shape[0] // INPROJ_CHAINS
    for s in range(INPROJ_CHAINS):
        rows = slice(s * sub, (s + 1) * sub)
        _inproj_rows(x_ref[rows, :], pos_ref[rows, :], inv_ref, g_ref, w_ref, b_ref,
                     [r.at[rows, :] for r in out_refs])


def _inproj_rows(x, pos, inv_ref, g_ref, w_ref, b_ref, out_refs):
    qa_ref, ka_ref, va_ref, qb_ref, kb_ref, vb_ref = out_refs
    h = _rms(x, g_ref[...]).astype(_bf16)
    proj = jnp.dot(h, w_ref[...], preferred_element_type=_f32) + b_ref[...]

    ang = pos.astype(_f32) * inv_ref[...]
    lane = lax.broadcasted_iota(jnp.int32, ang.shape, 1)
    first_half = (lane % HEAD_DIM) < (HEAD_DIM // 2)
    cos = jnp.cos(ang)
    sin = jnp.where(first_half, -jnp.sin(ang), jnp.sin(ang))

    def rope(t, scale):
        outs = []
        for c in range(t.shape[1] // LANES):
            tc = t[:, c * LANES:(c + 1) * LANES]
            partner = jnp.where(first_half, pltpu.roll(tc, LANES - HEAD_DIM // 2, 1),
                                pltpu.roll(tc, HEAD_DIM // 2, 1))
            outs.append((tc * cos + partner * sin) * scale)
        return jnp.concatenate(outs, axis=1) if len(outs) > 1 else outs[0]

    scale = HEAD_DIM ** -0.5 * LOG2E
    o = 0
    qa_ref[...] = rope(proj[:, o:o + W_DIL], scale).astype(_bf16); o += W_DIL
    ka_ref[...] = rope(proj[:, o:o + W_DIL], 1.0).astype(_bf16); o += W_DIL
    va_ref[...] = proj[:, o:o + W_DIL].astype(_bf16); o += W_DIL
    qb_ref[...] = rope(proj[:, o:o + W_SWA], scale).astype(_bf16); o += W_SWA
    kb = rope(proj[:, o:o + W_KV_SWA], 1.0); o += W_KV_SWA
    vb = proj[:, o:o + W_KV_SWA]

    def dup_heads(t):
        sw = pltpu.roll(t, HEAD_DIM, 1)
        lo = lane < HEAD_DIM
        return jnp.concatenate([jnp.where(lo, t, sw), jnp.where(lo, sw, t)], axis=1)

    kb_ref[...] = dup_heads(kb).astype(_bf16)
    vb_ref[...] = dup_heads(vb).astype(_bf16)


def _inproj_call(x2, pos, inv, g, w, b):
    T = x2.shape[0]
    tm = TOKEN_TILE
    row = lambda i: (i, 0)
    fix = lambda i: (0, 0)
    widths = (W_DIL, W_DIL, W_DIL, W_SWA, PAIR * W_KV_SWA, PAIR * W_KV_SWA)
    return pl.pallas_call(
        _inproj_kernel,
        grid=(T // tm,),
        in_specs=[pl.BlockSpec((tm, D_MODEL), row), pl.BlockSpec((tm, 1), row),
                  pl.BlockSpec((1, LANES), fix), pl.BlockSpec((1, D_MODEL), fix),
                  pl.BlockSpec((D_MODEL, D_IN), fix), pl.BlockSpec((1, D_IN), fix)],
        out_specs=[pl.BlockSpec((tm, wd), row) for wd in widths],
        out_shape=[jax.ShapeDtypeStruct((T, wd), _bf16) for wd in widths],
        compiler_params=pltpu.CompilerParams(dimension_semantics=("parallel",),
                                             vmem_limit_bytes=VMEM_LIMIT),
        name="inproj",
    )(x2, pos, inv, g, w, b)


def _attn_kernel(*refs, configs, seq, with_sink):
    if with_sink:
        q_ref, k_ref, v_ref, sink_ref, o_ref = refs[:5]
        scratch = refs[5:]
    else:
        q_ref, k_ref, v_ref, o_ref = refs[:4]
        scratch = refs[4:]
    n_cfg = len(configs)
    obufs = scratch[:n_cfg]
    lbufs = scratch[n_cfg:2 * n_cfg]
    extra = list(scratch[2 * n_cfg:])
    blk = ATTN_BLOCK
    dilated = [r for _, r in configs if r > 1]
    if dilated:
        qf, kf, vf = extra[:3]
        qf[...] = q_ref[...].astype(_f32)
        kf[...] = k_ref[...].astype(_f32)
        vf[...] = v_ref[...].astype(_f32)
        extra = extra[3:]

    lane = lax.broadcasted_iota(jnp.int32, (blk, LANES), 1)
    head0 = lane < HEAD_DIM
    lane_row = lax.broadcasted_iota(jnp.int32, (1, LANES), 1)
    head_rows = [(lane_row < HEAD_DIM).astype(_bf16), (lane_row >= HEAD_DIM).astype(_bf16)]
    qi = lax.broadcasted_iota(jnp.int32, (blk, 2 * blk), 0)
    kj = lax.broadcasted_iota(jnp.int32, (blk, 2 * blk), 1)
    dist = qi - kj + blk

    base_perm = None
    for ci, (max_dist, r) in enumerate(configs):
        nb = seq // r // blk
        band = (dist >= 0) & (dist <= max_dist)
        obuf, lbuf = obufs[ci], lbufs[ci]
        if r == 1:
            qp, kp, vp = q_ref, k_ref, v_ref
        else:
            perm_bf16 = extra[:3]
            extra = extra[3:]
            later = [r2 for _, r2 in configs[ci + 1:] if r2 % r == 0]
            perm_f32 = []
            if later:
                perm_f32 = extra[:3]
                extra = extra[3:]
            r_src, srcs = 1, (qf, kf, vf)
            if base_perm is not None and r % base_perm[0] == 0:
                r_src, srcs = base_perm
            if perm_f32:
                base_perm = (r, perm_f32)
            qp, kp, vp = perm_bf16
            piece = 2 * blk
            step = r // r_src
            per_res = seq // r // piece
            len_src = seq // r_src

            def permute(idx, carry, r_src=r_src, srcs=srcs, step=step, per_res=per_res, len_src=len_src,
                        perm_bf16=perm_bf16, perm_f32=perm_f32):
                c = idx // per_res
                start = (c % r_src) * len_src + c // r_src + (step * piece) * (idx % per_res)
                src = pl.ds(start, piece, stride=step)
                dst = pl.ds(pl.multiple_of(idx * piece, piece), piece)
                for t in range(3):
                    val = srcs[t][src, :]
                    perm_bf16[t][dst, :] = val.astype(_bf16)
                    if perm_f32:
                        perm_f32[t][dst, :] = val
                return carry

            lax.fori_loop(0, seq // piece, permute, 0, unroll=4)

        def body(n, carry, r=r, nb=nb, band=band, obuf=obuf, lbuf=lbuf, qp=qp, kp=kp, vp=vp):
            j = n % nb
            cur = pl.ds(pl.multiple_of(n * blk, blk), blk)
            prev = pl.ds(pl.multiple_of(jnp.maximum(n - 1, 0) * blk, blk), blk)
            q = qp[cur, :]
            k2 = jnp.concatenate([kp[prev, :], kp[cur, :]], axis=0)
            v2 = jnp.concatenate([vp[prev, :], vp[cur, :]], axis=0)
            mask = band & (kj >= jnp.where(j > 0, 0, blk))
            outs, lses = [], []
            for hm in head_rows:
                qh = q * hm
                s = lax.dot_general(qh, k2, (((1,), (1,)), ((), ())), preferred_element_type=_f32)
                s = jnp.where(mask, s, -jnp.inf)
                m = jnp.max(s, axis=-1, keepdims=True)
                p = jnp.exp2(s - m)
                l = jnp.sum(p, axis=-1, keepdims=True)
                o = jnp.dot(p.astype(_bf16), v2, preferred_element_type=_f32) / l
                outs.append(o)
                lses.append(jnp.broadcast_to(m + jnp.log2(l), (blk, LANES)))
            start = n // nb + (r * blk) * j
            dst = pl.ds(pl.multiple_of(start, blk), blk) if r == 1 else pl.ds(start, blk, stride=r)
            obuf[dst, :] = jnp.where(head0, outs[0], outs[1])
            lbuf[dst, :] = jnp.where(head0, lses[0], lses[1])
            return carry

        lax.fori_loop(0, seq // blk, body, 0, unroll=ATTN_UNROLL)

    chunk = 4 * blk

    def finish(i, carry):
        sl = pl.ds(pl.multiple_of(i * chunk, chunk), chunk)
        if with_sink:
            o = obufs[0][sl, :] * (1.0 / (1.0 + jnp.exp2(sink_ref[...] * LOG2E - lbufs[0][sl, :])))
        else:
            ls = [lb[sl, :] for lb in lbufs]
            mx = functools.reduce(jnp.maximum, ls)
            es = [jnp.exp2(x - mx) for x in ls]
            den = functools.reduce(lambda a, b: a + b, es)
            o = functools.reduce(lambda a, b: a + b,
                                 [e / den * ob[sl, :] for e, ob in zip(es, obufs)])
        o_ref[sl, :] = o
        return carry

    lax.fori_loop(0, seq // chunk, finish, 0)


def _attn_call(q, k, v, sinks, *, batch, seq, configs, kv_group):
    n_pairs = q.shape[1] // LANES
    with_sink = sinks is not None
    qmap = lambda b, p: (b, p)
    kvmap = lambda b, p: (b, p // kv_group)
    in_specs = [pl.BlockSpec((seq, LANES), qmap), pl.BlockSpec((seq, LANES), kvmap),
                pl.BlockSpec((seq, LANES), kvmap)]
    args = [q, k, v]
    if with_sink:
        in_specs.append(pl.BlockSpec((1, LANES), lambda b, p: (0, p)))
        args.append(sinks)
    buf = lambda dt: pltpu.VMEM((seq, LANES), dt)
    scratch = [buf(_f32) for _ in range(2 * len(configs))]
    if any(r > 1 for _, r in configs):
        scratch += [buf(_f32) for _ in range(3)]
    for ci, (_, r) in enumerate(configs):
        if r > 1:
            scratch += [buf(_bf16) for _ in range(3)]
            if any(r2 % r == 0 for _, r2 in configs[ci + 1:]):
                scratch += [buf(_f32) for _ in range(3)]
    return pl.pallas_call(
        functools.partial(_attn_kernel, configs=configs, seq=seq, with_sink=with_sink),
        grid=(batch, n_pairs),
        in_specs=in_specs,
        out_specs=pl.BlockSpec((seq, LANES), qmap),
        out_shape=jax.ShapeDtypeStruct(q.shape, _f32),
        scratch_shapes=scratch,
        compiler_params=pltpu.CompilerParams(dimension_semantics=("parallel", "parallel"),
                                             vmem_limit_bytes=VMEM_LIMIT),
        name="attn_sink" if with_sink else "attn_dilated",
    )(*args)


def _outproj_kernel(od_ref, os_ref, x_ref, gd_ref, gs_ref, w_ref, b_ref, g2_ref, wr_ref, br_ref,
                    x1_ref, ri_ref, rg_ref, cnt_ref, carry_ref):
    tm = x_ref.shape[0]

    @pl.when(pl.program_id(0) == 0)
    def _():
        carry_ref[...] = jnp.zeros_like(carry_ref)

    sub = tm // OUTPROJ_CHAINS
    h2 = []
    for s in range(OUTPROJ_CHAINS):
        rows = slice(s * sub, (s + 1) * sub)
        mixed = jnp.concatenate([_rms(od_ref[rows, :], gd_ref[...]), _rms(os_ref[rows, :], gs_ref[...])], axis=1)
        x1 = x_ref[rows, :] + jnp.dot(mixed.astype(_bf16), w_ref[...], preferred_element_type=_f32) + b_ref[...]
        _store_token_rows(x1_ref.at[pl.ds(s * sub * ROW_TILES, sub * ROW_TILES)], x1)
        h2.append(_rms(x1, g2_ref[...]))
    h2 = jnp.concatenate(h2, axis=0)
    logits = lax.dot_general(wr_ref[...], h2, (((1,), (1,)), ((), ())),
                             precision=lax.Precision.HIGHEST, preferred_element_type=_f32) + br_ref[...]
    eiota = lax.broadcasted_iota(jnp.int32, logits.shape, 0).astype(_f32)
    vals, idxs, sels = [], [], []
    l = logits
    for _ in range(TOP_K):
        m = jnp.max(l, axis=0, keepdims=True)
        idx = jnp.min(jnp.where(l == m, eiota, float(N_EXPERTS)), axis=0, keepdims=True)
        sel = eiota == idx
        vals.append(m); idxs.append(idx); sels.append(sel)
        l = jnp.where(sel, -jnp.inf, l)
    exps = [jnp.exp(v - vals[0]) for v in vals]
    den = functools.reduce(lambda a, b: a + b, exps)
    gates = [e / den for e in exps]

    chosen = functools.reduce(jnp.logical_or, sels).astype(_f32)
    ti = lax.broadcasted_iota(jnp.int32, (tm, tm), 0)
    tj = lax.broadcasted_iota(jnp.int32, (tm, tm), 1)
    before = (ti < tj).astype(_bf16)
    prefix = jnp.dot(chosen.astype(_bf16), before, preferred_element_type=_f32) + carry_ref[...]
    ranks = [jnp.sum(jnp.where(s, prefix, 0.0), axis=0, keepdims=True).astype(jnp.int32) for s in sels]
    carry_ref[...] = carry_ref[...] + jnp.sum(chosen, axis=1, keepdims=True)

    ri_ref[...] = jnp.concatenate([ix.astype(jnp.int32) for ix in idxs] + ranks, axis=0)
    rg_ref[...] = jnp.concatenate(gates + gates, axis=0)
    cnt_ref[...] = jnp.broadcast_to(carry_ref[...], cnt_ref.shape)


def _outproj_call(od, osw, x2, gd, gs, w, b, g2, wr_t, br):
    T = x2.shape[0]
    tm = TOKEN_TILE
    row = lambda i: (i, 0)
    col = lambda i: (0, i)
    fix = lambda i: (0, 0)
    return pl.pallas_call(
        _outproj_kernel,
        grid=(T // tm,),
        in_specs=[pl.BlockSpec((tm, W_DIL), row), pl.BlockSpec((tm, W_SWA), row),
                  pl.BlockSpec((tm, D_MODEL), row),
                  pl.BlockSpec((1, W_DIL), fix), pl.BlockSpec((1, W_SWA), fix),
                  pl.BlockSpec((D_MODEL, D_MODEL), fix), pl.BlockSpec((1, D_MODEL), fix),
                  pl.BlockSpec((1, D_MODEL), fix),
                  pl.BlockSpec((N_EXPERTS, D_MODEL), fix), pl.BlockSpec((N_EXPERTS, 1), fix)],
        out_specs=[pl.BlockSpec((tm * ROW_TILES, LANES), row), pl.BlockSpec((2 * TOP_K, tm), col),
                   pl.BlockSpec((2 * TOP_K, tm), col), pl.BlockSpec((N_EXPERTS, LANES), fix)],
        out_shape=[jax.ShapeDtypeStruct((T * ROW_TILES, LANES), _f32),
                   jax.ShapeDtypeStruct((2 * TOP_K, T), jnp.int32),
                   jax.ShapeDtypeStruct((2 * TOP_K, T), _f32),
                   jax.ShapeDtypeStruct((N_EXPERTS, LANES), _f32)],
        scratch_shapes=[pltpu.VMEM((N_EXPERTS, 1), _f32)],
        compiler_params=pltpu.CompilerParams(dimension_semantics=("arbitrary",),
                                             vmem_limit_bytes=VMEM_LIMIT),
        name="outproj_router",
    )(od, osw, x2, gd, gs, w, b, g2, wr_t, br)


def _dispatch_kernel(d_ref, x1_ref, xs_hbm, sem):
    td = DISPATCH_TILE

    def start_group(g, c):
        for rr in range(DMA_GROUP):
            r = g * DMA_GROUP + rr
            src = _token_row(x1_ref, r)
            for k in range(TOP_K):
                pltpu.make_async_copy(src, _token_row(xs_hbm, d_ref[0, k * td + r]), sem).start(priority=k % 2)
        return c

    lax.fori_loop(0, td // DMA_GROUP, start_group, 0)

    def wait_group(g, c):
        for _ in range(DMA_GROUP * TOP_K):
            pltpu.make_async_copy(_token_row(x1_ref, 0), _token_row(xs_hbm, 0), sem).wait()
        return c

    lax.fori_loop(0, td // DMA_GROUP, wait_group, 0)


def _dispatch_call(dest_tiles, x1r, n_slots):
    nt = dest_tiles.shape[0]
    return pl.pallas_call(
        _dispatch_kernel,
        grid=(nt,),
        in_specs=[pl.BlockSpec((None, 1, TOP_K * DISPATCH_TILE), lambda i: (i, 0, 0), memory_space=pltpu.SMEM),
                  pl.BlockSpec((DISPATCH_TILE * ROW_TILES, LANES), lambda i: (i, 0))],
        out_specs=pl.BlockSpec(memory_space=pl.ANY),
        out_shape=jax.ShapeDtypeStruct((n_slots * ROW_TILES, LANES), _f32),
        scratch_shapes=[pltpu.SemaphoreType.DMA(())],
        compiler_params=pltpu.CompilerParams(dimension_semantics=("arbitrary",),
                                             vmem_limit_bytes=VMEM_LIMIT),
        name="dispatch_rows",
    )(dest_tiles, x1r)


def _moe_kernel(be_ref, nv_ref, src_ref, grp_ref, nxt_ref, xs_ref, g2_ref, w1_hbm, b1_ref, w2_hbm, b2_ref,
                ys_ref, st1, st2, w1p, w2b, sem):
    i = pl.program_id(0)
    nv = nv_ref[i]
    tm = EXPERT_TILE

    def weight_copies(e, slot):
        return (pltpu.make_async_copy(w1_hbm.at[e], st1.at[slot], sem.at[0, slot]),
                pltpu.make_async_copy(w2_hbm.at[e], st2.at[slot], sem.at[1, slot]))

    @pl.when(nv > 0)
    def _():
        first_of_expert = jnp.logical_or(i == 0, be_ref[i] != be_ref[jnp.maximum(i - 1, 0)])

        @pl.when(first_of_expert)
        def _():
            slot = grp_ref[i] % 2

            @pl.when(i == 0)
            def _():
                for c in weight_copies(be_ref[0], 0):
                    c.start()

            for c in weight_copies(be_ref[i], slot):
                c.wait()

            @pl.when(nxt_ref[i] >= 0)
            def _():
                for c in weight_copies(nxt_ref[i], 1 - slot):
                    c.start()

            src = lax.broadcasted_iota(jnp.int32, (GLU_GROUP, GLU_GROUP), 0)
            dst = lax.broadcasted_iota(jnp.int32, (GLU_GROUP, GLU_GROUP), 1)
            half = GLU_GROUP // 2
            perm = (src == jnp.where(dst < half, 2 * dst, 2 * (dst - half) + 1)).astype(_bf16)
            for g in range(2 * D_FF // GLU_GROUP):
                cols = slice(g * GLU_GROUP, (g + 1) * GLU_GROUP)
                w1p[:, cols] = jnp.dot(st1[slot, :, cols].astype(_bf16), perm,
                                       preferred_element_type=_f32).astype(_bf16)
            w2b[...] = st2[slot].astype(_bf16)

        hm = tm // MOE_CHAINS

        def sub_block(s):
            rows = lambda c: pl.ds(s * hm * ROW_TILES + c, hm, stride=ROW_TILES)
            x = jnp.concatenate([xs_ref[rows(c), :] for c in range(ROW_TILES)], axis=1)
            x = jnp.where(lax.broadcasted_iota(jnp.int32, (hm, 1), 0) < nv - s * hm, x, 0.0)
            h = _rms(x, g2_ref[...]).astype(_bf16)
            u = jnp.dot(h, w1p[...], preferred_element_type=_f32) + b1_ref[0]
            half = GLU_GROUP // 2
            hid = []
            for g in range(2 * D_FF // GLU_GROUP):
                glu = jnp.minimum(u[:, g * GLU_GROUP:g * GLU_GROUP + half], SWIGLU_LIMIT)
                lin = jnp.clip(u[:, g * GLU_GROUP + half:(g + 1) * GLU_GROUP], -SWIGLU_LIMIT, SWIGLU_LIMIT)
                hid.append(glu * (1.0 / (1.0 + jnp.exp(-SWIGLU_ALPHA * glu))) * (lin + 1.0))
            hid = jnp.concatenate(hid, axis=1).astype(_bf16)
            y = jnp.dot(hid, w2b[...], preferred_element_type=_f32) + b2_ref[0]
            for c in range(ROW_TILES):
                ys_ref[rows(c), :] = y[:, c * LANES:(c + 1) * LANES]

        @pl.when(nv > hm)
        def _():
            for s in range(MOE_CHAINS):
                sub_block(s)

        @pl.when(nv <= hm)
        def _():
            sub_block(0)


def _moe_call(block_e, n_valid, block_src, block_grp, next_e, xs, g2, w1, b1p, w2, b2):
    nblk = block_e.shape[0]
    tm = EXPERT_TILE
    by_e = lambda i, be, nv, src, grp, nxt: (be[i], 0, 0)
    by_src = lambda i, be, nv, src, grp, nxt: (src[i], 0)
    fix = lambda i, be, nv, src, grp, nxt: (0, 0)
    grid_spec = pltpu.PrefetchScalarGridSpec(
        num_scalar_prefetch=5,
        grid=(nblk,),
        in_specs=[pl.BlockSpec((tm * ROW_TILES, LANES), by_src), pl.BlockSpec((1, D_MODEL), fix),
                  pl.BlockSpec(memory_space=pl.ANY), pl.BlockSpec((1, 1, 2 * D_FF), by_e),
                  pl.BlockSpec(memory_space=pl.ANY), pl.BlockSpec((1, 1, D_MODEL), by_e)],
        out_specs=pl.BlockSpec((tm * ROW_TILES, LANES), by_src),
        scratch_shapes=[pltpu.VMEM((2, D_MODEL, 2 * D_FF), _f32), pltpu.VMEM((2, D_FF, D_MODEL), _f32),
                        pltpu.VMEM((D_MODEL, 2 * D_FF), _bf16), pltpu.VMEM((D_FF, D_MODEL), _bf16),
                        pltpu.SemaphoreType.DMA((2, 2))],
    )
    return pl.pallas_call(
        _moe_kernel,
        grid_spec=grid_spec,
        out_shape=jax.ShapeDtypeStruct(xs.shape, _f32),
        compiler_params=pltpu.CompilerParams(dimension_semantics=("arbitrary",),
                                             vmem_limit_bytes=VMEM_LIMIT),
        name="moe_experts",
    )(block_e, n_valid, block_src, block_grp, next_e, xs, g2, w1, b1p, w2, b2)


def _combine_kernel(dcur_ref, dnext_ref, x1_ref, ys_hbm, g_ref, gf_ref, o_ref, gbuf, sem):
    tc = COMBINE_TILE
    i = pl.program_id(0)
    slot = i % 2

    def row_copy(d_ref, s, r, k):
        dst = gbuf.at[s, k, pl.ds(r * ROW_TILES, ROW_TILES)]
        return pltpu.make_async_copy(_token_row(ys_hbm, d_ref[0, k * tc + r]), dst, sem.at[s])

    def issue(d_ref, s):
        def group(g, c):
            for rr in range(DMA_GROUP):
                for k in range(TOP_K):
                    row_copy(d_ref, s, g * DMA_GROUP + rr, k).start(priority=k % 2)
            return c
        lax.fori_loop(0, tc // DMA_GROUP, group, 0)

    @pl.when(i == 0)
    def _():
        issue(dcur_ref, 0)

    @pl.when(i + 1 < pl.num_programs(0))
    def _():
        issue(dnext_ref, 1 - slot)

    def wait_group(g, c):
        for _ in range(DMA_GROUP * TOP_K):
            row_copy(dcur_ref, slot, 0, 0).wait()
        return c

    lax.fori_loop(0, tc // DMA_GROUP, wait_group, 0)

    g = g_ref[...]
    acc = _load_token_rows(x1_ref, tc)
    for k in range(TOP_K):
        acc = acc + g[:, k:k + 1] * _load_token_rows(gbuf.at[slot, k], tc)
    o_ref[...] = _rms(acc, gf_ref[...])


def _combine_call(dest_tiles, x1r, ys, gates_t, gf):
    nt = dest_tiles.shape[0]
    tc = COMBINE_TILE
    row = lambda i: (i, 0)
    return pl.pallas_call(
        _combine_kernel,
        grid=(nt,),
        in_specs=[pl.BlockSpec((None, 1, TOP_K * tc), lambda i: (i, 0, 0), memory_space=pltpu.SMEM),
                  pl.BlockSpec((None, 1, TOP_K * tc), lambda i: (jnp.minimum(i + 1, nt - 1), 0, 0),
                               memory_space=pltpu.SMEM),
                  pl.BlockSpec((tc * ROW_TILES, LANES), row), pl.BlockSpec(memory_space=pl.ANY),
                  pl.BlockSpec((tc, TOP_K), row), pl.BlockSpec((1, D_MODEL), lambda i: (0, 0))],
        out_specs=pl.BlockSpec((tc, D_MODEL), row),
        out_shape=jax.ShapeDtypeStruct((nt * tc, D_MODEL), _f32),
        scratch_shapes=[pltpu.VMEM((2, TOP_K, tc * ROW_TILES, LANES), _f32), pltpu.SemaphoreType.DMA((2,))],
        compiler_params=pltpu.CompilerParams(dimension_semantics=("arbitrary",),
                                             vmem_limit_bytes=VMEM_LIMIT),
        name="combine_norm",
    )(dest_tiles, dest_tiles, x1r, ys, gates_t, gf)


def _plan_blocks(route_i, counts, n_tokens):
    tm = EXPERT_TILE
    nblk = (n_tokens * TOP_K + N_EXPERTS * tm) // tm
    counts = counts.astype(jnp.int32)
    blocks_e = (counts + tm - 1) // tm
    blk_end = jnp.cumsum(blocks_e)
    pad_start = (blk_end - blocks_e) * tm
    experts = jnp.arange(N_EXPERTS, dtype=jnp.int32)
    hot = route_i[:TOP_K, :, None] == experts
    dest = jnp.sum(jnp.where(hot, pad_start, 0), axis=-1) + route_i[TOP_K:]
    n_used = blk_end[-1]
    blk = jnp.arange(nblk, dtype=jnp.int32)
    src = jnp.minimum(blk, jnp.maximum(n_used - 1, 0))
    be = jnp.minimum(jnp.sum((blk_end[None, :] <= src[:, None]).astype(jnp.int32), axis=1), N_EXPERTS - 1)
    hot_b = be[:, None] == experts
    pick = lambda table: jnp.sum(jnp.where(hot_b, table, 0), axis=1)
    n_valid = jnp.where(blk < n_used, jnp.clip(pick(counts) - (src * tm - pick(pad_start)), 0, tm), 0)
    used = blocks_e > 0
    grp_e = jnp.cumsum(used.astype(jnp.int32)) - 1
    later = jnp.where((experts[None, :] > experts[:, None]) & used[None, :], experts[None, :], N_EXPERTS)
    nxt_e = jnp.min(later, axis=1)
    nxt_e = jnp.where(nxt_e < N_EXPERTS, nxt_e, -1)
    return be, n_valid.astype(jnp.int32), src, pick(grp_e), pick(nxt_e), dest


def _tile_dest(dest, tile):
    k, t = dest.shape
    return dest.reshape(k, t // tile, tile).transpose(1, 0, 2).reshape(t // tile, 1, k * tile)


def kernel(x, positions, norm1_g, w_in, b_in, sinks, gn_dil, gn_swa, w_out, b_out,
           norm2_g, w_router, b_router, w_mlp1, b_mlp1, w_mlp2, b_mlp2, norm_f_g):
    batch, seq, d = x.shape
    T = batch * seq
    x2 = x.reshape(T, d)
    depth = norm1_g.shape[0]
    assert depth == 1, "the final norm is fused into the last layer's combine kernel"
    inv = ROPE_THETA ** (-jnp.arange(0, HEAD_DIM, 2, dtype=_f32) / HEAD_DIM)
    inv = jnp.tile(inv, LANES // (HEAD_DIM // 2)).reshape(1, LANES)
    pos = positions.reshape(T, 1)
    for layer in range(depth):
        qa, ka, va, qb, kb, vb = _inproj_call(
            x2, pos, inv, norm1_g[layer].reshape(1, d), w_in[layer].astype(_bf16), b_in[layer].reshape(1, D_IN))
        o_dil = _attn_call(qa, ka, va, None, batch=batch, seq=seq,
                           configs=tuple((w // r, r) for w, r in DILATED_CONFIGS), kv_group=1)
        sink_lanes = jnp.repeat(sinks[layer].astype(_f32), HEAD_DIM).reshape(1, W_SWA)
        o_swa = _attn_call(qb, kb, vb, sink_lanes, batch=batch, seq=seq,
                           configs=((SWA_WINDOW - 1, 1),), kv_group=N_HEADS_SWA // N_KV_SWA // PAIR)
        x1r, route_i, route_g, counts = _outproj_call(
            o_dil, o_swa, x2, gn_dil[layer].reshape(1, W_DIL), gn_swa[layer].reshape(1, W_SWA),
            w_out[layer].astype(_bf16), b_out[layer].reshape(1, d), norm2_g[layer].reshape(1, d),
            w_router[layer].T, b_router[layer].reshape(N_EXPERTS, 1))
        block_e, n_valid, block_src, block_grp, next_e, dest = _plan_blocks(route_i, counts[:, 0], T)
        xs = _dispatch_call(_tile_dest(dest, DISPATCH_TILE), x1r, block_e.shape[0] * EXPERT_TILE)
        half = GLU_GROUP // 2
        b1p = b_mlp1[layer].reshape(N_EXPERTS, 2 * D_FF // GLU_GROUP, half, 2)
        b1p = jnp.swapaxes(b1p, 2, 3).reshape(N_EXPERTS, 1, 2 * D_FF)
        ys = _moe_call(block_e, n_valid, block_src, block_grp, next_e, xs, norm2_g[layer].reshape(1, d),
                       w_mlp1[layer], b1p, w_mlp2[layer], b_mlp2[layer].reshape(N_EXPERTS, 1, d))
        x2 = _combine_call(_tile_dest(dest, COMBINE_TILE), x1r, ys, route_g[:TOP_K].T, norm_f_g.reshape(1, d))
    return x2.reshape(batch, seq, d)
```

```python
import functools

import jax
import jax.numpy as jnp
from jax import lax
from jax.experimental import pallas as pl
from jax.experimental.pallas import tpu as pltpu

D_MODEL = 1024
HEAD_DIM = 64
N_HEADS_DIL = 8
DILATED_CONFIGS = ((128, 1), (512, 4), (2048, 16))
N_HEADS_SWA = 8
N_KV_SWA = 2
SWA_WINDOW = 128
ATTN_BLOCK = 128
ROPE_THETA = 10000.0
N_EXPERTS = 32
TOP_K = 4
D_FF = D_MODEL
SWIGLU_ALPHA = 1.702
SWIGLU_LIMIT = 7.0
NORM_EPS = 1e-5
LOG2E = 1.4426950408889634

W_DIL = N_HEADS_DIL * HEAD_DIM
W_SWA = N_HEADS_SWA * HEAD_DIM
W_KV_SWA = N_KV_SWA * HEAD_DIM
D_IN = 3 * W_DIL + W_SWA + 2 * W_KV_SWA

LANES = 128
PAIR = LANES // HEAD_DIM
SUBLANES = 8
ROW_TILES = D_MODEL // LANES
TOKEN_TILE = 512
EXPERT_TILE = 1024
DISPATCH_TILE = 2048
COMBINE_TILE = 256
DMA_GROUP = 8
ATTN_UNROLL = 32
MOE_CHAINS = 4
INPROJ_TILE = 1024
INPROJ_CHAINS = 4
OUTPROJ_CHAINS = 2
GLU_GROUP = 256
VMEM_LIMIT = 56 * 1024 * 1024

_f32 = jnp.float32
_bf16 = jnp.bfloat16


def _rms(x, g):
    return x * lax.rsqrt(jnp.mean(x * x, axis=-1, keepdims=True) + NORM_EPS) * g


def _store_token_rows(ref, x):
    tm = x.shape[0]
    for c in range(ROW_TILES):
        ref[pl.ds(c, tm, stride=ROW_TILES), :] = x[:, c * LANES:(c + 1) * LANES]


def _load_token_rows(ref, tm):
    return jnp.concatenate([ref[pl.ds(c, tm, stride=ROW_TILES), :] for c in range(ROW_TILES)], axis=1)


def _token_row(ref, tok):
    return ref.at[pl.ds(pl.multiple_of(tok * ROW_TILES, ROW_TILES), ROW_TILES)]


def _inproj_kernel(x_ref, pos_ref, inv_ref, g_ref, w_ref, b_ref, *out_refs):
    sub = x_ref.shape[0] // INPROJ_CHAINS
    for s in range(INPROJ_CHAINS):
        rows = slice(s * sub, (s + 1) * sub)
        _inproj_rows(x_ref[rows, :], pos_ref[rows, :], inv_ref, g_ref, w_ref, b_ref,
                     [r.at[rows, :] for r in out_refs])


def _inproj_rows(x, pos, inv_ref, g_ref, w_ref, b_ref, out_refs):
    qa_ref, ka_ref, va_ref, qb_ref, kb_ref, vb_ref = out_refs
    h = _rms(x, g_ref[...]).astype(_bf16)
    proj = jnp.dot(h, w_ref[...], preferred_element_type=_f32) + b_ref[...]

    ang = pos.astype(_f32) * inv_ref[...]
    lane = lax.broadcasted_iota(jnp.int32, ang.shape, 1)
    first_half = (lane % HEAD_DIM) < (HEAD_DIM // 2)
    cos = jnp.cos(ang)
    sin = jnp.where(first_half, -jnp.sin(ang), jnp.sin(ang))

    def rope(t, scale):
        outs = []
        for c in range(t.shape[1] // LANES):
            tc = t[:, c * LANES:(c + 1) * LANES]
            partner = jnp.where(first_half, pltpu.roll(tc, LANES - HEAD_DIM // 2, 1),
                                pltpu.roll(tc, HEAD_DIM // 2, 1))
            outs.append((tc * cos + partner * sin) * scale)
        return jnp.concatenate(outs, axis=1) if len(outs) > 1 else outs[0]

    scale = HEAD_DIM ** -0.5 * LOG2E
    o = 0
    qa_ref[...] = rope(proj[:, o:o + W_DIL], scale).astype(_bf16); o += W_DIL
    ka_ref[...] = rope(proj[:, o:o + W_DIL], 1.0).astype(_bf16); o += W_DIL
    va_ref[...] = proj[:, o:o + W_DIL].astype(_bf16); o += W_DIL
    qb_ref[...] = rope(proj[:, o:o + W_SWA], scale).astype(_bf16); o += W_SWA
    kb = rope(proj[:, o:o + W_KV_SWA], 1.0); o += W_KV_SWA
    vb = proj[:, o:o + W_KV_SWA]

    def dup_heads(t):
        sw = pltpu.roll(t, HEAD_DIM, 1)
        lo = lane < HEAD_DIM
        return jnp.concatenate([jnp.where(lo, t, sw), jnp.where(lo, sw, t)], axis=1)

    kb_ref[...] = dup_heads(kb).astype(_bf16)
    vb_ref[...] = dup_heads(vb).astype(_bf16)


def _inproj_call(x2, pos, inv, g, w, b):
    T = x2.shape[0]
    tm = INPROJ_TILE
    row = lambda i: (i, 0)
    fix = lambda i: (0, 0)
    widths = (W_DIL, W_DIL, W_DIL, W_SWA, PAIR * W_KV_SWA, PAIR * W_KV_SWA)
    return pl.pallas_call(
        _inproj_kernel,
        grid=(T // tm,),
        in_specs=[pl.BlockSpec((tm, D_MODEL), row), pl.BlockSpec((tm, 1), row),
                  pl.BlockSpec((1, LANES), fix), pl.BlockSpec((1, D_MODEL), fix),
                  pl.BlockSpec((D_MODEL, D_IN), fix), pl.BlockSpec((1, D_IN), fix)],
        out_specs=[pl.BlockSpec((tm, wd), row) for wd in widths],
        out_shape=[jax.ShapeDtypeStruct((T, wd), _bf16) for wd in widths],
        compiler_params=pltpu.CompilerParams(dimension_semantics=("parallel",),
                                             vmem_limit_bytes=VMEM_LIMIT),
        name="inproj",
    )(x2, pos, inv, g, w, b)


def _attn_kernel(*refs, configs, seq, with_sink):
    if with_sink:
        q_ref, k_ref, v_ref, sink_ref, o_ref = refs[:5]
        scratch = refs[5:]
    else:
        q_ref, k_ref, v_ref, o_ref = refs[:4]
        scratch = refs[4:]
    n_cfg = len(configs)
    obufs = scratch[:n_cfg]
    lbufs = scratch[n_cfg:2 * n_cfg]
    extra = list(scratch[2 * n_cfg:])
    blk = ATTN_BLOCK
    dilated = [r for _, r in configs if r > 1]
    if dilated:
        qf, kf, vf = extra[:3]
        qf[...] = q_ref[...].astype(_f32)
        kf[...] = k_ref[...].astype(_f32)
        vf[...] = v_ref[...].astype(_f32)
        extra = extra[3:]

    lane = lax.broadcasted_iota(jnp.int32, (blk, LANES), 1)
    head0 = lane < HEAD_DIM
    lane_row = lax.broadcasted_iota(jnp.int32, (1, LANES), 1)
    head_rows = [(lane_row < HEAD_DIM).astype(_bf16), (lane_row >= HEAD_DIM).astype(_bf16)]
    qi = lax.broadcasted_iota(jnp.int32, (blk, 2 * blk), 0)
    kj = lax.broadcasted_iota(jnp.int32, (blk, 2 * blk), 1)
    dist = qi - kj + blk

    base_perm = None
    for ci, (max_dist, r) in enumerate(configs):
        nb = seq // r // blk
        band = (dist >= 0) & (dist <= max_dist)
        obuf, lbuf = obufs[ci], lbufs[ci]
        if r == 1:
            qp, kp, vp = q_ref, k_ref, v_ref
        else:
            perm_bf16 = extra[:3]
            extra = extra[3:]
            later = [r2 for _, r2 in configs[ci + 1:] if r2 % r == 0]
            perm_f32 = []
            if later:
                perm_f32 = extra[:3]
                extra = extra[3:]
            r_src, srcs = 1, (qf, kf, vf)
            if base_perm is not None and r % base_perm[0] == 0:
                r_src, srcs = base_perm
            if perm_f32:
                base_perm = (r, perm_f32)
            qp, kp, vp = perm_bf16
            piece = 2 * blk
            step = r // r_src
            per_res = seq // r // piece
            len_src = seq // r_src

            def permute(idx, carry, r_src=r_src, srcs=srcs, step=step, per_res=per_res, len_src=len_src,
                        perm_bf16=perm_bf16, perm_f32=perm_f32):
                c = idx // per_res
                start = (c % r_src) * len_src + c // r_src + (step * piece) * (idx % per_res)
                src = pl.ds(start, piece, stride=step)
                dst = pl.ds(pl.multiple_of(idx * piece, piece), piece)
                for t in range(3):
                    val = srcs[t][src, :]
                    perm_bf16[t][dst, :] = val.astype(_bf16)
                    if perm_f32:
                        perm_f32[t][dst, :] = val
                return carry

            lax.fori_loop(0, seq // piece, permute, 0, unroll=4)

        def body(n, carry, r=r, nb=nb, band=band, obuf=obuf, lbuf=lbuf, qp=qp, kp=kp, vp=vp):
            j = n % nb
            cur = pl.ds(pl.multiple_of(n * blk, blk), blk)
            prev = pl.ds(pl.multiple_of(jnp.maximum(n - 1, 0) * blk, blk), blk)
            q = qp[cur, :]
            k2 = jnp.concatenate([kp[prev, :], kp[cur, :]], axis=0)
            v2 = jnp.concatenate([vp[prev, :], vp[cur, :]], axis=0)
            mask = band & (kj >= jnp.where(j > 0, 0, blk))
            outs, lses = [], []
            for hm in head_rows:
                qh = q * hm
                s = lax.dot_general(qh, k2, (((1,), (1,)), ((), ())), preferred_element_type=_f32)
                s = jnp.where(mask, s, -jnp.inf)
                m = jnp.max(s, axis=-1, keepdims=True)
                p = jnp.exp2(s - m)
                l = jnp.sum(p, axis=-1, keepdims=True)
                o = jnp.dot(p.astype(_bf16), v2, preferred_element_type=_f32) / l
                outs.append(o)
                lses.append(jnp.broadcast_to(m + jnp.log2(l), (blk, LANES)))
            start = n // nb + (r * blk) * j
            dst = pl.ds(pl.multiple_of(start, blk), blk) if r == 1 else pl.ds(start, blk, stride=r)
            obuf[dst, :] = jnp.where(head0, outs[0], outs[1])
            lbuf[dst, :] = jnp.where(head0, lses[0], lses[1])
            return carry

        lax.fori_loop(0, seq // blk, body, 0, unroll=ATTN_UNROLL)

    chunk = 4 * blk

    def finish(i, carry):
        sl = pl.ds(pl.multiple_of(i * chunk, chunk), chunk)
        if with_sink:
            o = obufs[0][sl, :] * (1.0 / (1.0 + jnp.exp2(sink_ref[...] * LOG2E - lbufs[0][sl, :])))
        else:
            ls = [lb[sl, :] for lb in lbufs]
            mx = functools.reduce(jnp.maximum, ls)
            es = [jnp.exp2(x - mx) for x in ls]
            den = functools.reduce(lambda a, b: a + b, es)
            o = functools.reduce(lambda a, b: a + b,
                                 [e / den * ob[sl, :] for e, ob in zip(es, obufs)])
        o_ref[sl, :] = o
        return carry

    lax.fori_loop(0, seq // chunk, finish, 0)


def _attn_call(q, k, v, sinks, *, batch, seq, configs, kv_group):
    n_pairs = q.shape[1] // LANES
    with_sink = sinks is not None
    qmap = lambda b, p: (b, p)
    kvmap = lambda b, p: (b, p // kv_group)
    in_specs = [pl.BlockSpec((seq, LANES), qmap), pl.BlockSpec((seq, LANES), kvmap),
                pl.BlockSpec((seq, LANES), kvmap)]
    args = [q, k, v]
    if with_sink:
        in_specs.append(pl.BlockSpec((1, LANES), lambda b, p: (0, p)))
        args.append(sinks)
    buf = lambda dt: pltpu.VMEM((seq, LANES), dt)
    scratch = [buf(_f32) for _ in range(2 * len(configs))]
    if any(r > 1 for _, r in configs):
        scratch += [buf(_f32) for _ in range(3)]
    for ci, (_, r) in enumerate(configs):
        if r > 1:
            scratch += [buf(_bf16) for _ in range(3)]
            if any(r2 % r == 0 for _, r2 in configs[ci + 1:]):
                scratch += [buf(_f32) for _ in range(3)]
    return pl.pallas_call(
        functools.partial(_attn_kernel, configs=configs, seq=seq, with_sink=with_sink),
        grid=(batch, n_pairs),
        in_specs=in_specs,
        out_specs=pl.BlockSpec((seq, LANES), qmap),
        out_shape=jax.ShapeDtypeStruct(q.shape, _f32),
        scratch_shapes=scratch,
        compiler_params=pltpu.CompilerParams(dimension_semantics=("parallel", "parallel"),
                                             vmem_limit_bytes=VMEM_LIMIT),
        name="attn_sink" if with_sink else "attn_dilated",
    )(*args)


def _outproj_kernel(od_ref, os_ref, x_ref, gd_ref, gs_ref, w_ref, b_ref, g2_ref, wr_ref, br_ref,
                    x1_ref, ri_ref, rg_ref, cnt_ref, carry_ref):
    tm = x_ref.shape[0]

    @pl.when(pl.program_id(0) == 0)
    def _():
        carry_ref[...] = jnp.zeros_like(carry_ref)

    sub = tm // OUTPROJ_CHAINS
    h2 = []
    for s in range(OUTPROJ_CHAINS):
        rows = slice(s * sub, (s + 1) * sub)
        mixed = jnp.concatenate([_rms(od_ref[rows, :], gd_ref[...]), _rms(os_ref[rows, :], gs_ref[...])], axis=1)
        x1 = x_ref[rows, :] + jnp.dot(mixed.astype(_bf16), w_ref[...], preferred_element_type=_f32) + b_ref[...]
        _store_token_rows(x1_ref.at[pl.ds(s * sub * ROW_TILES, sub * ROW_TILES)], x1)
        h2.append(_rms(x1, g2_ref[...]))
    h2 = jnp.concatenate(h2, axis=0)
    logits = lax.dot_general(wr_ref[...], h2, (((1,), (1,)), ((), ())),
                             precision=lax.Precision.HIGHEST, preferred_element_type=_f32) + br_ref[...]
    eiota = lax.broadcasted_iota(jnp.int32, logits.shape, 0).astype(_f32)
    vals, idxs, sels = [], [], []
    l = logits
    for _ in range(TOP_K):
        m = jnp.max(l, axis=0, keepdims=True)
        idx = jnp.min(jnp.where(l == m, eiota, float(N_EXPERTS)), axis=0, keepdims=True)
        sel = eiota == idx
        vals.append(m); idxs.append(idx); sels.append(sel)
        l = jnp.where(sel, -jnp.inf, l)
    exps = [jnp.exp(v - vals[0]) for v in vals]
    den = functools.reduce(lambda a, b: a + b, exps)
    gates = [e / den for e in exps]

    chosen = functools.reduce(jnp.logical_or, sels).astype(_f32)
    ti = lax.broadcasted_iota(jnp.int32, (tm, tm), 0)
    tj = lax.broadcasted_iota(jnp.int32, (tm, tm), 1)
    before = (ti < tj).astype(_bf16)
    prefix = jnp.dot(chosen.astype(_bf16), before, preferred_element_type=_f32) + carry_ref[...]
    ranks = [jnp.sum(jnp.where(s, prefix, 0.0), axis=0, keepdims=True).astype(jnp.int32) for s in sels]
    carry_ref[...] = carry_ref[...] + jnp.sum(chosen, axis=1, keepdims=True)

    ri_ref[...] = jnp.concatenate([ix.astype(jnp.int32) for ix in idxs] + ranks, axis=0)
    rg_ref[...] = jnp.concatenate(gates + gates, axis=0)
    cnt_ref[...] = jnp.broadcast_to(carry_ref[...], cnt_ref.shape)


def _outproj_call(od, osw, x2, gd, gs, w, b, g2, wr_t, br):
    T = x2.shape[0]
    tm = TOKEN_TILE
    row = lambda i: (i, 0)
    col = lambda i: (0, i)
    fix = lambda i: (0, 0)
    return pl.pallas_call(
        _outproj_kernel,
        grid=(T // tm,),
        in_specs=[pl.BlockSpec((tm, W_DIL), row), pl.BlockSpec((tm, W_SWA), row),
                  pl.BlockSpec((tm, D_MODEL), row),
                  pl.BlockSpec((1, W_DIL), fix), pl.BlockSpec((1, W_SWA), fix),
                  pl.BlockSpec((D_MODEL, D_MODEL), fix), pl.BlockSpec((1, D_MODEL), fix),
                  pl.BlockSpec((1, D_MODEL), fix),
                  pl.BlockSpec((N_EXPERTS, D_MODEL), fix), pl.BlockSpec((N_EXPERTS, 1), fix)],
        out_specs=[pl.BlockSpec((tm * ROW_TILES, LANES), row), pl.BlockSpec((2 * TOP_K, tm), col),
                   pl.BlockSpec((2 * TOP_K, tm), col), pl.BlockSpec((N_EXPERTS, LANES), fix)],
        out_shape=[jax.ShapeDtypeStruct((T * ROW_TILES, LANES), _f32),
                   jax.ShapeDtypeStruct((2 * TOP_K, T), jnp.int32),
                   jax.ShapeDtypeStruct((2 * TOP_K, T), _f32),
                   jax.ShapeDtypeStruct((N_EXPERTS, LANES), _f32)],
        scratch_shapes=[pltpu.VMEM((N_EXPERTS, 1), _f32)],
        compiler_params=pltpu.CompilerParams(dimension_semantics=("arbitrary",),
                                             vmem_limit_bytes=VMEM_LIMIT),
        name="outproj_router",
    )(od, osw, x2, gd, gs, w, b, g2, wr_t, br)


def _dispatch_kernel(d_ref, x1_ref, xs_hbm, sem):
    td = DISPATCH_TILE

    def start_group(g, c):
        for rr in range(DMA_GROUP):
            r = g * DMA_GROUP + rr
            src = _token_row(x1_ref, r)
            for k in range(TOP_K):
                pltpu.make_async_copy(src, _token_row(xs_hbm, d_ref[0, k * td + r]), sem).start(priority=k % 2)
        return c

    lax.fori_loop(0, td // DMA_GROUP, start_group, 0)

    def wait_group(g, c):
        for _ in range(DMA_GROUP * TOP_K):
            pltpu.make_async_copy(_token_row(x1_ref, 0), _token_row(xs_hbm, 0), sem).wait()
        return c

    lax.fori_loop(0, td // DMA_GROUP, wait_group, 0)


def _dispatch_call(dest_tiles, x1r, n_slots):
    nt = dest_tiles.shape[0]
    return pl.pallas_call(
        _dispatch_kernel,
        grid=(nt,),
        in_specs=[pl.BlockSpec((None, 1, TOP_K * DISPATCH_TILE), lambda i: (i, 0, 0), memory_space=pltpu.SMEM),
                  pl.BlockSpec((DISPATCH_TILE * ROW_TILES, LANES), lambda i: (i, 0))],
        out_specs=pl.BlockSpec(memory_space=pl.ANY),
        out_shape=jax.ShapeDtypeStruct((n_slots * ROW_TILES, LANES), _f32),
        scratch_shapes=[pltpu.SemaphoreType.DMA(())],
        compiler_params=pltpu.CompilerParams(dimension_semantics=("arbitrary",),
                                             vmem_limit_bytes=VMEM_LIMIT),
        name="dispatch_rows",
    )(dest_tiles, x1r)


def _moe_kernel(be_ref, nv_ref, src_ref, grp_ref, nxt_ref, xs_ref, g2_ref, w1_hbm, b1_ref, w2_hbm, b2_ref,
                ys_ref, st1, st2, w1p, w2b, sem):
    i = pl.program_id(0)
    nv = nv_ref[i]
    tm = EXPERT_TILE

    def weight_copies(e, slot):
        return (pltpu.make_async_copy(w1_hbm.at[e], st1.at[slot], sem.at[0, slot]),
                pltpu.make_async_copy(w2_hbm.at[e], st2.at[slot], sem.at[1, slot]))

    @pl.when(nv > 0)
    def _():
        first_of_expert = jnp.logical_or(i == 0, be_ref[i] != be_ref[jnp.maximum(i - 1, 0)])

        @pl.when(first_of_expert)
        def _():
            slot = grp_ref[i] % 2

            @pl.when(i == 0)
            def _():
                for c in weight_copies(be_ref[0], 0):
                    c.start()

            for c in weight_copies(be_ref[i], slot):
                c.wait()

            @pl.when(nxt_ref[i] >= 0)
            def _():
                for c in weight_copies(nxt_ref[i], 1 - slot):
                    c.start()

            src = lax.broadcasted_iota(jnp.int32, (GLU_GROUP, GLU_GROUP), 0)
            dst = lax.broadcasted_iota(jnp.int32, (GLU_GROUP, GLU_GROUP), 1)
            half = GLU_GROUP // 2
            perm = (src == jnp.where(dst < half, 2 * dst, 2 * (dst - half) + 1)).astype(_bf16)
            for g in range(2 * D_FF // GLU_GROUP):
                cols = slice(g * GLU_GROUP, (g + 1) * GLU_GROUP)
                w1p[:, cols] = jnp.dot(st1[slot, :, cols].astype(_bf16), perm,
                                       preferred_element_type=_f32).astype(_bf16)
            w2b[...] = st2[slot].astype(_bf16)

        hm = tm // MOE_CHAINS

        def sub_block(s):
            rows = lambda c: pl.ds(s * hm * ROW_TILES + c, hm, stride=ROW_TILES)
            x = jnp.concatenate([xs_ref[rows(c), :] for c in range(ROW_TILES)], axis=1)
            x = jnp.where(lax.broadcasted_iota(jnp.int32, (hm, 1), 0) < nv - s * hm, x, 0.0)
            h = _rms(x, g2_ref[...]).astype(_bf16)
            u = jnp.dot(h, w1p[...], preferred_element_type=_f32) + b1_ref[0]
            half = GLU_GROUP // 2
            hid = []
            for g in range(2 * D_FF // GLU_GROUP):
                glu = jnp.minimum(u[:, g * GLU_GROUP:g * GLU_GROUP + half], SWIGLU_LIMIT)
                lin = jnp.clip(u[:, g * GLU_GROUP + half:(g + 1) * GLU_GROUP], -SWIGLU_LIMIT, SWIGLU_LIMIT)
                hid.append(glu * (1.0 / (1.0 + jnp.exp(-SWIGLU_ALPHA * glu))) * (lin + 1.0))
            hid = jnp.concatenate(hid, axis=1).astype(_bf16)
            y = jnp.dot(hid, w2b[...], preferred_element_type=_f32) + b2_ref[0]
            for c in range(ROW_TILES):
                ys_ref[rows(c), :] = y[:, c * LANES:(c + 1) * LANES]

        for n_sub in range(1, MOE_CHAINS + 1):
            @pl.when(jnp.logical_and(nv > (n_sub - 1) * hm, nv <= n_sub * hm))
            def _(n_sub=n_sub):
                for s in range(n_sub):
                    sub_block(s)


def _moe_call(block_e, n_valid, block_src, block_grp, next_e, xs, g2, w1, b1p, w2, b2):
    nblk = block_e.shape[0]
    tm = EXPERT_TILE
    by_e = lambda i, be, nv, src, grp, nxt: (be[i], 0, 0)
    by_src = lambda i, be, nv, src, grp, nxt: (src[i], 0)
    fix = lambda i, be, nv, src, grp, nxt: (0, 0)
    grid_spec = pltpu.PrefetchScalarGridSpec(
        num_scalar_prefetch=5,
        grid=(nblk,),
        in_specs=[pl.BlockSpec((tm * ROW_TILES, LANES), by_src), pl.BlockSpec((1, D_MODEL), fix),
                  pl.BlockSpec(memory_space=pl.ANY), pl.BlockSpec((1, 1, 2 * D_FF), by_e),
                  pl.BlockSpec(memory_space=pl.ANY), pl.BlockSpec((1, 1, D_MODEL), by_e)],
        out_specs=pl.BlockSpec((tm * ROW_TILES, LANES), by_src),
        scratch_shapes=[pltpu.VMEM((2, D_MODEL, 2 * D_FF), _f32), pltpu.VMEM((2, D_FF, D_MODEL), _f32),
                        pltpu.VMEM((D_MODEL, 2 * D_FF), _bf16), pltpu.VMEM((D_FF, D_MODEL), _bf16),
                        pltpu.SemaphoreType.DMA((2, 2))],
    )
    return pl.pallas_call(
        _moe_kernel,
        grid_spec=grid_spec,
        out_shape=jax.ShapeDtypeStruct(xs.shape, _f32),
        compiler_params=pltpu.CompilerParams(dimension_semantics=("arbitrary",),
                                             vmem_limit_bytes=VMEM_LIMIT),
        name="moe_experts",
    )(block_e, n_valid, block_src, block_grp, next_e, xs, g2, w1, b1p, w2, b2)


def _combine_kernel(dcur_ref, dnext_ref, x1_ref, ys_hbm, g_ref, gf_ref, o_ref, gbuf, sem):
    tc = COMBINE_TILE
    i = pl.program_id(0)
    slot = i % 2

    def row_copy(d_ref, s, r, k):
        dst = gbuf.at[s, k, pl.ds(r * ROW_TILES, ROW_TILES)]
        return pltpu.make_async_copy(_token_row(ys_hbm, d_ref[0, k * tc + r]), dst, sem.at[s])

    def issue(d_ref, s):
        def group(g, c):
            for rr in range(DMA_GROUP):
                for k in range(TOP_K):
                    row_copy(d_ref, s, g * DMA_GROUP + rr, k).start(priority=k % 2)
            return c
        lax.fori_loop(0, tc // DMA_GROUP, group, 0)

    @pl.when(i == 0)
    def _():
        issue(dcur_ref, 0)

    @pl.when(i + 1 < pl.num_programs(0))
    def _():
        issue(dnext_ref, 1 - slot)

    def wait_group(g, c):
        for _ in range(DMA_GROUP * TOP_K):
            row_copy(dcur_ref, slot, 0, 0).wait()
        return c

    lax.fori_loop(0, tc // DMA_GROUP, wait_group, 0)

    g = g_ref[...]
    acc = _load_token_rows(x1_ref, tc)
    for k in range(TOP_K):
        acc = acc + g[:, k:k + 1] * _load_token_rows(gbuf.at[slot, k], tc)
    o_ref[...] = _rms(acc, gf_ref[...])


def _combine_call(dest_tiles, x1r, ys, gates_t, gf):
    nt = dest_tiles.shape[0]
    tc = COMBINE_TILE
    row = lambda i: (i, 0)
    return pl.pallas_call(
        _combine_kernel,
        grid=(nt,),
        in_specs=[pl.BlockSpec((None, 1, TOP_K * tc), lambda i: (i, 0, 0), memory_space=pltpu.SMEM),
                  pl.BlockSpec((None, 1, TOP_K * tc), lambda i: (jnp.minimum(i + 1, nt - 1), 0, 0),
                               memory_space=pltpu.SMEM),
                  pl.BlockSpec((tc * ROW_TILES, LANES), row), pl.BlockSpec(memory_space=pl.ANY),
                  pl.BlockSpec((tc, TOP_K), row), pl.BlockSpec((1, D_MODEL), lambda i: (0, 0))],
        out_specs=pl.BlockSpec((tc, D_MODEL), row),
        out_shape=jax.ShapeDtypeStruct((nt * tc, D_MODEL), _f32),
        scratch_shapes=[pltpu.VMEM((2, TOP_K, tc * ROW_TILES, LANES), _f32), pltpu.SemaphoreType.DMA((2,))],
        compiler_params=pltpu.CompilerParams(dimension_semantics=("arbitrary",),
                                             vmem_limit_bytes=VMEM_LIMIT),
        name="combine_norm",
    )(dest_tiles, dest_tiles, x1r, ys, gates_t, gf)


def _plan_blocks(route_i, counts, n_tokens):
    tm = EXPERT_TILE
    nblk = (n_tokens * TOP_K + N_EXPERTS * tm) // tm
    counts = counts.astype(jnp.int32)
    blocks_e = (counts + tm - 1) // tm
    blk_end = jnp.cumsum(blocks_e)
    pad_start = (blk_end - blocks_e) * tm
    experts = jnp.arange(N_EXPERTS, dtype=jnp.int32)
    hot = route_i[:TOP_K, :, None] == experts
    dest = jnp.sum(jnp.where(hot, pad_start, 0), axis=-1) + route_i[TOP_K:]
    n_used = blk_end[-1]
    blk = jnp.arange(nblk, dtype=jnp.int32)
    src = jnp.minimum(blk, jnp.maximum(n_used - 1, 0))
    be = jnp.minimum(jnp.sum((blk_end[None, :] <= src[:, None]).astype(jnp.int32), axis=1), N_EXPERTS - 1)
    hot_b = be[:, None] == experts
    pick = lambda table: jnp.sum(jnp.where(hot_b, table, 0), axis=1)
    n_valid = jnp.where(blk < n_used, jnp.clip(pick(counts) - (src * tm - pick(pad_start)), 0, tm), 0)
    used = blocks_e > 0
    grp_e = jnp.cumsum(used.astype(jnp.int32)) - 1
    later = jnp.where((experts[None, :] > experts[:, None]) & used[None, :], experts[None, :], N_EXPERTS)
    nxt_e = jnp.min(later, axis=1)
    nxt_e = jnp.where(nxt_e < N_EXPERTS, nxt_e, -1)
    return be, n_valid.astype(jnp.int32), src, pick(grp_e), pick(nxt_e), dest


def _tile_dest(dest, tile):
    k, t = dest.shape
    return dest.reshape(k, t // tile, tile).transpose(1, 0, 2).reshape(t // tile, 1, k * tile)


def kernel(x, positions, norm1_g, w_in, b_in, sinks, gn_dil, gn_swa, w_out, b_out,
           norm2_g, w_router, b_router, w_mlp1, b_mlp1, w_mlp2, b_mlp2, norm_f_g):
    batch, seq, d = x.shape
    T = batch * seq
    x2 = x.reshape(T, d)
    depth = norm1_g.shape[0]
    assert depth == 1, "the final norm is fused into the last layer's combine kernel"
    inv = ROPE_THETA ** (-jnp.arange(0, HEAD_DIM, 2, dtype=_f32) / HEAD_DIM)
    inv = jnp.tile(inv, LANES // (HEAD_DIM // 2)).reshape(1, LANES)
    pos = positions.reshape(T, 1)
    for layer in range(depth):
        qa, ka, va, qb, kb, vb = _inproj_call(
            x2, pos, inv, norm1_g[layer].reshape(1, d), w_in[layer].astype(_bf16), b_in[layer].reshape(1, D_IN))
        o_dil = _attn_call(qa, ka, va, None, batch=batch, seq=seq,
                           configs=tuple((w // r, r) for w, r in DILATED_CONFIGS), kv_group=1)
        sink_lanes = jnp.repeat(sinks[layer].astype(_f32), HEAD_DIM).reshape(1, W_SWA)
        o_swa = _attn_call(qb, kb, vb, sink_lanes, batch=batch, seq=seq,
                           configs=((SWA_WINDOW - 1, 1),), kv_group=N_HEADS_SWA // N_KV_SWA // PAIR)
        x1r, route_i, route_g, counts = _outproj_call(
            o_dil, o_swa, x2, gn_dil[layer].reshape(1, W_DIL), gn_swa[layer].reshape(1, W_SWA),
            w_out[layer].astype(_bf16), b_out[layer].reshape(1, d), norm2_g[layer].reshape(1, d),
            w_router[layer].T, b_router[layer].reshape(N_EXPERTS, 1))
        block_e, n_valid, block_src, block_grp, next_e, dest = _plan_blocks(route_i, counts[:, 0], T)
        xs = _dispatch_call(_tile_dest(dest, DISPATCH_TILE), x1r, block_e.shape[0] * EXPERT_TILE)
        half = GLU_GROUP // 2
        b1p = b_mlp1[layer].reshape(N_EXPERTS, 2 * D_FF // GLU_GROUP, half, 2)
        b1p = jnp.swapaxes(b1p, 2, 3).reshape(N_EXPERTS, 1, 2 * D_FF)
        ys = _moe_call(block_e, n_valid, block_src, block_grp, next_e, xs, norm2_g[layer].reshape(1, d),
                       w_mlp1[layer], b1p, w_mlp2[layer], b_mlp2[layer].reshape(N_EXPERTS, 1, d))
        x2 = _combine_call(_tile_dest(dest, COMBINE_TILE), x1r, ys, route_g[:TOP_K].T, norm_f_g.reshape(1, d))
    return x2.reshape(batch, seq, d)
```

```python
import functools

import jax
import jax.numpy as jnp
from jax import lax
from jax.experimental import pallas as pl
from jax.experimental.pallas import tpu as pltpu

D_MODEL = 1024
HEAD_DIM = 64
N_HEADS_DIL = 8
DILATED_CONFIGS = ((128, 1), (512, 4), (2048, 16))
N_HEADS_SWA = 8
N_KV_SWA = 2
SWA_WINDOW = 128
ATTN_BLOCK = 128
ROPE_THETA = 10000.0
N_EXPERTS = 32
TOP_K = 4
D_FF = D_MODEL
SWIGLU_ALPHA = 1.702
SWIGLU_LIMIT = 7.0
NORM_EPS = 1e-5
LOG2E = 1.4426950408889634

W_DIL = N_HEADS_DIL * HEAD_DIM
W_SWA = N_HEADS_SWA * HEAD_DIM
W_KV_SWA = N_KV_SWA * HEAD_DIM
D_IN = 3 * W_DIL + W_SWA + 2 * W_KV_SWA

LANES = 128
PAIR = LANES // HEAD_DIM
SUBLANES = 8
ROW_TILES = D_MODEL // LANES
TOKEN_TILE = 512
EXPERT_TILE = 1024
DISPATCH_TILE = 2048
COMBINE_TILE = 256
DMA_GROUP = 8
ATTN_UNROLL = 32
MOE_CHAINS = 4
INPROJ_TILE = 1024
INPROJ_CHAINS = 4
OUTPROJ_CHAINS = 2
GLU_GROUP = 256
VMEM_LIMIT = 56 * 1024 * 1024

_f32 = jnp.float32
_bf16 = jnp.bfloat16


def _rms(x, g):
    return x * lax.rsqrt(jnp.mean(x * x, axis=-1, keepdims=True) + NORM_EPS) * g


def _store_token_rows(ref, x):
    tm = x.shape[0]
    for c in range(ROW_TILES):
        ref[pl.ds(c, tm, stride=ROW_TILES), :] = x[:, c * LANES:(c + 1) * LANES]


def _load_token_rows(ref, tm):
    return jnp.concatenate([ref[pl.ds(c, tm, stride=ROW_TILES), :] for c in range(ROW_TILES)], axis=1)


def _token_row(ref, tok):
    return ref.at[pl.ds(pl.multiple_of(tok * ROW_TILES, ROW_TILES), ROW_TILES)]


def _inproj_kernel(x_ref, pos_ref, inv_ref, g_ref, w_ref, b_ref, *out_refs):
    sub = x_ref.shape[0] // INPROJ_CHAINS
    for s in range(INPROJ_CHAINS):
        rows = slice(s * sub, (s + 1) * sub)
        _inproj_rows(x_ref[rows, :], pos_ref[rows, :], inv_ref, g_ref, w_ref, b_ref,
                     [r.at[rows, :] for r in out_refs])


def _inproj_rows(x, pos, inv_ref, g_ref, w_ref, b_ref, out_refs):
    qa_ref, ka_ref, va_ref, qb_ref, kb_ref, vb_ref = out_refs
    h = _rms(x, g_ref[...]).astype(_bf16)
    proj = jnp.dot(h, w_ref[...], preferred_element_type=_f32) + b_ref[...]

    ang = pos.astype(_f32) * inv_ref[...]
    lane = lax.broadcasted_iota(jnp.int32, ang.shape, 1)
    first_half = (lane % HEAD_DIM) < (HEAD_DIM // 2)
    cos = jnp.cos(ang)
    sin = jnp.where(first_half, -jnp.sin(ang), jnp.sin(ang))

    def rope(t, scale):
        outs = []
        for c in range(t.shape[1] // LANES):
            tc = t[:, c * LANES:(c + 1) * LANES]
            partner = jnp.where(first_half, pltpu.roll(tc, LANES - HEAD_DIM // 2, 1),
                                pltpu.roll(tc, HEAD_DIM // 2, 1))
            outs.append((tc * cos + partner * sin) * scale)
        return jnp.concatenate(outs, axis=1) if len(outs) > 1 else outs[0]

    scale = HEAD_DIM ** -0.5 * LOG2E
    o = 0
    qa_ref[...] = rope(proj[:, o:o + W_DIL], scale).astype(_bf16); o += W_DIL
    ka_ref[...] = rope(proj[:, o:o + W_DIL], 1.0).astype(_bf16); o += W_DIL
    va_ref[...] = proj[:, o:o + W_DIL].astype(_bf16); o += W_DIL
    qb_ref[...] = rope(proj[:, o:o + W_SWA], scale).astype(_bf16); o += W_SWA
    kb = rope(proj[:, o:o + W_KV_SWA], 1.0); o += W_KV_SWA
    vb = proj[:, o:o + W_KV_SWA]

    def dup_heads(t):
        sw = pltpu.roll(t, HEAD_DIM, 1)
        lo = lane < HEAD_DIM
        return jnp.concatenate([jnp.where(lo, t, sw), jnp.where(lo, sw, t)], axis=1)

    kb_ref[...] = dup_heads(kb).astype(_bf16)
    vb_ref[...] = dup_heads(vb).astype(_bf16)


def _inproj_call(x2, pos, inv, g, w, b):
    T = x2.shape[0]
    tm = INPROJ_TILE
    row = lambda i: (i, 0)
    fix = lambda i: (0, 0)
    widths = (W_DIL, W_DIL, W_DIL, W_SWA, PAIR * W_KV_SWA, PAIR * W_KV_SWA)
    return pl.pallas_call(
        _inproj_kernel,
        grid=(T // tm,),
        in_specs=[pl.BlockSpec((tm, D_MODEL), row), pl.BlockSpec((tm, 1), row),
                  pl.BlockSpec((1, LANES), fix), pl.BlockSpec((1, D_MODEL), fix),
                  pl.BlockSpec((D_MODEL, D_IN), fix), pl.BlockSpec((1, D_IN), fix)],
        out_specs=[pl.BlockSpec((tm, wd), row) for wd in widths],
        out_shape=[jax.ShapeDtypeStruct((T, wd), _bf16) for wd in widths],
        compiler_params=pltpu.CompilerParams(dimension_semantics=("parallel",),
                                             vmem_limit_bytes=VMEM_LIMIT),
        name="inproj",
    )(x2, pos, inv, g, w, b)


def _attn_kernel(*refs, configs, seq, with_sink):
    if with_sink:
        q_ref, k_ref, v_ref, sink_ref, o_ref = refs[:5]
        scratch = refs[5:]
    else:
        q_ref, k_ref, v_ref, o_ref = refs[:4]
        scratch = refs[4:]
    n_cfg = len(configs)
    obufs = scratch[:n_cfg]
    mbufs = scratch[n_cfg:2 * n_cfg]
    lbufs = scratch[2 * n_cfg:3 * n_cfg]
    extra = list(scratch[3 * n_cfg:])
    blk = ATTN_BLOCK
    dilated = [r for _, r in configs if r > 1]
    if dilated:
        qf, kf, vf = extra[:3]
        qf[...] = q_ref[...].astype(_f32)
        kf[...] = k_ref[...].astype(_f32)
        vf[...] = v_ref[...].astype(_f32)
        extra = extra[3:]

    lane = lax.broadcasted_iota(jnp.int32, (blk, LANES), 1)
    head0 = lane < HEAD_DIM
    lane_row = lax.broadcasted_iota(jnp.int32, (1, LANES), 1)
    head_rows = [(lane_row < HEAD_DIM).astype(_bf16), (lane_row >= HEAD_DIM).astype(_bf16)]
    qi = lax.broadcasted_iota(jnp.int32, (blk, 2 * blk), 0)
    kj = lax.broadcasted_iota(jnp.int32, (blk, 2 * blk), 1)
    dist = qi - kj + blk

    base_perm = None
    for ci, (max_dist, r) in enumerate(configs):
        nb = seq // r // blk
        band = (dist >= 0) & (dist <= max_dist)
        obuf, mbuf, lbuf = obufs[ci], mbufs[ci], lbufs[ci]
        if r == 1:
            qp, kp, vp = q_ref, k_ref, v_ref
        else:
            perm_bf16 = extra[:3]
            extra = extra[3:]
            later = [r2 for _, r2 in configs[ci + 1:] if r2 % r == 0]
            perm_f32 = []
            if later:
                perm_f32 = extra[:3]
                extra = extra[3:]
            r_src, srcs = 1, (qf, kf, vf)
            if base_perm is not None and r % base_perm[0] == 0:
                r_src, srcs = base_perm
            if perm_f32:
                base_perm = (r, perm_f32)
            qp, kp, vp = perm_bf16
            piece = 2 * blk
            step = r // r_src
            per_res = seq // r // piece
            len_src = seq // r_src

            def permute(idx, carry, r_src=r_src, srcs=srcs, step=step, per_res=per_res, len_src=len_src,
                        perm_bf16=perm_bf16, perm_f32=perm_f32):
                c = idx // per_res
                start = (c % r_src) * len_src + c // r_src + (step * piece) * (idx % per_res)
                src = pl.ds(start, piece, stride=step)
                dst = pl.ds(pl.multiple_of(idx * piece, piece), piece)
                for t in range(3):
                    val = srcs[t][src, :]
                    perm_bf16[t][dst, :] = val.astype(_bf16)
                    if perm_f32:
                        perm_f32[t][dst, :] = val
                return carry

            lax.fori_loop(0, seq // piece, permute, 0, unroll=4)

        def body(n, carry, r=r, nb=nb, band=band, obuf=obuf, lbuf=lbuf, qp=qp, kp=kp, vp=vp):
            j = n % nb
            cur = pl.ds(pl.multiple_of(n * blk, blk), blk)
            prev = pl.ds(pl.multiple_of(jnp.maximum(n - 1, 0) * blk, blk), blk)
            q = qp[cur, :]
            k2 = jnp.concatenate([kp[prev, :], kp[cur, :]], axis=0)
            v2 = jnp.concatenate([vp[prev, :], vp[cur, :]], axis=0)
            mask = band & (kj >= jnp.where(j > 0, 0, blk))
            q2 = jnp.concatenate([q * hm for hm in head_rows], axis=0)
            s2 = lax.dot_general(q2, k2, (((1,), (1,)), ((), ())), preferred_element_type=_f32)
            ps, maxs, sums = [], [], []
            for h in range(PAIR):
                s = jnp.where(mask, s2[h * blk:(h + 1) * blk], -jnp.inf)
                m = jnp.max(s, axis=-1, keepdims=True)
                p = jnp.exp2(s - m)
                ps.append(p.astype(_bf16))
                maxs.append(jnp.broadcast_to(m, (blk, LANES)))
                sums.append(jnp.broadcast_to(jnp.sum(p, axis=-1, keepdims=True), (blk, LANES)))
            o2 = jnp.dot(jnp.concatenate(ps, axis=0), v2, preferred_element_type=_f32)
            outs = [o2[:blk], o2[blk:]]
            start = n // nb + (r * blk) * j
            dst = pl.ds(pl.multiple_of(start, blk), blk) if r == 1 else pl.ds(start, blk, stride=r)
            obuf[dst, :] = jnp.where(head0, outs[0], outs[1])
            mbuf[dst, :] = jnp.where(head0, maxs[0], maxs[1])
            lbuf[dst, :] = jnp.where(head0, sums[0], sums[1])
            return carry

        lax.fori_loop(0, seq // blk, body, 0, unroll=ATTN_UNROLL)

    chunk = 4 * blk

    def finish(i, carry):
        sl = pl.ds(pl.multiple_of(i * chunk, chunk), chunk)
        ms = [mb[sl, :] for mb in mbufs]
        if with_sink:
            ms.append(jnp.broadcast_to(sink_ref[...] * LOG2E, ms[0].shape))
        mx = functools.reduce(jnp.maximum, ms)
        es = [jnp.exp2(m - mx) for m in ms]
        num = functools.reduce(lambda a, b: a + b, [e * ob[sl, :] for e, ob in zip(es, obufs)])
        den = functools.reduce(lambda a, b: a + b, [e * lb[sl, :] for e, lb in zip(es, lbufs)])
        if with_sink:
            den = den + es[-1]
        o_ref[sl, :] = num / den
        return carry

    lax.fori_loop(0, seq // chunk, finish, 0)


def _attn_call(q, k, v, sinks, *, batch, seq, configs, kv_group):
    n_pairs = q.shape[1] // LANES
    with_sink = sinks is not None
    qmap = lambda b, p: (b, p)
    kvmap = lambda b, p: (b, p // kv_group)
    in_specs = [pl.BlockSpec((seq, LANES), qmap), pl.BlockSpec((seq, LANES), kvmap),
                pl.BlockSpec((seq, LANES), kvmap)]
    args = [q, k, v]
    if with_sink:
        in_specs.append(pl.BlockSpec((1, LANES), lambda b, p: (0, p)))
        args.append(sinks)
    buf = lambda dt: pltpu.VMEM((seq, LANES), dt)
    scratch = [buf(_f32) for _ in range(3 * len(configs))]
    if any(r > 1 for _, r in configs):
        scratch += [buf(_f32) for _ in range(3)]
    for ci, (_, r) in enumerate(configs):
        if r > 1:
            scratch += [buf(_bf16) for _ in range(3)]
            if any(r2 % r == 0 for _, r2 in configs[ci + 1:]):
                scratch += [buf(_f32) for _ in range(3)]
    return pl.pallas_call(
        functools.partial(_attn_kernel, configs=configs, seq=seq, with_sink=with_sink),
        grid=(batch, n_pairs),
        in_specs=in_specs,
        out_specs=pl.BlockSpec((seq, LANES), qmap),
        out_shape=jax.ShapeDtypeStruct(q.shape, _f32),
        scratch_shapes=scratch,
        compiler_params=pltpu.CompilerParams(dimension_semantics=("parallel", "parallel"),
                                             vmem_limit_bytes=VMEM_LIMIT),
        name="attn_sink" if with_sink else "attn_dilated",
    )(*args)


def _outproj_kernel(od_ref, os_ref, x_ref, gd_ref, gs_ref, w_ref, b_ref, g2_ref, wr_ref, br_ref,
                    x1_ref, ri_ref, rg_ref, cnt_ref, carry_ref):
    tm = x_ref.shape[0]

    @pl.when(pl.program_id(0) == 0)
    def _():
        carry_ref[...] = jnp.zeros_like(carry_ref)

    sub = tm // OUTPROJ_CHAINS
    h2 = []
    for s in range(OUTPROJ_CHAINS):
        rows = slice(s * sub, (s + 1) * sub)
        mixed = jnp.concatenate([_rms(od_ref[rows, :], gd_ref[...]), _rms(os_ref[rows, :], gs_ref[...])], axis=1)
        x1 = x_ref[rows, :] + jnp.dot(mixed.astype(_bf16), w_ref[...], preferred_element_type=_f32) + b_ref[...]
        _store_token_rows(x1_ref.at[pl.ds(s * sub * ROW_TILES, sub * ROW_TILES)], x1)
        h2.append(_rms(x1, g2_ref[...]))
    h2 = jnp.concatenate(h2, axis=0)
    logits = lax.dot_general(wr_ref[...], h2, (((1,), (1,)), ((), ())),
                             precision=lax.Precision.HIGHEST, preferred_element_type=_f32) + br_ref[...]
    eiota = lax.broadcasted_iota(jnp.int32, logits.shape, 0).astype(_f32)
    vals, idxs, sels = [], [], []
    l = logits
    for _ in range(TOP_K):
        m = jnp.max(l, axis=0, keepdims=True)
        idx = jnp.min(jnp.where(l == m, eiota, float(N_EXPERTS)), axis=0, keepdims=True)
        sel = eiota == idx
        vals.append(m); idxs.append(idx); sels.append(sel)
        l = jnp.where(sel, -jnp.inf, l)
    exps = [jnp.exp(v - vals[0]) for v in vals]
    den = functools.reduce(lambda a, b: a + b, exps)
    gates = [e / den for e in exps]

    chosen = functools.reduce(jnp.logical_or, sels).astype(_f32)
    ti = lax.broadcasted_iota(jnp.int32, (tm, tm), 0)
    tj = lax.broadcasted_iota(jnp.int32, (tm, tm), 1)
    before = (ti < tj).astype(_bf16)
    prefix = jnp.dot(chosen.astype(_bf16), before, preferred_element_type=_f32) + carry_ref[...]
    ranks = [jnp.sum(jnp.where(s, prefix, 0.0), axis=0, keepdims=True).astype(jnp.int32) for s in sels]
    carry_ref[...] = carry_ref[...] + jnp.sum(chosen, axis=1, keepdims=True)

    ri_ref[...] = jnp.concatenate([ix.astype(jnp.int32) for ix in idxs] + ranks, axis=0)
    rg_ref[...] = jnp.concatenate(gates + gates, axis=0)
    cnt_ref[...] = jnp.broadcast_to(carry_ref[...], cnt_ref.shape)


def _outproj_call(od, osw, x2, gd, gs, w, b, g2, wr_t, br):
    T = x2.shape[0]
    tm = TOKEN_TILE
    row = lambda i: (i, 0)
    col = lambda i: (0, i)
    fix = lambda i: (0, 0)
    return pl.pallas_call(
        _outproj_kernel,
        grid=(T // tm,),
        in_specs=[pl.BlockSpec((tm, W_DIL), row), pl.BlockSpec((tm, W_SWA), row),
                  pl.BlockSpec((tm, D_MODEL), row),
                  pl.BlockSpec((1, W_DIL), fix), pl.BlockSpec((1, W_SWA), fix),
                  pl.BlockSpec((D_MODEL, D_MODEL), fix), pl.BlockSpec((1, D_MODEL), fix),
                  pl.BlockSpec((1, D_MODEL), fix),
                  pl.BlockSpec((N_EXPERTS, D_MODEL), fix), pl.BlockSpec((N_EXPERTS, 1), fix)],
        out_specs=[pl.BlockSpec((tm * ROW_TILES, LANES), row), pl.BlockSpec((2 * TOP_K, tm), col),
                   pl.BlockSpec((2 * TOP_K, tm), col), pl.BlockSpec((N_EXPERTS, LANES), fix)],
        out_shape=[jax.ShapeDtypeStruct((T * ROW_TILES, LANES), _f32),
                   jax.ShapeDtypeStruct((2 * TOP_K, T), jnp.int32),
                   jax.ShapeDtypeStruct((2 * TOP_K, T), _f32),
                   jax.ShapeDtypeStruct((N_EXPERTS, LANES), _f32)],
        scratch_shapes=[pltpu.VMEM((N_EXPERTS, 1), _f32)],
        compiler_params=pltpu.CompilerParams(dimension_semantics=("arbitrary",),
                                             vmem_limit_bytes=VMEM_LIMIT),
        name="outproj_router",
    )(od, osw, x2, gd, gs, w, b, g2, wr_t, br)


def _dispatch_kernel(d_ref, x1_ref, xs_hbm, sem):
    td = DISPATCH_TILE

    def start_group(g, c):
        for rr in range(DMA_GROUP):
            r = g * DMA_GROUP + rr
            src = _token_row(x1_ref, r)
            for k in range(TOP_K):
                pltpu.make_async_copy(src, _token_row(xs_hbm, d_ref[0, k * td + r]), sem).start(priority=k % 2)
        return c

    lax.fori_loop(0, td // DMA_GROUP, start_group, 0)

    def wait_group(g, c):
        for _ in range(DMA_GROUP * TOP_K):
            pltpu.make_async_copy(_token_row(x1_ref, 0), _token_row(xs_hbm, 0), sem).wait()
        return c

    lax.fori_loop(0, td // DMA_GROUP, wait_group, 0)


def _dispatch_call(dest_tiles, x1r, n_slots):
    nt = dest_tiles.shape[0]
    return pl.pallas_call(
        _dispatch_kernel,
        grid=(nt,),
        in_specs=[pl.BlockSpec((None, 1, TOP_K * DISPATCH_TILE), lambda i: (i, 0, 0), memory_space=pltpu.SMEM),
                  pl.BlockSpec((DISPATCH_TILE * ROW_TILES, LANES), lambda i: (i, 0))],
        out_specs=pl.BlockSpec(memory_space=pl.ANY),
        out_shape=jax.ShapeDtypeStruct((n_slots * ROW_TILES, LANES), _f32),
        scratch_shapes=[pltpu.SemaphoreType.DMA(())],
        compiler_params=pltpu.CompilerParams(dimension_semantics=("arbitrary",),
                                             vmem_limit_bytes=VMEM_LIMIT),
        name="dispatch_rows",
    )(dest_tiles, x1r)


def _moe_kernel(be_ref, nv_ref, src_ref, grp_ref, nxt_ref, xs_ref, g2_ref, w1_hbm, b1_ref, w2_hbm, b2_ref,
                ys_ref, st1, st2, w1p, w2b, sem):
    i = pl.program_id(0)
    nv = nv_ref[i]
    tm = EXPERT_TILE

    def weight_copies(e, slot):
        return (pltpu.make_async_copy(w1_hbm.at[e], st1.at[slot], sem.at[0, slot]),
                pltpu.make_async_copy(w2_hbm.at[e], st2.at[slot], sem.at[1, slot]))

    @pl.when(nv > 0)
    def _():
        first_of_expert = jnp.logical_or(i == 0, be_ref[i] != be_ref[jnp.maximum(i - 1, 0)])

        @pl.when(first_of_expert)
        def _():
            slot = grp_ref[i] % 2

            @pl.when(i == 0)
            def _():
                for c in weight_copies(be_ref[0], 0):
                    c.start()

            for c in weight_copies(be_ref[i], slot):
                c.wait()

            @pl.when(nxt_ref[i] >= 0)
            def _():
                for c in weight_copies(nxt_ref[i], 1 - slot):
                    c.start()

            src = lax.broadcasted_iota(jnp.int32, (GLU_GROUP, GLU_GROUP), 0)
            dst = lax.broadcasted_iota(jnp.int32, (GLU_GROUP, GLU_GROUP), 1)
            half = GLU_GROUP // 2
            perm = (src == jnp.where(dst < half, 2 * dst, 2 * (dst - half) + 1)).astype(_bf16)
            for g in range(2 * D_FF // GLU_GROUP):
                cols = slice(g * GLU_GROUP, (g + 1) * GLU_GROUP)
                w1p[:, cols] = jnp.dot(st1[slot, :, cols].astype(_bf16), perm,
                                       preferred_element_type=_f32).astype(_bf16)
            w2b[...] = st2[slot].astype(_bf16)

        hm = tm // MOE_CHAINS

        def sub_block(s):
            rows = lambda c: pl.ds(s * hm * ROW_TILES + c, hm, stride=ROW_TILES)
            x = jnp.concatenate([xs_ref[rows(c), :] for c in range(ROW_TILES)], axis=1)
            x = jnp.where(lax.broadcasted_iota(jnp.int32, (hm, 1), 0) < nv - s * hm, x, 0.0)
            h = _rms(x, g2_ref[...]).astype(_bf16)
            u = jnp.dot(h, w1p[...], preferred_element_type=_f32) + b1_ref[0]
            half = GLU_GROUP // 2
            hid = []
            for g in range(2 * D_FF // GLU_GROUP):
                glu = jnp.minimum(u[:, g * GLU_GROUP:g * GLU_GROUP + half], SWIGLU_LIMIT)
                lin = jnp.clip(u[:, g * GLU_GROUP + half:(g + 1) * GLU_GROUP], -SWIGLU_LIMIT, SWIGLU_LIMIT)
                hid.append(glu * (1.0 / (1.0 + jnp.exp(-SWIGLU_ALPHA * glu))) * (lin + 1.0))
            hid = jnp.concatenate(hid, axis=1).astype(_bf16)
            y = jnp.dot(hid, w2b[...], preferred_element_type=_f32) + b2_ref[0]
            for c in range(ROW_TILES):
                ys_ref[rows(c), :] = y[:, c * LANES:(c + 1) * LANES]

        for n_sub in range(1, MOE_CHAINS + 1):
            @pl.when(jnp.logical_and(nv > (n_sub - 1) * hm, nv <= n_sub * hm))
            def _(n_sub=n_sub):
                for s in range(n_sub):
                    sub_block(s)


def _moe_call(block_e, n_valid, block_src, block_grp, next_e, xs, g2, w1, b1p, w2, b2):
    nblk = block_e.shape[0]
    tm = EXPERT_TILE
    by_e = lambda i, be, nv, src, grp, nxt: (be[i], 0, 0)
    by_src = lambda i, be, nv, src, grp, nxt: (src[i], 0)
    fix = lambda i, be, nv, src, grp, nxt: (0, 0)
    grid_spec = pltpu.PrefetchScalarGridSpec(
        num_scalar_prefetch=5,
        grid=(nblk,),
        in_specs=[pl.BlockSpec((tm * ROW_TILES, LANES), by_src), pl.BlockSpec((1, D_MODEL), fix),
                  pl.BlockSpec(memory_space=pl.ANY), pl.BlockSpec((1, 1, 2 * D_FF), by_e),
                  pl.BlockSpec(memory_space=pl.ANY), pl.BlockSpec((1, 1, D_MODEL), by_e)],
        out_specs=pl.BlockSpec((tm * ROW_TILES, LANES), by_src),
        scratch_shapes=[pltpu.VMEM((2, D_MODEL, 2 * D_FF), _f32), pltpu.VMEM((2, D_FF, D_MODEL), _f32),
                        pltpu.VMEM((D_MODEL, 2 * D_FF), _bf16), pltpu.VMEM((D_FF, D_MODEL), _bf16),
                        pltpu.SemaphoreType.DMA((2, 2))],
    )
    return pl.pallas_call(
        _moe_kernel,
        grid_spec=grid_spec,
        out_shape=jax.ShapeDtypeStruct(xs.shape, _f32),
        compiler_params=pltpu.CompilerParams(dimension_semantics=("arbitrary",),
                                             vmem_limit_bytes=VMEM_LIMIT),
        name="moe_experts",
    )(block_e, n_valid, block_src, block_grp, next_e, xs, g2, w1, b1p, w2, b2)


def _combine_kernel(dcur_ref, dnext_ref, x1_ref, ys_hbm, g_ref, gf_ref, o_ref, gbuf, sem):
    tc = COMBINE_TILE
    i = pl.program_id(0)
    slot = i % 2

    def row_copy(d_ref, s, r, k):
        dst = gbuf.at[s, k, pl.ds(r * ROW_TILES, ROW_TILES)]
        return pltpu.make_async_copy(_token_row(ys_hbm, d_ref[0, k * tc + r]), dst, sem.at[s])

    def issue(d_ref, s):
        def group(g, c):
            for rr in range(DMA_GROUP):
                for k in range(TOP_K):
                    row_copy(d_ref, s, g * DMA_GROUP + rr, k).start(priority=k % 2)
            return c
        lax.fori_loop(0, tc // DMA_GROUP, group, 0)

    @pl.when(i == 0)
    def _():
        issue(dcur_ref, 0)

    @pl.when(i + 1 < pl.num_programs(0))
    def _():
        issue(dnext_ref, 1 - slot)

    def wait_group(g, c):
        for _ in range(DMA_GROUP * TOP_K):
            row_copy(dcur_ref, slot, 0, 0).wait()
        return c

    lax.fori_loop(0, tc // DMA_GROUP, wait_group, 0)

    g = g_ref[...]
    acc = _load_token_rows(x1_ref, tc)
    for k in range(TOP_K):
        acc = acc + g[:, k:k + 1] * _load_token_rows(gbuf.at[slot, k], tc)
    o_ref[...] = _rms(acc, gf_ref[...])


def _combine_call(dest_tiles, x1r, ys, gates_t, gf):
    nt = dest_tiles.shape[0]
    tc = COMBINE_TILE
    row = lambda i: (i, 0)
    return pl.pallas_call(
        _combine_kernel,
        grid=(nt,),
        in_specs=[pl.BlockSpec((None, 1, TOP_K * tc), lambda i: (i, 0, 0), memory_space=pltpu.SMEM),
                  pl.BlockSpec((None, 1, TOP_K * tc), lambda i: (jnp.minimum(i + 1, nt - 1), 0, 0),
                               memory_space=pltpu.SMEM),
                  pl.BlockSpec((tc * ROW_TILES, LANES), row), pl.BlockSpec(memory_space=pl.ANY),
                  pl.BlockSpec((tc, TOP_K), row), pl.BlockSpec((1, D_MODEL), lambda i: (0, 0))],
        out_specs=pl.BlockSpec((tc, D_MODEL), row),
        out_shape=jax.ShapeDtypeStruct((nt * tc, D_MODEL), _f32),
        scratch_shapes=[pltpu.VMEM((2, TOP_K, tc * ROW_TILES, LANES), _f32), pltpu.SemaphoreType.DMA((2,))],
        compiler_params=pltpu.CompilerParams(dimension_semantics=("arbitrary",),
                                             vmem_limit_bytes=VMEM_LIMIT),
        name="combine_norm",
    )(dest_tiles, dest_tiles, x1r, ys, gates_t, gf)


def _plan_blocks(route_i, counts, n_tokens):
    tm = EXPERT_TILE
    nblk = (n_tokens * TOP_K + N_EXPERTS * tm) // tm
    counts = counts.astype(jnp.int32)
    blocks_e = (counts + tm - 1) // tm
    blk_end = jnp.cumsum(blocks_e)
    pad_start = (blk_end - blocks_e) * tm
    experts = jnp.arange(N_EXPERTS, dtype=jnp.int32)
    hot = route_i[:TOP_K, :, None] == experts
    dest = jnp.sum(jnp.where(hot, pad_start, 0), axis=-1) + route_i[TOP_K:]
    n_used = blk_end[-1]
    blk = jnp.arange(nblk, dtype=jnp.int32)
    src = jnp.minimum(blk, jnp.maximum(n_used - 1, 0))
    be = jnp.minimum(jnp.sum((blk_end[None, :] <= src[:, None]).astype(jnp.int32), axis=1), N_EXPERTS - 1)
    hot_b = be[:, None] == experts
    pick = lambda table: jnp.sum(jnp.where(hot_b, table, 0), axis=1)
    n_valid = jnp.where(blk < n_used, jnp.clip(pick(counts) - (src * tm - pick(pad_start)), 0, tm), 0)
    used = blocks_e > 0
    grp_e = jnp.cumsum(used.astype(jnp.int32)) - 1
    later = jnp.where((experts[None, :] > experts[:, None]) & used[None, :], experts[None, :], N_EXPERTS)
    nxt_e = jnp.min(later, axis=1)
    nxt_e = jnp.where(nxt_e < N_EXPERTS, nxt_e, -1)
    return be, n_valid.astype(jnp.int32), src, pick(grp_e), pick(nxt_e), dest


def _tile_dest(dest, tile):
    k, t = dest.shape
    return dest.reshape(k, t // tile, tile).transpose(1, 0, 2).reshape(t // tile, 1, k * tile)


def kernel(x, positions, norm1_g, w_in, b_in, sinks, gn_dil, gn_swa, w_out, b_out,
           norm2_g, w_router, b_router, w_mlp1, b_mlp1, w_mlp2, b_mlp2, norm_f_g):
    batch, seq, d = x.shape
    T = batch * seq
    x2 = x.reshape(T, d)
    depth = norm1_g.shape[0]
    assert depth == 1, "the final norm is fused into the last layer's combine kernel"
    inv = ROPE_THETA ** (-jnp.arange(0, HEAD_DIM, 2, dtype=_f32) / HEAD_DIM)
    inv = jnp.tile(inv, LANES // (HEAD_DIM // 2)).reshape(1, LANES)
    pos = positions.reshape(T, 1)
    for layer in range(depth):
        qa, ka, va, qb, kb, vb = _inproj_call(
            x2, pos, inv, norm1_g[layer].reshape(1, d), w_in[layer].astype(_bf16), b_in[layer].reshape(1, D_IN))
        o_dil = _attn_call(qa, ka, va, None, batch=batch, seq=seq,
                           configs=tuple((w // r, r) for w, r in DILATED_CONFIGS), kv_group=1)
        sink_lanes = jnp.repeat(sinks[layer].astype(_f32), HEAD_DIM).reshape(1, W_SWA)
        o_swa = _attn_call(qb, kb, vb, sink_lanes, batch=batch, seq=seq,
                           configs=((SWA_WINDOW - 1, 1),), kv_group=N_HEADS_SWA // N_KV_SWA // PAIR)
        x1r, route_i, route_g, counts = _outproj_call(
            o_dil, o_swa, x2, gn_dil[layer].reshape(1, W_DIL), gn_swa[layer].reshape(1, W_SWA),
            w_out[layer].astype(_bf16), b_out[layer].reshape(1, d), norm2_g[layer].reshape(1, d),
            w_router[layer].T, b_router[layer].reshape(N_EXPERTS, 1))
        block_e, n_valid, block_src, block_grp, next_e, dest = _plan_blocks(route_i, counts[:, 0], T)
        xs = _dispatch_call(_tile_dest(dest, DISPATCH_TILE), x1r, block_e.shape[0] * EXPERT_TILE)
        half = GLU_GROUP // 2
        b1p = b_mlp1[layer].reshape(N_EXPERTS, 2 * D_FF // GLU_GROUP, half, 2)
        b1p = jnp.swapaxes(b1p, 2, 3).reshape(N_EXPERTS, 1, 2 * D_FF)
        ys = _moe_call(block_e, n_valid, block_src, block_grp, next_e, xs, norm2_g[layer].reshape(1, d),
                       w_mlp1[layer], b1p, w_mlp2[layer], b_mlp2[layer].reshape(N_EXPERTS, 1, d))
        x2 = _combine_call(_tile_dest(dest, COMBINE_TILE), x1r, ys, route_g[:TOP_K].T, norm_f_g.reshape(1, d))
    return x2.reshape(batch, seq, d)
```

```python
import functools

import jax
import jax.numpy as jnp
from jax import lax
from jax.experimental import pallas as pl
from jax.experimental.pallas import tpu as pltpu

D_MODEL = 1024
HEAD_DIM = 64
N_HEADS_DIL = 8
DILATED_CONFIGS = ((128, 1), (512, 4), (2048, 16))
N_HEADS_SWA = 8
N_KV_SWA = 2
SWA_WINDOW = 128
ATTN_BLOCK = 128
ROPE_THETA = 10000.0
N_EXPERTS = 32
TOP_K = 4
D_FF = D_MODEL
SWIGLU_ALPHA = 1.702
SWIGLU_LIMIT = 7.0
NORM_EPS = 1e-5
LOG2E = 1.4426950408889634

W_DIL = N_HEADS_DIL * HEAD_DIM
W_SWA = N_HEADS_SWA * HEAD_DIM
W_KV_SWA = N_KV_SWA * HEAD_DIM
D_IN = 3 * W_DIL + W_SWA + 2 * W_KV_SWA

LANES = 128
PAIR = LANES // HEAD_DIM
SUBLANES = 8
ROW_TILES = D_MODEL // LANES
TOKEN_TILE = 512
EXPERT_TILE = 1024
DISPATCH_TILE = 2048
COMBINE_TILE = 256
DMA_GROUP = 8
ATTN_UNROLL = 32
MOE_CHAINS = 4
INPROJ_TILE = 1024
INPROJ_CHAINS = 4
OUTPROJ_CHAINS = 2
GLU_GROUP = 256
VMEM_LIMIT = 56 * 1024 * 1024

_f32 = jnp.float32
_bf16 = jnp.bfloat16


def _rms(x, g):
    return x * lax.rsqrt(jnp.mean(x * x, axis=-1, keepdims=True) + NORM_EPS) * g


def _store_token_rows(ref, x):
    tm = x.shape[0]
    for c in range(ROW_TILES):
        ref[pl.ds(c, tm, stride=ROW_TILES), :] = x[:, c * LANES:(c + 1) * LANES]


def _load_token_rows(ref, tm):
    return jnp.concatenate([ref[pl.ds(c, tm, stride=ROW_TILES), :] for c in range(ROW_TILES)], axis=1)


def _token_row(ref, tok):
    return ref.at[pl.ds(pl.multiple_of(tok * ROW_TILES, ROW_TILES), ROW_TILES)]


def _inproj_kernel(x_ref, pos_ref, inv_ref, g_ref, w_ref, b_ref, *out_refs):
    sub = x_ref.shape[0] // INPROJ_CHAINS
    for s in range(INPROJ_CHAINS):
        rows = slice(s * sub, (s + 1) * sub)
        _inproj_rows(x_ref[rows, :], pos_ref[rows, :], inv_ref, g_ref, w_ref, b_ref,
                     [r.at[rows, :] for r in out_refs])


def _inproj_rows(x, pos, inv_ref, g_ref, w_ref, b_ref, out_refs):
    qa_ref, ka_ref, va_ref, qb_ref, kb_ref, vb_ref = out_refs
    h = _rms(x, g_ref[...]).astype(_bf16)
    proj = jnp.dot(h, w_ref[...], preferred_element_type=_f32) + b_ref[...]

    ang = pos.astype(_f32) * inv_ref[...]
    lane = lax.broadcasted_iota(jnp.int32, ang.shape, 1)
    first_half = (lane % HEAD_DIM) < (HEAD_DIM // 2)
    cos = jnp.cos(ang)
    sin = jnp.where(first_half, -jnp.sin(ang), jnp.sin(ang))

    def rope(t, scale):
        outs = []
        for c in range(t.shape[1] // LANES):
            tc = t[:, c * LANES:(c + 1) * LANES]
            partner = jnp.where(first_half, pltpu.roll(tc, LANES - HEAD_DIM // 2, 1),
                                pltpu.roll(tc, HEAD_DIM // 2, 1))
            outs.append((tc * cos + partner * sin) * scale)
        return jnp.concatenate(outs, axis=1) if len(outs) > 1 else outs[0]

    scale = HEAD_DIM ** -0.5 * LOG2E
    o = 0
    qa_ref[...] = rope(proj[:, o:o + W_DIL], scale).astype(_bf16); o += W_DIL
    ka_ref[...] = rope(proj[:, o:o + W_DIL], 1.0).astype(_bf16); o += W_DIL
    va_ref[...] = proj[:, o:o + W_DIL].astype(_bf16); o += W_DIL
    qb_ref[...] = rope(proj[:, o:o + W_SWA], scale).astype(_bf16); o += W_SWA
    kb = rope(proj[:, o:o + W_KV_SWA], 1.0); o += W_KV_SWA
    vb = proj[:, o:o + W_KV_SWA]

    def dup_heads(t):
        sw = pltpu.roll(t, HEAD_DIM, 1)
        lo = lane < HEAD_DIM
        return jnp.concatenate([jnp.where(lo, t, sw), jnp.where(lo, sw, t)], axis=1)

    kb_ref[...] = dup_heads(kb).astype(_bf16)
    vb_ref[...] = dup_heads(vb).astype(_bf16)


def _inproj_call(x2, pos, inv, g, w, b):
    T = x2.shape[0]
    tm = INPROJ_TILE
    row = lambda i: (i, 0)
    fix = lambda i: (0, 0)
    widths = (W_DIL, W_DIL, W_DIL, W_SWA, PAIR * W_KV_SWA, PAIR * W_KV_SWA)
    return pl.pallas_call(
        _inproj_kernel,
        grid=(T // tm,),
        in_specs=[pl.BlockSpec((tm, D_MODEL), row), pl.BlockSpec((tm, 1), row),
                  pl.BlockSpec((1, LANES), fix), pl.BlockSpec((1, D_MODEL), fix),
                  pl.BlockSpec((D_MODEL, D_IN), fix), pl.BlockSpec((1, D_IN), fix)],
        out_specs=[pl.BlockSpec((tm, wd), row) for wd in widths],
        out_shape=[jax.ShapeDtypeStruct((T, wd), _bf16) for wd in widths],
        compiler_params=pltpu.CompilerParams(dimension_semantics=("parallel",),
                                             vmem_limit_bytes=VMEM_LIMIT),
        name="inproj",
    )(x2, pos, inv, g, w, b)


def _attn_kernel(*refs, configs, seq, with_sink):
    if with_sink:
        q_ref, k_ref, v_ref, sink_ref, o_ref = refs[:5]
        scratch = refs[5:]
    else:
        q_ref, k_ref, v_ref, o_ref = refs[:4]
        scratch = refs[4:]
    n_cfg = len(configs)
    obufs = scratch[:n_cfg]
    mbufs = scratch[n_cfg:2 * n_cfg]
    lbufs = scratch[2 * n_cfg:3 * n_cfg]
    extra = list(scratch[3 * n_cfg:])
    blk = ATTN_BLOCK
    dilated = [r for _, r in configs if r > 1]
    if dilated:
        qf, kf, vf = extra[:3]
        qf[...] = q_ref[...].astype(_f32)
        kf[...] = k_ref[...].astype(_f32)
        vf[...] = v_ref[...].astype(_f32)
        extra = extra[3:]

    lane = lax.broadcasted_iota(jnp.int32, (blk, LANES), 1)
    head0 = lane < HEAD_DIM
    lane_row = lax.broadcasted_iota(jnp.int32, (1, LANES), 1)
    head_rows = [(lane_row < HEAD_DIM).astype(_bf16), (lane_row >= HEAD_DIM).astype(_bf16)]
    qi = lax.broadcasted_iota(jnp.int32, (blk, 2 * blk), 0)
    kj = lax.broadcasted_iota(jnp.int32, (blk, 2 * blk), 1)
    dist = qi - kj + blk

    base_perm = None
    for ci, (max_dist, r) in enumerate(configs):
        nb = seq // r // blk
        band = (dist >= 0) & (dist <= max_dist)
        obuf, mbuf, lbuf = obufs[ci], mbufs[ci], lbufs[ci]
        if r == 1:
            qp, kp, vp = q_ref, k_ref, v_ref
        else:
            perm_bf16 = extra[:3]
            extra = extra[3:]
            later = [r2 for _, r2 in configs[ci + 1:] if r2 % r == 0]
            perm_f32 = []
            if later:
                perm_f32 = extra[:3]
                extra = extra[3:]
            r_src, srcs = 1, (qf, kf, vf)
            if base_perm is not None and r % base_perm[0] == 0:
                r_src, srcs = base_perm
            if perm_f32:
                base_perm = (r, perm_f32)
            qp, kp, vp = perm_bf16
            piece = 2 * blk
            step = r // r_src
            per_res = seq // r // piece
            len_src = seq // r_src

            def permute(idx, carry, r_src=r_src, srcs=srcs, step=step, per_res=per_res, len_src=len_src,
                        perm_bf16=perm_bf16, perm_f32=perm_f32):
                c = idx // per_res
                start = (c % r_src) * len_src + c // r_src + (step * piece) * (idx % per_res)
                src = pl.ds(start, piece, stride=step)
                dst = pl.ds(pl.multiple_of(idx * piece, piece), piece)
                for t in range(3):
                    val = srcs[t][src, :]
                    perm_bf16[t][dst, :] = val.astype(_bf16)
                    if perm_f32:
                        perm_f32[t][dst, :] = val
                return carry

            lax.fori_loop(0, seq // piece, permute, 0, unroll=4)

        def body(n, carry, r=r, nb=nb, band=band, obuf=obuf, lbuf=lbuf, qp=qp, kp=kp, vp=vp):
            j = n % nb
            cur = pl.ds(pl.multiple_of(n * blk, blk), blk)
            prev = pl.ds(pl.multiple_of(jnp.maximum(n - 1, 0) * blk, blk), blk)
            q = qp[cur, :]
            k2 = jnp.concatenate([kp[prev, :], kp[cur, :]], axis=0)
            v2 = jnp.concatenate([vp[prev, :], vp[cur, :]], axis=0)
            mask = band & (kj >= jnp.where(j > 0, 0, blk))
            q2 = jnp.concatenate([q * hm for hm in head_rows], axis=0)
            s2 = lax.dot_general(q2, k2, (((1,), (1,)), ((), ())), preferred_element_type=_f32)
            ps, maxs, sums = [], [], []
            for h in range(PAIR):
                s = jnp.where(mask, s2[h * blk:(h + 1) * blk], -jnp.inf)
                m = jnp.max(s, axis=-1, keepdims=True)
                p = jnp.exp2(s - m)
                ps.append(p.astype(_bf16))
                maxs.append(jnp.broadcast_to(m, (blk, LANES)))
                sums.append(jnp.broadcast_to(jnp.sum(p, axis=-1, keepdims=True), (blk, LANES)))
            o2 = jnp.dot(jnp.concatenate(ps, axis=0), v2, preferred_element_type=_f32)
            outs = [o2[:blk], o2[blk:]]
            start = n // nb + (r * blk) * j
            dst = pl.ds(pl.multiple_of(start, blk), blk) if r == 1 else pl.ds(start, blk, stride=r)
            obuf[dst, :] = jnp.where(head0, outs[0], outs[1])
            mbuf[dst, :] = jnp.where(head0, maxs[0], maxs[1])
            lbuf[dst, :] = jnp.where(head0, sums[0], sums[1])
            return carry

        lax.fori_loop(0, seq // blk, body, 0, unroll=ATTN_UNROLL)

    chunk = 4 * blk

    def finish(i, carry):
        sl = pl.ds(pl.multiple_of(i * chunk, chunk), chunk)
        ms = [mb[sl, :] for mb in mbufs]
        if with_sink:
            ms.append(jnp.broadcast_to(sink_ref[...] * LOG2E, ms[0].shape))
        mx = functools.reduce(jnp.maximum, ms)
        es = [jnp.exp2(m - mx) for m in ms]
        num = functools.reduce(lambda a, b: a + b, [e * ob[sl, :] for e, ob in zip(es, obufs)])
        den = functools.reduce(lambda a, b: a + b, [e * lb[sl, :] for e, lb in zip(es, lbufs)])
        if with_sink:
            den = den + es[-1]
        o_ref[sl, :] = num / den
        return carry

    lax.fori_loop(0, seq // chunk, finish, 0)


def _attn_call(q, k, v, sinks, *, batch, seq, configs, kv_group):
    n_pairs = q.shape[1] // LANES
    with_sink = sinks is not None
    qmap = lambda b, p: (b, p)
    kvmap = lambda b, p: (b, p // kv_group)
    in_specs = [pl.BlockSpec((seq, LANES), qmap), pl.BlockSpec((seq, LANES), kvmap),
                pl.BlockSpec((seq, LANES), kvmap)]
    args = [q, k, v]
    if with_sink:
        in_specs.append(pl.BlockSpec((1, LANES), lambda b, p: (0, p)))
        args.append(sinks)
    buf = lambda dt: pltpu.VMEM((seq, LANES), dt)
    scratch = [buf(_f32) for _ in range(3 * len(configs))]
    if any(r > 1 for _, r in configs):
        scratch += [buf(_f32) for _ in range(3)]
    for ci, (_, r) in enumerate(configs):
        if r > 1:
            scratch += [buf(_bf16) for _ in range(3)]
            if any(r2 % r == 0 for _, r2 in configs[ci + 1:]):
                scratch += [buf(_f32) for _ in range(3)]
    return pl.pallas_call(
        functools.partial(_attn_kernel, configs=configs, seq=seq, with_sink=with_sink),
        grid=(batch, n_pairs),
        in_specs=in_specs,
        out_specs=pl.BlockSpec((seq, LANES), qmap),
        out_shape=jax.ShapeDtypeStruct(q.shape, _f32),
        scratch_shapes=scratch,
        compiler_params=pltpu.CompilerParams(dimension_semantics=("parallel", "parallel"),
                                             vmem_limit_bytes=VMEM_LIMIT),
        name="attn_sink" if with_sink else "attn_dilated",
    )(*args)


def _outproj_kernel(od_ref, os_ref, x_ref, gd_ref, gs_ref, w_ref, b_ref, g2_ref, wr_ref, br_ref,
                    x1_ref, ri_ref, rg_ref, cnt_ref, carry_ref, before_ref):
    tm = x_ref.shape[0]

    @pl.when(pl.program_id(0) == 0)
    def _():
        carry_ref[...] = jnp.zeros_like(carry_ref)
        ti = lax.broadcasted_iota(jnp.int32, (tm, tm), 0)
        tj = lax.broadcasted_iota(jnp.int32, (tm, tm), 1)
        before_ref[...] = (ti < tj).astype(_bf16)

    sub = tm // OUTPROJ_CHAINS
    h2 = []
    for s in range(OUTPROJ_CHAINS):
        rows = slice(s * sub, (s + 1) * sub)
        mixed = jnp.concatenate([_rms(od_ref[rows, :], gd_ref[...]), _rms(os_ref[rows, :], gs_ref[...])], axis=1)
        x1 = x_ref[rows, :] + jnp.dot(mixed.astype(_bf16), w_ref[...], preferred_element_type=_f32) + b_ref[...]
        _store_token_rows(x1_ref.at[pl.ds(s * sub * ROW_TILES, sub * ROW_TILES)], x1)
        h2.append(_rms(x1, g2_ref[...]))
    h2 = jnp.concatenate(h2, axis=0)
    h_hi = h2.astype(_bf16)
    h_lo = (h2 - h_hi.astype(_f32)).astype(_bf16)
    nt_dot = lambda a, b: lax.dot_general(a, b, (((1,), (1,)), ((), ())), preferred_element_type=_f32)
    logits = (nt_dot(wr_ref[0], h_hi) + nt_dot(wr_ref[0], h_lo) + nt_dot(wr_ref[1], h_hi)) + br_ref[...]
    eiota = lax.broadcasted_iota(jnp.int32, logits.shape, 0).astype(_f32)
    vals, idxs, sels = [], [], []
    l = logits
    for _ in range(TOP_K):
        m = jnp.max(l, axis=0, keepdims=True)
        idx = jnp.min(jnp.where(l == m, eiota, float(N_EXPERTS)), axis=0, keepdims=True)
        sel = eiota == idx
        vals.append(m); idxs.append(idx); sels.append(sel)
        l = jnp.where(sel, -jnp.inf, l)
    exps = [jnp.exp(v - vals[0]) for v in vals]
    den = functools.reduce(lambda a, b: a + b, exps)
    gates = [e / den for e in exps]

    chosen = functools.reduce(jnp.logical_or, sels).astype(_f32)
    prefix = jnp.dot(chosen.astype(_bf16), before_ref[...], preferred_element_type=_f32) + carry_ref[...]
    ranks = [jnp.sum(jnp.where(s, prefix, 0.0), axis=0, keepdims=True).astype(jnp.int32) for s in sels]
    carry_ref[...] = carry_ref[...] + jnp.sum(chosen, axis=1, keepdims=True)

    ri_ref[...] = jnp.concatenate([ix.astype(jnp.int32) for ix in idxs] + ranks, axis=0)
    rg_ref[...] = jnp.concatenate(gates + gates, axis=0)
    cnt_ref[...] = jnp.broadcast_to(carry_ref[...], cnt_ref.shape)


def _outproj_call(od, osw, x2, gd, gs, w, b, g2, wr_t, br):
    T = x2.shape[0]
    tm = TOKEN_TILE
    row = lambda i: (i, 0)
    col = lambda i: (0, i)
    fix = lambda i: (0, 0)
    return pl.pallas_call(
        _outproj_kernel,
        grid=(T // tm,),
        in_specs=[pl.BlockSpec((tm, W_DIL), row), pl.BlockSpec((tm, W_SWA), row),
                  pl.BlockSpec((tm, D_MODEL), row),
                  pl.BlockSpec((1, W_DIL), fix), pl.BlockSpec((1, W_SWA), fix),
                  pl.BlockSpec((D_MODEL, D_MODEL), fix), pl.BlockSpec((1, D_MODEL), fix),
                  pl.BlockSpec((1, D_MODEL), fix),
                  pl.BlockSpec((2, N_EXPERTS, D_MODEL), lambda i: (0, 0, 0)), pl.BlockSpec((N_EXPERTS, 1), fix)],
        out_specs=[pl.BlockSpec((tm * ROW_TILES, LANES), row), pl.BlockSpec((2 * TOP_K, tm), col),
                   pl.BlockSpec((2 * TOP_K, tm), col), pl.BlockSpec((N_EXPERTS, LANES), fix)],
        out_shape=[jax.ShapeDtypeStruct((T * ROW_TILES, LANES), _f32),
                   jax.ShapeDtypeStruct((2 * TOP_K, T), jnp.int32),
                   jax.ShapeDtypeStruct((2 * TOP_K, T), _f32),
                   jax.ShapeDtypeStruct((N_EXPERTS, LANES), _f32)],
        scratch_shapes=[pltpu.VMEM((N_EXPERTS, 1), _f32), pltpu.VMEM((tm, tm), _bf16)],
        compiler_params=pltpu.CompilerParams(dimension_semantics=("arbitrary",),
                                             vmem_limit_bytes=VMEM_LIMIT),
        name="outproj_router",
    )(od, osw, x2, gd, gs, w, b, g2, wr_t, br)


def _dispatch_kernel(d_ref, x1_ref, xs_hbm, sem):
    td = DISPATCH_TILE

    def start_group(g, c):
        for rr in range(DMA_GROUP):
            r = g * DMA_GROUP + rr
            src = _token_row(x1_ref, r)
            for k in range(TOP_K):
                pltpu.make_async_copy(src, _token_row(xs_hbm, d_ref[0, k * td + r]), sem).start(priority=k % 2)
        return c

    lax.fori_loop(0, td // DMA_GROUP, start_group, 0)

    def wait_group(g, c):
        for _ in range(DMA_GROUP * TOP_K):
            pltpu.make_async_copy(_token_row(x1_ref, 0), _token_row(xs_hbm, 0), sem).wait()
        return c

    lax.fori_loop(0, td // DMA_GROUP, wait_group, 0)


def _dispatch_call(dest_tiles, x1r, n_slots):
    nt = dest_tiles.shape[0]
    return pl.pallas_call(
        _dispatch_kernel,
        grid=(nt,),
        in_specs=[pl.BlockSpec((None, 1, TOP_K * DISPATCH_TILE), lambda i: (i, 0, 0), memory_space=pltpu.SMEM),
                  pl.BlockSpec((DISPATCH_TILE * ROW_TILES, LANES), lambda i: (i, 0))],
        out_specs=pl.BlockSpec(memory_space=pl.ANY),
        out_shape=jax.ShapeDtypeStruct((n_slots * ROW_TILES, LANES), _f32),
        scratch_shapes=[pltpu.SemaphoreType.DMA(())],
        compiler_params=pltpu.CompilerParams(dimension_semantics=("arbitrary",),
                                             vmem_limit_bytes=VMEM_LIMIT),
        name="dispatch_rows",
    )(dest_tiles, x1r)


def _moe_kernel(be_ref, nv_ref, src_ref, grp_ref, nxt_ref, xs_ref, g2_ref, w1_hbm, b1_ref, w2_hbm, b2_ref,
                ys_ref, st1, st2, w1p, w2b, sem):
    i = pl.program_id(0)
    nv = nv_ref[i]
    tm = EXPERT_TILE

    def weight_copies(e, slot):
        return (pltpu.make_async_copy(w1_hbm.at[e], st1.at[slot], sem.at[0, slot]),
                pltpu.make_async_copy(w2_hbm.at[e], st2.at[slot], sem.at[1, slot]))

    @pl.when(nv > 0)
    def _():
        first_of_expert = jnp.logical_or(i == 0, be_ref[i] != be_ref[jnp.maximum(i - 1, 0)])

        @pl.when(first_of_expert)
        def _():
            slot = grp_ref[i] % 2

            @pl.when(i == 0)
            def _():
                for c in weight_copies(be_ref[0], 0):
                    c.start()

            for c in weight_copies(be_ref[i], slot):
                c.wait()

            @pl.when(nxt_ref[i] >= 0)
            def _():
                for c in weight_copies(nxt_ref[i], 1 - slot):
                    c.start()

            src = lax.broadcasted_iota(jnp.int32, (GLU_GROUP, GLU_GROUP), 0)
            dst = lax.broadcasted_iota(jnp.int32, (GLU_GROUP, GLU_GROUP), 1)
            half = GLU_GROUP // 2
            perm = (src == jnp.where(dst < half, 2 * dst, 2 * (dst - half) + 1)).astype(_bf16)
            for g in range(2 * D_FF // GLU_GROUP):
                cols = slice(g * GLU_GROUP, (g + 1) * GLU_GROUP)
                w1p[:, cols] = jnp.dot(st1[slot, :, cols].astype(_bf16), perm,
                                       preferred_element_type=_f32).astype(_bf16)
            w2b[...] = st2[slot].astype(_bf16)

        hm = tm // MOE_CHAINS

        def sub_block(s):
            rows = lambda c: pl.ds(s * hm * ROW_TILES + c, hm, stride=ROW_TILES)
            x = jnp.concatenate([xs_ref[rows(c), :] for c in range(ROW_TILES)], axis=1)
            x = jnp.where(lax.broadcasted_iota(jnp.int32, (hm, 1), 0) < nv - s * hm, x, 0.0)
            h = _rms(x, g2_ref[...]).astype(_bf16)
            u = jnp.dot(h, w1p[...], preferred_element_type=_f32) + b1_ref[0]
            half = GLU_GROUP // 2
            hid = []
            for g in range(2 * D_FF // GLU_GROUP):
                glu = jnp.minimum(u[:, g * GLU_GROUP:g * GLU_GROUP + half], SWIGLU_LIMIT)
                lin = jnp.clip(u[:, g * GLU_GROUP + half:(g + 1) * GLU_GROUP], -SWIGLU_LIMIT, SWIGLU_LIMIT)
                hid.append(glu * (1.0 / (1.0 + jnp.exp(-SWIGLU_ALPHA * glu))) * (lin + 1.0))
            hid = jnp.concatenate(hid, axis=1).astype(_bf16)
            y = jnp.dot(hid, w2b[...], preferred_element_type=_f32) + b2_ref[0]
            for c in range(ROW_TILES):
                ys_ref[rows(c), :] = y[:, c * LANES:(c + 1) * LANES]

        for n_sub in range(1, MOE_CHAINS + 1):
            @pl.when(jnp.logical_and(nv > (n_sub - 1) * hm, nv <= n_sub * hm))
            def _(n_sub=n_sub):
                for s in range(n_sub):
                    sub_block(s)


def _moe_call(block_e, n_valid, block_src, block_grp, next_e, xs, g2, w1, b1p, w2, b2):
    nblk = block_e.shape[0]
    tm = EXPERT_TILE
    by_e = lambda i, be, nv, src, grp, nxt: (be[i], 0, 0)
    by_src = lambda i, be, nv, src, grp, nxt: (src[i], 0)
    fix = lambda i, be, nv, src, grp, nxt: (0, 0)
    grid_spec = pltpu.PrefetchScalarGridSpec(
        num_scalar_prefetch=5,
        grid=(nblk,),
        in_specs=[pl.BlockSpec((tm * ROW_TILES, LANES), by_src), pl.BlockSpec((1, D_MODEL), fix),
                  pl.BlockSpec(memory_space=pl.ANY), pl.BlockSpec((1, 1, 2 * D_FF), by_e),
                  pl.BlockSpec(memory_space=pl.ANY), pl.BlockSpec((1, 1, D_MODEL), by_e)],
        out_specs=pl.BlockSpec((tm * ROW_TILES, LANES), by_src),
        scratch_shapes=[pltpu.VMEM((2, D_MODEL, 2 * D_FF), _f32), pltpu.VMEM((2, D_FF, D_MODEL), _f32),
                        pltpu.VMEM((D_MODEL, 2 * D_FF), _bf16), pltpu.VMEM((D_FF, D_MODEL), _bf16),
                        pltpu.SemaphoreType.DMA((2, 2))],
    )
    return pl.pallas_call(
        _moe_kernel,
        grid_spec=grid_spec,
        out_shape=jax.ShapeDtypeStruct(xs.shape, _f32),
        compiler_params=pltpu.CompilerParams(dimension_semantics=("arbitrary",),
                                             vmem_limit_bytes=VMEM_LIMIT),
        name="moe_experts",
    )(block_e, n_valid, block_src, block_grp, next_e, xs, g2, w1, b1p, w2, b2)


def _combine_kernel(dcur_ref, dnext_ref, x1_ref, ys_hbm, g_ref, gf_ref, o_ref, gbuf, sem):
    tc = COMBINE_TILE
    i = pl.program_id(0)
    slot = i % 2

    def row_copy(d_ref, s, r, k):
        dst = gbuf.at[s, k, pl.ds(r * ROW_TILES, ROW_TILES)]
        return pltpu.make_async_copy(_token_row(ys_hbm, d_ref[0, k * tc + r]), dst, sem.at[s])

    def issue(d_ref, s):
        def group(g, c):
            for rr in range(DMA_GROUP):
                for k in range(TOP_K):
                    row_copy(d_ref, s, g * DMA_GROUP + rr, k).start(priority=k % 2)
            return c
        lax.fori_loop(0, tc // DMA_GROUP, group, 0)

    @pl.when(i == 0)
    def _():
        issue(dcur_ref, 0)

    @pl.when(i + 1 < pl.num_programs(0))
    def _():
        issue(dnext_ref, 1 - slot)

    def wait_group(g, c):
        for _ in range(DMA_GROUP * TOP_K):
            row_copy(dcur_ref, slot, 0, 0).wait()
        return c

    lax.fori_loop(0, tc // DMA_GROUP, wait_group, 0)

    g = g_ref[...]
    acc = _load_token_rows(x1_ref, tc)
    for k in range(TOP_K):
        acc = acc + g[:, k:k + 1] * _load_token_rows(gbuf.at[slot, k], tc)
    o_ref[...] = _rms(acc, gf_ref[...])


def _combine_call(dest_tiles, x1r, ys, gates_t, gf):
    nt = dest_tiles.shape[0]
    tc = COMBINE_TILE
    row = lambda i: (i, 0)
    return pl.pallas_call(
        _combine_kernel,
        grid=(nt,),
        in_specs=[pl.BlockSpec((None, 1, TOP_K * tc), lambda i: (i, 0, 0), memory_space=pltpu.SMEM),
                  pl.BlockSpec((None, 1, TOP_K * tc), lambda i: (jnp.minimum(i + 1, nt - 1), 0, 0),
                               memory_space=pltpu.SMEM),
                  pl.BlockSpec((tc * ROW_TILES, LANES), row), pl.BlockSpec(memory_space=pl.ANY),
                  pl.BlockSpec((tc, TOP_K), row), pl.BlockSpec((1, D_MODEL), lambda i: (0, 0))],
        out_specs=pl.BlockSpec((tc, D_MODEL), row),
        out_shape=jax.ShapeDtypeStruct((nt * tc, D_MODEL), _f32),
        scratch_shapes=[pltpu.VMEM((2, TOP_K, tc * ROW_TILES, LANES), _f32), pltpu.SemaphoreType.DMA((2,))],
        compiler_params=pltpu.CompilerParams(dimension_semantics=("arbitrary",),
                                             vmem_limit_bytes=VMEM_LIMIT),
        name="combine_norm",
    )(dest_tiles, dest_tiles, x1r, ys, gates_t, gf)


def _plan_blocks(route_i, counts, n_tokens):
    tm = EXPERT_TILE
    nblk = (n_tokens * TOP_K + N_EXPERTS * tm) // tm
    counts = counts.astype(jnp.int32)
    blocks_e = (counts + tm - 1) // tm
    blk_end = jnp.cumsum(blocks_e)
    pad_start = (blk_end - blocks_e) * tm
    experts = jnp.arange(N_EXPERTS, dtype=jnp.int32)
    hot = route_i[:TOP_K, :, None] == experts
    dest = jnp.sum(jnp.where(hot, pad_start, 0), axis=-1) + route_i[TOP_K:]
    n_used = blk_end[-1]
    blk = jnp.arange(nblk, dtype=jnp.int32)
    src = jnp.minimum(blk, jnp.maximum(n_used - 1, 0))
    be = jnp.minimum(jnp.sum((blk_end[None, :] <= src[:, None]).astype(jnp.int32), axis=1), N_EXPERTS - 1)
    hot_b = be[:, None] == experts
    pick = lambda table: jnp.sum(jnp.where(hot_b, table, 0), axis=1)
    n_valid = jnp.where(blk < n_used, jnp.clip(pick(counts) - (src * tm - pick(pad_start)), 0, tm), 0)
    used = blocks_e > 0
    grp_e = jnp.cumsum(used.astype(jnp.int32)) - 1
    later = jnp.where((experts[None, :] > experts[:, None]) & used[None, :], experts[None, :], N_EXPERTS)
    nxt_e = jnp.min(later, axis=1)
    nxt_e = jnp.where(nxt_e < N_EXPERTS, nxt_e, -1)
    return be, n_valid.astype(jnp.int32), src, pick(grp_e), pick(nxt_e), dest


def _split_bf16(w):
    hi = w.astype(_bf16)
    return jnp.stack([hi, (w - hi.astype(_f32)).astype(_bf16)])


def _tile_dest(dest, tile):
    k, t = dest.shape
    return dest.reshape(k, t // tile, tile).transpose(1, 0, 2).reshape(t // tile, 1, k * tile)


def kernel(x, positions, norm1_g, w_in, b_in, sinks, gn_dil, gn_swa, w_out, b_out,
           norm2_g, w_router, b_router, w_mlp1, b_mlp1, w_mlp2, b_mlp2, norm_f_g):
    batch, seq, d = x.shape
    T = batch * seq
    x2 = x.reshape(T, d)
    depth = norm1_g.shape[0]
    assert depth == 1, "the final norm is fused into the last layer's combine kernel"
    inv = ROPE_THETA ** (-jnp.arange(0, HEAD_DIM, 2, dtype=_f32) / HEAD_DIM)
    inv = jnp.tile(inv, LANES // (HEAD_DIM // 2)).reshape(1, LANES)
    pos = positions.reshape(T, 1)
    for layer in range(depth):
        qa, ka, va, qb, kb, vb = _inproj_call(
            x2, pos, inv, norm1_g[layer].reshape(1, d), w_in[layer].astype(_bf16), b_in[layer].reshape(1, D_IN))
        o_dil = _attn_call(qa, ka, va, None, batch=batch, seq=seq,
                           configs=tuple((w // r, r) for w, r in DILATED_CONFIGS), kv_group=1)
        sink_lanes = jnp.repeat(sinks[layer].astype(_f32), HEAD_DIM).reshape(1, W_SWA)
        o_swa = _attn_call(qb, kb, vb, sink_lanes, batch=batch, seq=seq,
                           configs=((SWA_WINDOW - 1, 1),), kv_group=N_HEADS_SWA // N_KV_SWA // PAIR)
        x1r, route_i, route_g, counts = _outproj_call(
            o_dil, o_swa, x2, gn_dil[layer].reshape(1, W_DIL), gn_swa[layer].reshape(1, W_SWA),
            w_out[layer].astype(_bf16), b_out[layer].reshape(1, d), norm2_g[layer].reshape(1, d),
            _split_bf16(w_router[layer].T), b_router[layer].reshape(N_EXPERTS, 1))
        block_e, n_valid, block_src, block_grp, next_e, dest = _plan_blocks(route_i, counts[:, 0], T)
        xs = _dispatch_call(_tile_dest(dest, DISPATCH_TILE), x1r, block_e.shape[0] * EXPERT_TILE)
        half = GLU_GROUP // 2
        b1p = b_mlp1[layer].reshape(N_EXPERTS, 2 * D_FF // GLU_GROUP, half, 2)
        b1p = jnp.swapaxes(b1p, 2, 3).reshape(N_EXPERTS, 1, 2 * D_FF)
        ys = _moe_call(block_e, n_valid, block_src, block_grp, next_e, xs, norm2_g[layer].reshape(1, d),
                       w_mlp1[layer], b1p, w_mlp2[layer], b_mlp2[layer].reshape(N_EXPERTS, 1, d))
        x2 = _combine_call(_tile_dest(dest, COMBINE_TILE), x1r, ys, route_g[:TOP_K].T, norm_f_g.reshape(1, d))
    return x2.reshape(batch, seq, d)
```

```python
import functools

import jax
import jax.numpy as jnp
from jax import lax
from jax.experimental import pallas as pl
from jax.experimental.pallas import tpu as pltpu

D_MODEL = 1024
HEAD_DIM = 64
N_HEADS_DIL = 8
DILATED_CONFIGS = ((128, 1), (512, 4), (2048, 16))
N_HEADS_SWA = 8
N_KV_SWA = 2
SWA_WINDOW = 128
ATTN_BLOCK = 128
ROPE_THETA = 10000.0
N_EXPERTS = 32
TOP_K = 4
D_FF = D_MODEL
SWIGLU_ALPHA = 1.702
SWIGLU_LIMIT = 7.0
NORM_EPS = 1e-5
LOG2E = 1.4426950408889634

W_DIL = N_HEADS_DIL * HEAD_DIM
W_SWA = N_HEADS_SWA * HEAD_DIM
W_KV_SWA = N_KV_SWA * HEAD_DIM
D_IN = 3 * W_DIL + W_SWA + 2 * W_KV_SWA

LANES = 128
PAIR = LANES // HEAD_DIM
SUBLANES = 8
ROW_TILES = D_MODEL // LANES
TOKEN_TILE = 1024
EXPERT_TILE = 1024
DISPATCH_TILE = 2048
COMBINE_TILE = 256
DMA_GROUP = 8
ATTN_UNROLL = 32
MOE_CHAINS = 4
INPROJ_TILE = 1024
INPROJ_CHAINS = 4
OUTPROJ_CHAINS = 2
GLU_GROUP = 256
VMEM_LIMIT = 56 * 1024 * 1024

_f32 = jnp.float32
_bf16 = jnp.bfloat16


def _rms(x, g):
    return x * lax.rsqrt(jnp.mean(x * x, axis=-1, keepdims=True) + NORM_EPS) * g


def _store_token_rows(ref, x):
    tm = x.shape[0]
    for c in range(ROW_TILES):
        ref[pl.ds(c, tm, stride=ROW_TILES), :] = x[:, c * LANES:(c + 1) * LANES]


def _load_token_rows(ref, tm):
    return jnp.concatenate([ref[pl.ds(c, tm, stride=ROW_TILES), :] for c in range(ROW_TILES)], axis=1)


def _token_row(ref, tok):
    return ref.at[pl.ds(pl.multiple_of(tok * ROW_TILES, ROW_TILES), ROW_TILES)]


def _inproj_kernel(x_ref, pos_ref, inv_ref, g_ref, w_ref, b_ref, *out_refs):
    sub = x_ref.shape[0] // INPROJ_CHAINS
    for s in range(INPROJ_CHAINS):
        rows = slice(s * sub, (s + 1) * sub)
        _inproj_rows(x_ref[rows, :], pos_ref[rows, :], inv_ref, g_ref, w_ref, b_ref,
                     [r.at[rows, :] for r in out_refs])


def _inproj_rows(x, pos, inv_ref, g_ref, w_ref, b_ref, out_refs):
    qa_ref, ka_ref, va_ref, qb_ref, kb_ref, vb_ref = out_refs
    h = _rms(x, g_ref[...]).astype(_bf16)
    proj = jnp.dot(h, w_ref[...], preferred_element_type=_f32) + b_ref[...]

    ang = pos.astype(_f32) * inv_ref[...]
    lane = lax.broadcasted_iota(jnp.int32, ang.shape, 1)
    first_half = (lane % HEAD_DIM) < (HEAD_DIM // 2)
    cos = jnp.cos(ang)
    sin = jnp.where(first_half, -jnp.sin(ang), jnp.sin(ang))

    def rope(t, scale):
        outs = []
        for c in range(t.shape[1] // LANES):
            tc = t[:, c * LANES:(c + 1) * LANES]
            partner = jnp.where(first_half, pltpu.roll(tc, LANES - HEAD_DIM // 2, 1),
                                pltpu.roll(tc, HEAD_DIM // 2, 1))
            outs.append((tc * cos + partner * sin) * scale)
        return jnp.concatenate(outs, axis=1) if len(outs) > 1 else outs[0]

    scale = HEAD_DIM ** -0.5 * LOG2E
    o = 0
    qa_ref[...] = rope(proj[:, o:o + W_DIL], scale).astype(_bf16); o += W_DIL
    ka_ref[...] = rope(proj[:, o:o + W_DIL], 1.0).astype(_bf16); o += W_DIL
    va_ref[...] = proj[:, o:o + W_DIL].astype(_bf16); o += W_DIL
    qb_ref[...] = rope(proj[:, o:o + W_SWA], scale).astype(_bf16); o += W_SWA
    kb = rope(proj[:, o:o + W_KV_SWA], 1.0); o += W_KV_SWA
    vb = proj[:, o:o + W_KV_SWA]

    def dup_heads(t):
        sw = pltpu.roll(t, HEAD_DIM, 1)
        lo = lane < HEAD_DIM
        return jnp.concatenate([jnp.where(lo, t, sw), jnp.where(lo, sw, t)], axis=1)

    kb_ref[...] = dup_heads(kb).astype(_bf16)
    vb_ref[...] = dup_heads(vb).astype(_bf16)


def _inproj_call(x2, pos, inv, g, w, b):
    T = x2.shape[0]
    tm = INPROJ_TILE
    row = lambda i: (i, 0)
    fix = lambda i: (0, 0)
    widths = (W_DIL, W_DIL, W_DIL, W_SWA, PAIR * W_KV_SWA, PAIR * W_KV_SWA)
    return pl.pallas_call(
        _inproj_kernel,
        grid=(T // tm,),
        in_specs=[pl.BlockSpec((tm, D_MODEL), row), pl.BlockSpec((tm, 1), row),
                  pl.BlockSpec((1, LANES), fix), pl.BlockSpec((1, D_MODEL), fix),
                  pl.BlockSpec((D_MODEL, D_IN), fix), pl.BlockSpec((1, D_IN), fix)],
        out_specs=[pl.BlockSpec((tm, wd), row) for wd in widths],
        out_shape=[jax.ShapeDtypeStruct((T, wd), _bf16) for wd in widths],
        compiler_params=pltpu.CompilerParams(dimension_semantics=("parallel",),
                                             vmem_limit_bytes=VMEM_LIMIT),
        name="inproj",
    )(x2, pos, inv, g, w, b)


def _attn_kernel(*refs, configs, seq, with_sink):
    if with_sink:
        q_ref, k_ref, v_ref, sink_ref, o_ref = refs[:5]
        scratch = refs[5:]
    else:
        q_ref, k_ref, v_ref, o_ref = refs[:4]
        scratch = refs[4:]
    n_cfg = len(configs)
    obufs = scratch[:n_cfg]
    mbufs = scratch[n_cfg:2 * n_cfg]
    lbufs = scratch[2 * n_cfg:3 * n_cfg]
    extra = list(scratch[3 * n_cfg:])
    blk = ATTN_BLOCK
    dilated = [r for _, r in configs if r > 1]
    if dilated:
        qf, kf, vf = extra[:3]
        qf[...] = q_ref[...].astype(_f32)
        kf[...] = k_ref[...].astype(_f32)
        vf[...] = v_ref[...].astype(_f32)
        extra = extra[3:]

    lane = lax.broadcasted_iota(jnp.int32, (blk, LANES), 1)
    head0 = lane < HEAD_DIM
    lane_row = lax.broadcasted_iota(jnp.int32, (1, LANES), 1)
    head_rows = [(lane_row < HEAD_DIM).astype(_bf16), (lane_row >= HEAD_DIM).astype(_bf16)]
    qi = lax.broadcasted_iota(jnp.int32, (blk, 2 * blk), 0)
    kj = lax.broadcasted_iota(jnp.int32, (blk, 2 * blk), 1)
    dist = qi - kj + blk

    base_perm = None
    for ci, (max_dist, r) in enumerate(configs):
        nb = seq // r // blk
        band = (dist >= 0) & (dist <= max_dist)
        obuf, mbuf, lbuf = obufs[ci], mbufs[ci], lbufs[ci]
        if r == 1:
            qp, kp, vp = q_ref, k_ref, v_ref
        else:
            perm_bf16 = extra[:3]
            extra = extra[3:]
            later = [r2 for _, r2 in configs[ci + 1:] if r2 % r == 0]
            perm_f32 = []
            if later:
                perm_f32 = extra[:3]
                extra = extra[3:]
            r_src, srcs = 1, (qf, kf, vf)
            if base_perm is not None and r % base_perm[0] == 0:
                r_src, srcs = base_perm
            if perm_f32:
                base_perm = (r, perm_f32)
            qp, kp, vp = perm_bf16
            piece = 2 * blk
            step = r // r_src
            per_res = seq // r // piece
            len_src = seq // r_src

            def permute(idx, carry, r_src=r_src, srcs=srcs, step=step, per_res=per_res, len_src=len_src,
                        perm_bf16=perm_bf16, perm_f32=perm_f32):
                c = idx // per_res
                start = (c % r_src) * len_src + c // r_src + (step * piece) * (idx % per_res)
                src = pl.ds(start, piece, stride=step)
                dst = pl.ds(pl.multiple_of(idx * piece, piece), piece)
                for t in range(3):
                    val = srcs[t][src, :]
                    perm_bf16[t][dst, :] = val.astype(_bf16)
                    if perm_f32:
                        perm_f32[t][dst, :] = val
                return carry

            lax.fori_loop(0, seq // piece, permute, 0, unroll=4)

        def body(n, carry, r=r, nb=nb, band=band, obuf=obuf, lbuf=lbuf, qp=qp, kp=kp, vp=vp):
            j = n % nb
            cur = pl.ds(pl.multiple_of(n * blk, blk), blk)
            prev = pl.ds(pl.multiple_of(jnp.maximum(n - 1, 0) * blk, blk), blk)
            q = qp[cur, :]
            k2 = jnp.concatenate([kp[prev, :], kp[cur, :]], axis=0)
            v2 = jnp.concatenate([vp[prev, :], vp[cur, :]], axis=0)
            mask = band & (kj >= jnp.where(j > 0, 0, blk))
            q2 = jnp.concatenate([q * hm for hm in head_rows], axis=0)
            s2 = lax.dot_general(q2, k2, (((1,), (1,)), ((), ())), preferred_element_type=_f32)
            ps, maxs, sums = [], [], []
            for h in range(PAIR):
                s = jnp.where(mask, s2[h * blk:(h + 1) * blk], -jnp.inf)
                m = jnp.max(s, axis=-1, keepdims=True)
                p = jnp.exp2(s - m)
                ps.append(p.astype(_bf16))
                maxs.append(jnp.broadcast_to(m, (blk, LANES)))
                sums.append(jnp.broadcast_to(jnp.sum(p, axis=-1, keepdims=True), (blk, LANES)))
            o2 = jnp.dot(jnp.concatenate(ps, axis=0), v2, preferred_element_type=_f32)
            outs = [o2[:blk], o2[blk:]]
            start = n // nb + (r * blk) * j
            dst = pl.ds(pl.multiple_of(start, blk), blk) if r == 1 else pl.ds(start, blk, stride=r)
            obuf[dst, :] = jnp.where(head0, outs[0], outs[1])
            mbuf[dst, :] = jnp.where(head0, maxs[0], maxs[1])
            lbuf[dst, :] = jnp.where(head0, sums[0], sums[1])
            return carry

        lax.fori_loop(0, seq // blk, body, 0, unroll=ATTN_UNROLL)

    chunk = 4 * blk

    def finish(i, carry):
        sl = pl.ds(pl.multiple_of(i * chunk, chunk), chunk)
        ms = [mb[sl, :] for mb in mbufs]
        if with_sink:
            ms.append(jnp.broadcast_to(sink_ref[...] * LOG2E, ms[0].shape))
        mx = functools.reduce(jnp.maximum, ms)
        es = [jnp.exp2(m - mx) for m in ms]
        num = functools.reduce(lambda a, b: a + b, [e * ob[sl, :] for e, ob in zip(es, obufs)])
        den = functools.reduce(lambda a, b: a + b, [e * lb[sl, :] for e, lb in zip(es, lbufs)])
        if with_sink:
            den = den + es[-1]
        o_ref[sl, :] = num / den
        return carry

    lax.fori_loop(0, seq // chunk, finish, 0)


def _attn_call(q, k, v, sinks, *, batch, seq, configs, kv_group):
    n_pairs = q.shape[1] // LANES
    with_sink = sinks is not None
    qmap = lambda b, p: (b, p)
    kvmap = lambda b, p: (b, p // kv_group)
    in_specs = [pl.BlockSpec((seq, LANES), qmap), pl.BlockSpec((seq, LANES), kvmap),
                pl.BlockSpec((seq, LANES), kvmap)]
    args = [q, k, v]
    if with_sink:
        in_specs.append(pl.BlockSpec((1, LANES), lambda b, p: (0, p)))
        args.append(sinks)
    buf = lambda dt: pltpu.VMEM((seq, LANES), dt)
    scratch = [buf(_f32) for _ in range(3 * len(configs))]
    if any(r > 1 for _, r in configs):
        scratch += [buf(_f32) for _ in range(3)]
    for ci, (_, r) in enumerate(configs):
        if r > 1:
            scratch += [buf(_bf16) for _ in range(3)]
            if any(r2 % r == 0 for _, r2 in configs[ci + 1:]):
                scratch += [buf(_f32) for _ in range(3)]
    return pl.pallas_call(
        functools.partial(_attn_kernel, configs=configs, seq=seq, with_sink=with_sink),
        grid=(batch, n_pairs),
        in_specs=in_specs,
        out_specs=pl.BlockSpec((seq, LANES), qmap),
        out_shape=jax.ShapeDtypeStruct(q.shape, _f32),
        scratch_shapes=scratch,
        compiler_params=pltpu.CompilerParams(dimension_semantics=("parallel", "parallel"),
                                             vmem_limit_bytes=VMEM_LIMIT),
        name="attn_sink" if with_sink else "attn_dilated",
    )(*args)


def _outproj_kernel(od_ref, os_ref, x_ref, gd_ref, gs_ref, w_ref, b_ref, g2_ref, wr_ref, br_ref,
                    x1_ref, ri_ref, rg_ref, cnt_ref, carry_ref, before_ref):
    tm = x_ref.shape[0]

    @pl.when(pl.program_id(0) == 0)
    def _():
        carry_ref[...] = jnp.zeros_like(carry_ref)
        ti = lax.broadcasted_iota(jnp.int32, (tm, tm), 0)
        tj = lax.broadcasted_iota(jnp.int32, (tm, tm), 1)
        before_ref[...] = (ti < tj).astype(_bf16)

    sub = tm // OUTPROJ_CHAINS
    h2 = []
    for s in range(OUTPROJ_CHAINS):
        rows = slice(s * sub, (s + 1) * sub)
        mixed = jnp.concatenate([_rms(od_ref[rows, :], gd_ref[...]), _rms(os_ref[rows, :], gs_ref[...])], axis=1)
        x1 = x_ref[rows, :] + jnp.dot(mixed.astype(_bf16), w_ref[...], preferred_element_type=_f32) + b_ref[...]
        _store_token_rows(x1_ref.at[pl.ds(s * sub * ROW_TILES, sub * ROW_TILES)], x1)
        h2.append(_rms(x1, g2_ref[...]))
    h2 = jnp.concatenate(h2, axis=0)
    h_hi = h2.astype(_bf16)
    h_lo = (h2 - h_hi.astype(_f32)).astype(_bf16)
    nt_dot = lambda a, b: lax.dot_general(a, b, (((1,), (1,)), ((), ())), preferred_element_type=_f32)
    logits = (nt_dot(wr_ref[0], h_hi) + nt_dot(wr_ref[0], h_lo) + nt_dot(wr_ref[1], h_hi)) + br_ref[...]
    eiota = lax.broadcasted_iota(jnp.int32, logits.shape, 0).astype(_f32)
    vals, idxs, sels = [], [], []
    l = logits
    for _ in range(TOP_K):
        m = jnp.max(l, axis=0, keepdims=True)
        idx = jnp.min(jnp.where(l == m, eiota, float(N_EXPERTS)), axis=0, keepdims=True)
        sel = eiota == idx
        vals.append(m); idxs.append(idx); sels.append(sel)
        l = jnp.where(sel, -jnp.inf, l)
    exps = [jnp.exp(v - vals[0]) for v in vals]
    den = functools.reduce(lambda a, b: a + b, exps)
    gates = [e / den for e in exps]

    chosen = functools.reduce(jnp.logical_or, sels).astype(_f32)
    prefix = jnp.dot(chosen.astype(_bf16), before_ref[...], preferred_element_type=_f32) + carry_ref[...]
    ranks = [jnp.sum(jnp.where(s, prefix, 0.0), axis=0, keepdims=True).astype(jnp.int32) for s in sels]
    carry_ref[...] = carry_ref[...] + jnp.sum(chosen, axis=1, keepdims=True)

    ri_ref[...] = jnp.concatenate([ix.astype(jnp.int32) for ix in idxs] + ranks, axis=0)
    rg_ref[...] = jnp.concatenate(gates + gates, axis=0)
    cnt_ref[...] = jnp.broadcast_to(carry_ref[...], cnt_ref.shape)


def _outproj_call(od, osw, x2, gd, gs, w, b, g2, wr_t, br):
    T = x2.shape[0]
    tm = TOKEN_TILE
    row = lambda i: (i, 0)
    col = lambda i: (0, i)
    fix = lambda i: (0, 0)
    return pl.pallas_call(
        _outproj_kernel,
        grid=(T // tm,),
        in_specs=[pl.BlockSpec((tm, W_DIL), row), pl.BlockSpec((tm, W_SWA), row),
                  pl.BlockSpec((tm, D_MODEL), row),
                  pl.BlockSpec((1, W_DIL), fix), pl.BlockSpec((1, W_SWA), fix),
                  pl.BlockSpec((D_MODEL, D_MODEL), fix), pl.BlockSpec((1, D_MODEL), fix),
                  pl.BlockSpec((1, D_MODEL), fix),
                  pl.BlockSpec((2, N_EXPERTS, D_MODEL), lambda i: (0, 0, 0)), pl.BlockSpec((N_EXPERTS, 1), fix)],
        out_specs=[pl.BlockSpec((tm * ROW_TILES, LANES), row), pl.BlockSpec((2 * TOP_K, tm), col),
                   pl.BlockSpec((2 * TOP_K, tm), col), pl.BlockSpec((N_EXPERTS, LANES), fix)],
        out_shape=[jax.ShapeDtypeStruct((T * ROW_TILES, LANES), _f32),
                   jax.ShapeDtypeStruct((2 * TOP_K, T), jnp.int32),
                   jax.ShapeDtypeStruct((2 * TOP_K, T), _f32),
                   jax.ShapeDtypeStruct((N_EXPERTS, LANES), _f32)],
        scratch_shapes=[pltpu.VMEM((N_EXPERTS, 1), _f32), pltpu.VMEM((tm, tm), _bf16)],
        compiler_params=pltpu.CompilerParams(dimension_semantics=("arbitrary",),
                                             vmem_limit_bytes=VMEM_LIMIT),
        name="outproj_router",
    )(od, osw, x2, gd, gs, w, b, g2, wr_t, br)


def _dispatch_kernel(d_ref, x1_ref, xs_hbm, sem):
    td = DISPATCH_TILE

    def start_group(g, c):
        for rr in range(DMA_GROUP):
            r = g * DMA_GROUP + rr
            src = _token_row(x1_ref, r)
            for k in range(TOP_K):
                pltpu.make_async_copy(src, _token_row(xs_hbm, d_ref[0, k * td + r]), sem).start(priority=k % 2)
        return c

    lax.fori_loop(0, td // DMA_GROUP, start_group, 0)

    def wait_group(g, c):
        for _ in range(DMA_GROUP * TOP_K):
            pltpu.make_async_copy(_token_row(x1_ref, 0), _token_row(xs_hbm, 0), sem).wait()
        return c

    lax.fori_loop(0, td // DMA_GROUP, wait_group, 0)


def _dispatch_call(dest_tiles, x1r, n_slots):
    nt = dest_tiles.shape[0]
    return pl.pallas_call(
        _dispatch_kernel,
        grid=(nt,),
        in_specs=[pl.BlockSpec((None, 1, TOP_K * DISPATCH_TILE), lambda i: (i, 0, 0), memory_space=pltpu.SMEM),
                  pl.BlockSpec((DISPATCH_TILE * ROW_TILES, LANES), lambda i: (i, 0))],
        out_specs=pl.BlockSpec(memory_space=pl.ANY),
        out_shape=jax.ShapeDtypeStruct((n_slots * ROW_TILES, LANES), _f32),
        scratch_shapes=[pltpu.SemaphoreType.DMA(())],
        compiler_params=pltpu.CompilerParams(dimension_semantics=("arbitrary",),
                                             vmem_limit_bytes=VMEM_LIMIT),
        name="dispatch_rows",
    )(dest_tiles, x1r)


def _moe_kernel(be_ref, nv_ref, src_ref, grp_ref, nxt_ref, xs_ref, g2_ref, w1_hbm, b1_ref, w2_hbm, b2_ref,
                ys_ref, st1, st2, w1p, w2b, sem):
    i = pl.program_id(0)
    nv = nv_ref[i]
    tm = EXPERT_TILE

    def weight_copies(e, slot):
        return (pltpu.make_async_copy(w1_hbm.at[e], st1.at[slot], sem.at[0, slot]),
                pltpu.make_async_copy(w2_hbm.at[e], st2.at[slot], sem.at[1, slot]))

    @pl.when(nv > 0)
    def _():
        first_of_expert = jnp.logical_or(i == 0, be_ref[i] != be_ref[jnp.maximum(i - 1, 0)])

        @pl.when(first_of_expert)
        def _():
            slot = grp_ref[i] % 2

            @pl.when(i == 0)
            def _():
                for c in weight_copies(be_ref[0], 0):
                    c.start()

            for c in weight_copies(be_ref[i], slot):
                c.wait()

            @pl.when(nxt_ref[i] >= 0)
            def _():
                for c in weight_copies(nxt_ref[i], 1 - slot):
                    c.start()

            src = lax.broadcasted_iota(jnp.int32, (GLU_GROUP, GLU_GROUP), 0)
            dst = lax.broadcasted_iota(jnp.int32, (GLU_GROUP, GLU_GROUP), 1)
            half = GLU_GROUP // 2
            perm = (src == jnp.where(dst < half, 2 * dst, 2 * (dst - half) + 1)).astype(_bf16)
            for g in range(2 * D_FF // GLU_GROUP):
                cols = slice(g * GLU_GROUP, (g + 1) * GLU_GROUP)
                w1p[:, cols] = jnp.dot(st1[slot, :, cols].astype(_bf16), perm,
                                       preferred_element_type=_f32).astype(_bf16)
            w2b[...] = st2[slot].astype(_bf16)

        hm = tm // MOE_CHAINS

        def sub_block(s):
            rows = lambda c: pl.ds(s * hm * ROW_TILES + c, hm, stride=ROW_TILES)
            x = jnp.concatenate([xs_ref[rows(c), :] for c in range(ROW_TILES)], axis=1)
            x = jnp.where(lax.broadcasted_iota(jnp.int32, (hm, 1), 0) < nv - s * hm, x, 0.0)
            h = _rms(x, g2_ref[...]).astype(_bf16)
            u = jnp.dot(h, w1p[...], preferred_element_type=_f32) + b1_ref[0]
            half = GLU_GROUP // 2
            hid = []
            for g in range(2 * D_FF // GLU_GROUP):
                glu = jnp.minimum(u[:, g * GLU_GROUP:g * GLU_GROUP + half], SWIGLU_LIMIT)
                lin = jnp.clip(u[:, g * GLU_GROUP + half:(g + 1) * GLU_GROUP], -SWIGLU_LIMIT, SWIGLU_LIMIT)
                hid.append(glu * (1.0 / (1.0 + jnp.exp(-SWIGLU_ALPHA * glu))) * (lin + 1.0))
            hid = jnp.concatenate(hid, axis=1).astype(_bf16)
            y = jnp.dot(hid, w2b[...], preferred_element_type=_f32) + b2_ref[0]
            for c in range(ROW_TILES):
                ys_ref[rows(c), :] = y[:, c * LANES:(c + 1) * LANES]

        for n_sub in range(1, MOE_CHAINS + 1):
            @pl.when(jnp.logical_and(nv > (n_sub - 1) * hm, nv <= n_sub * hm))
            def _(n_sub=n_sub):
                for s in range(n_sub):
                    sub_block(s)


def _moe_call(block_e, n_valid, block_src, block_grp, next_e, xs, g2, w1, b1p, w2, b2):
    nblk = block_e.shape[0]
    tm = EXPERT_TILE
    by_e = lambda i, be, nv, src, grp, nxt: (be[i], 0, 0)
    by_src = lambda i, be, nv, src, grp, nxt: (src[i], 0)
    fix = lambda i, be, nv, src, grp, nxt: (0, 0)
    grid_spec = pltpu.PrefetchScalarGridSpec(
        num_scalar_prefetch=5,
        grid=(nblk,),
        in_specs=[pl.BlockSpec((tm * ROW_TILES, LANES), by_src), pl.BlockSpec((1, D_MODEL), fix),
                  pl.BlockSpec(memory_space=pl.ANY), pl.BlockSpec((1, 1, 2 * D_FF), by_e),
                  pl.BlockSpec(memory_space=pl.ANY), pl.BlockSpec((1, 1, D_MODEL), by_e)],
        out_specs=pl.BlockSpec((tm * ROW_TILES, LANES), by_src),
        scratch_shapes=[pltpu.VMEM((2, D_MODEL, 2 * D_FF), _f32), pltpu.VMEM((2, D_FF, D_MODEL), _f32),
                        pltpu.VMEM((D_MODEL, 2 * D_FF), _bf16), pltpu.VMEM((D_FF, D_MODEL), _bf16),
                        pltpu.SemaphoreType.DMA((2, 2))],
    )
    return pl.pallas_call(
        _moe_kernel,
        grid_spec=grid_spec,
        out_shape=jax.ShapeDtypeStruct(xs.shape, _f32),
        compiler_params=pltpu.CompilerParams(dimension_semantics=("arbitrary",),
                                             vmem_limit_bytes=VMEM_LIMIT),
        name="moe_experts",
    )(block_e, n_valid, block_src, block_grp, next_e, xs, g2, w1, b1p, w2, b2)


def _combine_kernel(dcur_ref, dnext_ref, x1_ref, ys_hbm, g_ref, gf_ref, o_ref, gbuf, sem):
    tc = COMBINE_TILE
    i = pl.program_id(0)
    slot = i % 2

    def row_copy(d_ref, s, r, k):
        dst = gbuf.at[s, k, pl.ds(r * ROW_TILES, ROW_TILES)]
        return pltpu.make_async_copy(_token_row(ys_hbm, d_ref[0, k * tc + r]), dst, sem.at[s])

    def issue(d_ref, s):
        def group(g, c):
            for rr in range(DMA_GROUP):
                for k in range(TOP_K):
                    row_copy(d_ref, s, g * DMA_GROUP + rr, k).start(priority=k % 2)
            return c
        lax.fori_loop(0, tc // DMA_GROUP, group, 0)

    @pl.when(i == 0)
    def _():
        issue(dcur_ref, 0)

    @pl.when(i + 1 < pl.num_programs(0))
    def _():
        issue(dnext_ref, 1 - slot)

    def wait_group(g, c):
        for _ in range(DMA_GROUP * TOP_K):
            row_copy(dcur_ref, slot, 0, 0).wait()
        return c

    lax.fori_loop(0, tc // DMA_GROUP, wait_group, 0)

    g = g_ref[...]
    acc = _load_token_rows(x1_ref, tc)
    for k in range(TOP_K):
        acc = acc + g[:, k:k + 1] * _load_token_rows(gbuf.at[slot, k], tc)
    o_ref[...] = _rms(acc, gf_ref[...])


def _combine_call(dest_tiles, x1r, ys, gates_t, gf):
    nt = dest_tiles.shape[0]
    tc = COMBINE_TILE
    row = lambda i: (i, 0)
    return pl.pallas_call(
        _combine_kernel,
        grid=(nt,),
        in_specs=[pl.BlockSpec((None, 1, TOP_K * tc), lambda i: (i, 0, 0), memory_space=pltpu.SMEM),
                  pl.BlockSpec((None, 1, TOP_K * tc), lambda i: (jnp.minimum(i + 1, nt - 1), 0, 0),
                               memory_space=pltpu.SMEM),
                  pl.BlockSpec((tc * ROW_TILES, LANES), row), pl.BlockSpec(memory_space=pl.ANY),
                  pl.BlockSpec((tc, TOP_K), row), pl.BlockSpec((1, D_MODEL), lambda i: (0, 0))],
        out_specs=pl.BlockSpec((tc, D_MODEL), row),
        out_shape=jax.ShapeDtypeStruct((nt * tc, D_MODEL), _f32),
        scratch_shapes=[pltpu.VMEM((2, TOP_K, tc * ROW_TILES, LANES), _f32), pltpu.SemaphoreType.DMA((2,))],
        compiler_params=pltpu.CompilerParams(dimension_semantics=("arbitrary",),
                                             vmem_limit_bytes=VMEM_LIMIT),
        name="combine_norm",
    )(dest_tiles, dest_tiles, x1r, ys, gates_t, gf)


def _plan_blocks(route_i, counts, n_tokens):
    tm = EXPERT_TILE
    nblk = (n_tokens * TOP_K + N_EXPERTS * tm) // tm
    counts = counts.astype(jnp.int32)
    blocks_e = (counts + tm - 1) // tm
    blk_end = jnp.cumsum(blocks_e)
    pad_start = (blk_end - blocks_e) * tm
    experts = jnp.arange(N_EXPERTS, dtype=jnp.int32)
    hot = route_i[:TOP_K, :, None] == experts
    dest = jnp.sum(jnp.where(hot, pad_start, 0), axis=-1) + route_i[TOP_K:]
    n_used = blk_end[-1]
    blk = jnp.arange(nblk, dtype=jnp.int32)
    src = jnp.minimum(blk, jnp.maximum(n_used - 1, 0))
    be = jnp.minimum(jnp.sum((blk_end[None, :] <= src[:, None]).astype(jnp.int32), axis=1), N_EXPERTS - 1)
    hot_b = be[:, None] == experts
    pick = lambda table: jnp.sum(jnp.where(hot_b, table, 0), axis=1)
    n_valid = jnp.where(blk < n_used, jnp.clip(pick(counts) - (src * tm - pick(pad_start)), 0, tm), 0)
    used = blocks_e > 0
    grp_e = jnp.cumsum(used.astype(jnp.int32)) - 1
    later = jnp.where((experts[None, :] > experts[:, None]) & used[None, :], experts[None, :], N_EXPERTS)
    nxt_e = jnp.min(later, axis=1)
    nxt_e = jnp.where(nxt_e < N_EXPERTS, nxt_e, -1)
    return be, n_valid.astype(jnp.int32), src, pick(grp_e), pick(nxt_e), dest


def _split_bf16(w):
    hi = w.astype(_bf16)
    return jnp.stack([hi, (w - hi.astype(_f32)).astype(_bf16)])


def _tile_dest(dest, tile):
    k, t = dest.shape
    return dest.reshape(k, t // tile, tile).transpose(1, 0, 2).reshape(t // tile, 1, k * tile)


def kernel(x, positions, norm1_g, w_in, b_in, sinks, gn_dil, gn_swa, w_out, b_out,
           norm2_g, w_router, b_router, w_mlp1, b_mlp1, w_mlp2, b_mlp2, norm_f_g):
    batch, seq, d = x.shape
    T = batch * seq
    x2 = x.reshape(T, d)
    depth = norm1_g.shape[0]
    assert depth == 1, "the final norm is fused into the last layer's combine kernel"
    inv = ROPE_THETA ** (-jnp.arange(0, HEAD_DIM, 2, dtype=_f32) / HEAD_DIM)
    inv = jnp.tile(inv, LANES // (HEAD_DIM // 2)).reshape(1, LANES)
    pos = positions.reshape(T, 1)
    for layer in range(depth):
        qa, ka, va, qb, kb, vb = _inproj_call(
            x2, pos, inv, norm1_g[layer].reshape(1, d), w_in[layer].astype(_bf16), b_in[layer].reshape(1, D_IN))
        o_dil = _attn_call(qa, ka, va, None, batch=batch, seq=seq,
                           configs=tuple((w // r, r) for w, r in DILATED_CONFIGS), kv_group=1)
        sink_lanes = jnp.repeat(sinks[layer].astype(_f32), HEAD_DIM).reshape(1, W_SWA)
        o_swa = _attn_call(qb, kb, vb, sink_lanes, batch=batch, seq=seq,
                           configs=((SWA_WINDOW - 1, 1),), kv_group=N_HEADS_SWA // N_KV_SWA // PAIR)
        x1r, route_i, route_g, counts = _outproj_call(
            o_dil, o_swa, x2, gn_dil[layer].reshape(1, W_DIL), gn_swa[layer].reshape(1, W_SWA),
            w_out[layer].astype(_bf16), b_out[layer].reshape(1, d), norm2_g[layer].reshape(1, d),
            _split_bf16(w_router[layer].T), b_router[layer].reshape(N_EXPERTS, 1))
        block_e, n_valid, block_src, block_grp, next_e, dest = _plan_blocks(route_i, counts[:, 0], T)
        xs = _dispatch_call(_tile_dest(dest, DISPATCH_TILE), x1r, block_e.shape[0] * EXPERT_TILE)
        half = GLU_GROUP // 2
        b1p = b_mlp1[layer].reshape(N_EXPERTS, 2 * D_FF // GLU_GROUP, half, 2)
        b1p = jnp.swapaxes(b1p, 2, 3).reshape(N_EXPERTS, 1, 2 * D_FF)
        ys = _moe_call(block_e, n_valid, block_src, block_grp, next_e, xs, norm2_g[layer].reshape(1, d),
                       w_mlp1[layer], b1p, w_mlp2[layer], b_mlp2[layer].reshape(N_EXPERTS, 1, d))
        x2 = _combine_call(_tile_dest(dest, COMBINE_TILE), x1r, ys, route_g[:TOP_K].T, norm_f_g.reshape(1, d))
    return x2.reshape(batch, seq, d)
```

```python
import functools

import jax
import jax.numpy as jnp
from jax import lax
from jax.experimental import pallas as pl
from jax.experimental.pallas import tpu as pltpu

D_MODEL = 1024
HEAD_DIM = 64
N_HEADS_DIL = 8
DILATED_CONFIGS = ((128, 1), (512, 4), (2048, 16))
N_HEADS_SWA = 8
N_KV_SWA = 2
SWA_WINDOW = 128
ATTN_BLOCK = 128
ROPE_THETA = 10000.0
N_EXPERTS = 32
TOP_K = 4
D_FF = D_MODEL
SWIGLU_ALPHA = 1.702
SWIGLU_LIMIT = 7.0
NORM_EPS = 1e-5
LOG2E = 1.4426950408889634

W_DIL = N_HEADS_DIL * HEAD_DIM
W_SWA = N_HEADS_SWA * HEAD_DIM
W_KV_SWA = N_KV_SWA * HEAD_DIM
D_IN = 3 * W_DIL + W_SWA + 2 * W_KV_SWA

LANES = 128
PAIR = LANES // HEAD_DIM
SUBLANES = 8
ROW_TILES = D_MODEL // LANES
TOKEN_TILE = 1024
EXPERT_TILE = 1024
DISPATCH_TILE = 2048
COMBINE_TILE = 256
DMA_GROUP = 8
ATTN_UNROLL = 32
MOE_CHAINS = 4
INPROJ_TILE = 2048
INPROJ_CHAINS = 8
OUTPROJ_CHAINS = 2
GLU_GROUP = 256
VMEM_LIMIT = 56 * 1024 * 1024

_f32 = jnp.float32
_bf16 = jnp.bfloat16


def _rms(x, g):
    return x * lax.rsqrt(jnp.mean(x * x, axis=-1, keepdims=True) + NORM_EPS) * g


def _store_token_rows(ref, x):
    tm = x.shape[0]
    for c in range(ROW_TILES):
        ref[pl.ds(c, tm, stride=ROW_TILES), :] = x[:, c * LANES:(c + 1) * LANES]


def _load_token_rows(ref, tm):
    return jnp.concatenate([ref[pl.ds(c, tm, stride=ROW_TILES), :] for c in range(ROW_TILES)], axis=1)


def _token_row(ref, tok):
    return ref.at[pl.ds(pl.multiple_of(tok * ROW_TILES, ROW_TILES), ROW_TILES)]


def _inproj_kernel(x_ref, pos_ref, inv_ref, g_ref, w_ref, b_ref, *out_refs):
    sub = x_ref.shape[0] // INPROJ_CHAINS
    for s in range(INPROJ_CHAINS):
        rows = slice(s * sub, (s + 1) * sub)
        _inproj_rows(x_ref[rows, :], pos_ref[rows, :], inv_ref, g_ref, w_ref, b_ref,
                     [r.at[rows, :] for r in out_refs])


def _inproj_rows(x, pos, inv_ref, g_ref, w_ref, b_ref, out_refs):
    qa_ref, ka_ref, va_ref, qb_ref, kb_ref, vb_ref = out_refs
    h = _rms(x, g_ref[...]).astype(_bf16)
    proj = jnp.dot(h, w_ref[...], preferred_element_type=_f32) + b_ref[...]

    ang = pos.astype(_f32) * inv_ref[...]
    lane = lax.broadcasted_iota(jnp.int32, ang.shape, 1)
    first_half = (lane % HEAD_DIM) < (HEAD_DIM // 2)
    cos = jnp.cos(ang)
    sin = jnp.where(first_half, -jnp.sin(ang), jnp.sin(ang))

    def rope(t, scale):
        outs = []
        for c in range(t.shape[1] // LANES):
            tc = t[:, c * LANES:(c + 1) * LANES]
            partner = jnp.where(first_half, pltpu.roll(tc, LANES - HEAD_DIM // 2, 1),
                                pltpu.roll(tc, HEAD_DIM // 2, 1))
            outs.append((tc * cos + partner * sin) * scale)
        return jnp.concatenate(outs, axis=1) if len(outs) > 1 else outs[0]

    scale = HEAD_DIM ** -0.5 * LOG2E
    o = 0
    qa_ref[...] = rope(proj[:, o:o + W_DIL], scale).astype(_bf16); o += W_DIL
    ka_ref[...] = rope(proj[:, o:o + W_DIL], 1.0).astype(_bf16); o += W_DIL
    va_ref[...] = proj[:, o:o + W_DIL].astype(_bf16); o += W_DIL
    qb_ref[...] = rope(proj[:, o:o + W_SWA], scale).astype(_bf16); o += W_SWA
    kb = rope(proj[:, o:o + W_KV_SWA], 1.0); o += W_KV_SWA
    vb = proj[:, o:o + W_KV_SWA]

    def dup_heads(t):
        sw = pltpu.roll(t, HEAD_DIM, 1)
        lo = lane < HEAD_DIM
        return jnp.concatenate([jnp.where(lo, t, sw), jnp.where(lo, sw, t)], axis=1)

    kb_ref[...] = dup_heads(kb).astype(_bf16)
    vb_ref[...] = dup_heads(vb).astype(_bf16)


def _inproj_call(x2, pos, inv, g, w, b):
    T = x2.shape[0]
    tm = INPROJ_TILE
    row = lambda i: (i, 0)
    fix = lambda i: (0, 0)
    widths = (W_DIL, W_DIL, W_DIL, W_SWA, PAIR * W_KV_SWA, PAIR * W_KV_SWA)
    return pl.pallas_call(
        _inproj_kernel,
        grid=(T // tm,),
        in_specs=[pl.BlockSpec((tm, D_MODEL), row), pl.BlockSpec((tm, 1), row),
                  pl.BlockSpec((1, LANES), fix), pl.BlockSpec((1, D_MODEL), fix),
                  pl.BlockSpec((D_MODEL, D_IN), fix), pl.BlockSpec((1, D_IN), fix)],
        out_specs=[pl.BlockSpec((tm, wd), row) for wd in widths],
        out_shape=[jax.ShapeDtypeStruct((T, wd), _bf16) for wd in widths],
        compiler_params=pltpu.CompilerParams(dimension_semantics=("parallel",),
                                             vmem_limit_bytes=VMEM_LIMIT),
        name="inproj",
    )(x2, pos, inv, g, w, b)


def _attn_kernel(*refs, configs, seq, with_sink):
    if with_sink:
        q_ref, k_ref, v_ref, sink_ref, o_ref = refs[:5]
        scratch = refs[5:]
    else:
        q_ref, k_ref, v_ref, o_ref = refs[:4]
        scratch = refs[4:]
    n_cfg = len(configs)
    obufs = scratch[:n_cfg]
    mbufs = scratch[n_cfg:2 * n_cfg]
    lbufs = scratch[2 * n_cfg:3 * n_cfg]
    extra = list(scratch[3 * n_cfg:])
    blk = ATTN_BLOCK
    dilated = [r for _, r in configs if r > 1]
    if dilated:
        qf, kf, vf = extra[:3]
        qf[...] = q_ref[...].astype(_f32)
        kf[...] = k_ref[...].astype(_f32)
        vf[...] = v_ref[...].astype(_f32)
        extra = extra[3:]

    lane = lax.broadcasted_iota(jnp.int32, (blk, LANES), 1)
    head0 = lane < HEAD_DIM
    lane_row = lax.broadcasted_iota(jnp.int32, (1, LANES), 1)
    head_rows = [(lane_row < HEAD_DIM).astype(_bf16), (lane_row >= HEAD_DIM).astype(_bf16)]
    qi = lax.broadcasted_iota(jnp.int32, (blk, 2 * blk), 0)
    kj = lax.broadcasted_iota(jnp.int32, (blk, 2 * blk), 1)
    dist = qi - kj + blk

    base_perm = None
    for ci, (max_dist, r) in enumerate(configs):
        nb = seq // r // blk
        band = (dist >= 0) & (dist <= max_dist)
        obuf, mbuf, lbuf = obufs[ci], mbufs[ci], lbufs[ci]
        if r == 1:
            qp, kp, vp = q_ref, k_ref, v_ref
        else:
            perm_bf16 = extra[:3]
            extra = extra[3:]
            later = [r2 for _, r2 in configs[ci + 1:] if r2 % r == 0]
            perm_f32 = []
            if later:
                perm_f32 = extra[:3]
                extra = extra[3:]
            r_src, srcs = 1, (qf, kf, vf)
            if base_perm is not None and r % base_perm[0] == 0:
                r_src, srcs = base_perm
            if perm_f32:
                base_perm = (r, perm_f32)
            qp, kp, vp = perm_bf16
            piece = 2 * blk
            step = r // r_src
            per_res = seq // r // piece
            len_src = seq // r_src

            def permute(idx, carry, r_src=r_src, srcs=srcs, step=step, per_res=per_res, len_src=len_src,
                        perm_bf16=perm_bf16, perm_f32=perm_f32):
                c = idx // per_res
                start = (c % r_src) * len_src + c // r_src + (step * piece) * (idx % per_res)
                src = pl.ds(start, piece, stride=step)
                dst = pl.ds(pl.multiple_of(idx * piece, piece), piece)
                for t in range(3):
                    val = srcs[t][src, :]
                    perm_bf16[t][dst, :] = val.astype(_bf16)
                    if perm_f32:
                        perm_f32[t][dst, :] = val
                return carry

            lax.fori_loop(0, seq // piece, permute, 0, unroll=True)

        def body(n, carry, r=r, nb=nb, band=band, obuf=obuf, lbuf=lbuf, qp=qp, kp=kp, vp=vp):
            j = n % nb
            cur = pl.ds(pl.multiple_of(n * blk, blk), blk)
            prev = pl.ds(pl.multiple_of(jnp.maximum(n - 1, 0) * blk, blk), blk)
            q = qp[cur, :]
            k2 = jnp.concatenate([kp[prev, :], kp[cur, :]], axis=0)
            v2 = jnp.concatenate([vp[prev, :], vp[cur, :]], axis=0)
            mask = band & (kj >= jnp.where(j > 0, 0, blk))
            q2 = jnp.concatenate([q * hm for hm in head_rows], axis=0)
            s2 = lax.dot_general(q2, k2, (((1,), (1,)), ((), ())), preferred_element_type=_f32)
            ps, maxs, sums = [], [], []
            for h in range(PAIR):
                s = jnp.where(mask, s2[h * blk:(h + 1) * blk], -jnp.inf)
                m = jnp.max(s, axis=-1, keepdims=True)
                p = jnp.exp2(s - m)
                ps.append(p.astype(_bf16))
                maxs.append(jnp.broadcast_to(m, (blk, LANES)))
                sums.append(jnp.broadcast_to(jnp.sum(p, axis=-1, keepdims=True), (blk, LANES)))
            o2 = jnp.dot(jnp.concatenate(ps, axis=0), v2, preferred_element_type=_f32)
            outs = [o2[:blk], o2[blk:]]
            start = n // nb + (r * blk) * j
            dst = pl.ds(pl.multiple_of(start, blk), blk) if r == 1 else pl.ds(start, blk, stride=r)
            obuf[dst, :] = jnp.where(head0, outs[0], outs[1])
            mbuf[dst, :] = jnp.where(head0, maxs[0], maxs[1])
            lbuf[dst, :] = jnp.where(head0, sums[0], sums[1])
            return carry

        lax.fori_loop(0, seq // blk, body, 0, unroll=ATTN_UNROLL)

    chunk = 4 * blk

    def finish(i, carry):
        sl = pl.ds(pl.multiple_of(i * chunk, chunk), chunk)
        ms = [mb[sl, :] for mb in mbufs]
        if with_sink:
            ms.append(jnp.broadcast_to(sink_ref[...] * LOG2E, ms[0].shape))
        mx = functools.reduce(jnp.maximum, ms)
        es = [jnp.exp2(m - mx) for m in ms]
        num = functools.reduce(lambda a, b: a + b, [e * ob[sl, :] for e, ob in zip(es, obufs)])
        den = functools.reduce(lambda a, b: a + b, [e * lb[sl, :] for e, lb in zip(es, lbufs)])
        if with_sink:
            den = den + es[-1]
        o_ref[sl, :] = num / den
        return carry

    lax.fori_loop(0, seq // chunk, finish, 0)


def _attn_call(q, k, v, sinks, *, batch, seq, configs, kv_group):
    n_pairs = q.shape[1] // LANES
    with_sink = sinks is not None
    qmap = lambda b, p: (b, p)
    kvmap = lambda b, p: (b, p // kv_group)
    in_specs = [pl.BlockSpec((seq, LANES), qmap), pl.BlockSpec((seq, LANES), kvmap),
                pl.BlockSpec((seq, LANES), kvmap)]
    args = [q, k, v]
    if with_sink:
        in_specs.append(pl.BlockSpec((1, LANES), lambda b, p: (0, p)))
        args.append(sinks)
    buf = lambda dt: pltpu.VMEM((seq, LANES), dt)
    scratch = [buf(_f32) for _ in range(3 * len(configs))]
    if any(r > 1 for _, r in configs):
        scratch += [buf(_f32) for _ in range(3)]
    for ci, (_, r) in enumerate(configs):
        if r > 1:
            scratch += [buf(_bf16) for _ in range(3)]
            if any(r2 % r == 0 for _, r2 in configs[ci + 1:]):
                scratch += [buf(_f32) for _ in range(3)]
    return pl.pallas_call(
        functools.partial(_attn_kernel, configs=configs, seq=seq, with_sink=with_sink),
        grid=(batch, n_pairs),
        in_specs=in_specs,
        out_specs=pl.BlockSpec((seq, LANES), qmap),
        out_shape=jax.ShapeDtypeStruct(q.shape, _f32),
        scratch_shapes=scratch,
        compiler_params=pltpu.CompilerParams(dimension_semantics=("parallel", "parallel"),
                                             vmem_limit_bytes=VMEM_LIMIT),
        name="attn_sink" if with_sink else "attn_dilated",
    )(*args)


def _outproj_kernel(od_ref, os_ref, x_ref, gd_ref, gs_ref, w_ref, b_ref, g2_ref, wr_ref, br_ref,
                    x1_ref, ri_ref, rg_ref, cnt_ref, carry_ref, before_ref):
    tm = x_ref.shape[0]

    @pl.when(pl.program_id(0) == 0)
    def _():
        carry_ref[...] = jnp.zeros_like(carry_ref)
        ti = lax.broadcasted_iota(jnp.int32, (tm, tm), 0)
        tj = lax.broadcasted_iota(jnp.int32, (tm, tm), 1)
        before_ref[...] = (ti < tj).astype(_bf16)

    sub = tm // OUTPROJ_CHAINS
    h2 = []
    for s in range(OUTPROJ_CHAINS):
        rows = slice(s * sub, (s + 1) * sub)
        mixed = jnp.concatenate([_rms(od_ref[rows, :], gd_ref[...]), _rms(os_ref[rows, :], gs_ref[...])], axis=1)
        x1 = x_ref[rows, :] + jnp.dot(mixed.astype(_bf16), w_ref[...], preferred_element_type=_f32) + b_ref[...]
        _store_token_rows(x1_ref.at[pl.ds(s * sub * ROW_TILES, sub * ROW_TILES)], x1)
        h2.append(_rms(x1, g2_ref[...]))
    h2 = jnp.concatenate(h2, axis=0)
    h_hi = h2.astype(_bf16)
    h_lo = (h2 - h_hi.astype(_f32)).astype(_bf16)
    nt_dot = lambda a, b: lax.dot_general(a, b, (((1,), (1,)), ((), ())), preferred_element_type=_f32)
    logits = (nt_dot(wr_ref[0], h_hi) + nt_dot(wr_ref[0], h_lo) + nt_dot(wr_ref[1], h_hi)) + br_ref[...]
    eiota = lax.broadcasted_iota(jnp.int32, logits.shape, 0).astype(_f32)
    vals, idxs, sels = [], [], []
    l = logits
    for _ in range(TOP_K):
        m = jnp.max(l, axis=0, keepdims=True)
        idx = jnp.min(jnp.where(l == m, eiota, float(N_EXPERTS)), axis=0, keepdims=True)
        sel = eiota == idx
        vals.append(m); idxs.append(idx); sels.append(sel)
        l = jnp.where(sel, -jnp.inf, l)
    exps = [jnp.exp(v - vals[0]) for v in vals]
    den = functools.reduce(lambda a, b: a + b, exps)
    gates = [e / den for e in exps]

    chosen = functools.reduce(jnp.logical_or, sels).astype(_f32)
    prefix = jnp.dot(chosen.astype(_bf16), before_ref[...], preferred_element_type=_f32) + carry_ref[...]
    ranks = [jnp.sum(jnp.where(s, prefix, 0.0), axis=0, keepdims=True).astype(jnp.int32) for s in sels]
    carry_ref[...] = carry_ref[...] + jnp.sum(chosen, axis=1, keepdims=True)

    ri_ref[...] = jnp.concatenate([ix.astype(jnp.int32) for ix in idxs] + ranks, axis=0)
    rg_ref[...] = jnp.concatenate(gates + gates, axis=0)
    cnt_ref[...] = jnp.broadcast_to(carry_ref[...], cnt_ref.shape)


def _outproj_call(od, osw, x2, gd, gs, w, b, g2, wr_t, br):
    T = x2.shape[0]
    tm = TOKEN_TILE
    row = lambda i: (i, 0)
    col = lambda i: (0, i)
    fix = lambda i: (0, 0)
    return pl.pallas_call(
        _outproj_kernel,
        grid=(T // tm,),
        in_specs=[pl.BlockSpec((tm, W_DIL), row), pl.BlockSpec((tm, W_SWA), row),
                  pl.BlockSpec((tm, D_MODEL), row),
                  pl.BlockSpec((1, W_DIL), fix), pl.BlockSpec((1, W_SWA), fix),
                  pl.BlockSpec((D_MODEL, D_MODEL), fix), pl.BlockSpec((1, D_MODEL), fix),
                  pl.BlockSpec((1, D_MODEL), fix),
                  pl.BlockSpec((2, N_EXPERTS, D_MODEL), lambda i: (0, 0, 0)), pl.BlockSpec((N_EXPERTS, 1), fix)],
        out_specs=[pl.BlockSpec((tm * ROW_TILES, LANES), row), pl.BlockSpec((2 * TOP_K, tm), col),
                   pl.BlockSpec((2 * TOP_K, tm), col), pl.BlockSpec((N_EXPERTS, LANES), fix)],
        out_shape=[jax.ShapeDtypeStruct((T * ROW_TILES, LANES), _f32),
                   jax.ShapeDtypeStruct((2 * TOP_K, T), jnp.int32),
                   jax.ShapeDtypeStruct((2 * TOP_K, T), _f32),
                   jax.ShapeDtypeStruct((N_EXPERTS, LANES), _f32)],
        scratch_shapes=[pltpu.VMEM((N_EXPERTS, 1), _f32), pltpu.VMEM((tm, tm), _bf16)],
        compiler_params=pltpu.CompilerParams(dimension_semantics=("arbitrary",),
                                             vmem_limit_bytes=VMEM_LIMIT),
        name="outproj_router",
    )(od, osw, x2, gd, gs, w, b, g2, wr_t, br)


def _dispatch_kernel(d_ref, x1_ref, xs_hbm, sem):
    td = DISPATCH_TILE

    def start_group(g, c):
        for rr in range(DMA_GROUP):
            r = g * DMA_GROUP + rr
            src = _token_row(x1_ref, r)
            for k in range(TOP_K):
                pltpu.make_async_copy(src, _token_row(xs_hbm, d_ref[0, k * td + r]), sem).start(priority=k % 2)
        return c

    lax.fori_loop(0, td // DMA_GROUP, start_group, 0)

    def wait_group(g, c):
        for _ in range(DMA_GROUP * TOP_K):
            pltpu.make_async_copy(_token_row(x1_ref, 0), _token_row(xs_hbm, 0), sem).wait()
        return c

    lax.fori_loop(0, td // DMA_GROUP, wait_group, 0)


def _dispatch_call(dest_tiles, x1r, n_slots):
    nt = dest_tiles.shape[0]
    return pl.pallas_call(
        _dispatch_kernel,
        grid=(nt,),
        in_specs=[pl.BlockSpec((None, 1, TOP_K * DISPATCH_TILE), lambda i: (i, 0, 0), memory_space=pltpu.SMEM),
                  pl.BlockSpec((DISPATCH_TILE * ROW_TILES, LANES), lambda i: (i, 0))],
        out_specs=pl.BlockSpec(memory_space=pl.ANY),
        out_shape=jax.ShapeDtypeStruct((n_slots * ROW_TILES, LANES), _f32),
        scratch_shapes=[pltpu.SemaphoreType.DMA(())],
        compiler_params=pltpu.CompilerParams(dimension_semantics=("arbitrary",),
                                             vmem_limit_bytes=VMEM_LIMIT),
        name="dispatch_rows",
    )(dest_tiles, x1r)


def _moe_kernel(be_ref, nv_ref, src_ref, grp_ref, nxt_ref, xs_ref, g2_ref, w1_hbm, b1_ref, w2_hbm, b2_ref,
                ys_ref, st1, st2, w1p, w2b, sem):
    i = pl.program_id(0)
    nv = nv_ref[i]
    tm = EXPERT_TILE

    def weight_copies(e, slot):
        return (pltpu.make_async_copy(w1_hbm.at[e], st1.at[slot], sem.at[0, slot]),
                pltpu.make_async_copy(w2_hbm.at[e], st2.at[slot], sem.at[1, slot]))

    @pl.when(nv > 0)
    def _():
        first_of_expert = jnp.logical_or(i == 0, be_ref[i] != be_ref[jnp.maximum(i - 1, 0)])

        @pl.when(first_of_expert)
        def _():
            slot = grp_ref[i] % 2

            @pl.when(i == 0)
            def _():
                for c in weight_copies(be_ref[0], 0):
                    c.start()

            for c in weight_copies(be_ref[i], slot):
                c.wait()

            @pl.when(nxt_ref[i] >= 0)
            def _():
                for c in weight_copies(nxt_ref[i], 1 - slot):
                    c.start()

            src = lax.broadcasted_iota(jnp.int32, (GLU_GROUP, GLU_GROUP), 0)
            dst = lax.broadcasted_iota(jnp.int32, (GLU_GROUP, GLU_GROUP), 1)
            half = GLU_GROUP // 2
            perm = (src == jnp.where(dst < half, 2 * dst, 2 * (dst - half) + 1)).astype(_bf16)
            for g in range(2 * D_FF // GLU_GROUP):
                cols = slice(g * GLU_GROUP, (g + 1) * GLU_GROUP)
                w1p[:, cols] = jnp.dot(st1[slot, :, cols].astype(_bf16), perm,
                                       preferred_element_type=_f32).astype(_bf16)
            w2b[...] = st2[slot].astype(_bf16)

        hm = tm // MOE_CHAINS

        def sub_block(s):
            rows = lambda c: pl.ds(s * hm * ROW_TILES + c, hm, stride=ROW_TILES)
            x = jnp.concatenate([xs_ref[rows(c), :] for c in range(ROW_TILES)], axis=1)
            x = jnp.where(lax.broadcasted_iota(jnp.int32, (hm, 1), 0) < nv - s * hm, x, 0.0)
            h = _rms(x, g2_ref[...]).astype(_bf16)
            u = jnp.dot(h, w1p[...], preferred_element_type=_f32) + b1_ref[0]
            half = GLU_GROUP // 2
            hid = []
            for g in range(2 * D_FF // GLU_GROUP):
                glu = jnp.minimum(u[:, g * GLU_GROUP:g * GLU_GROUP + half], SWIGLU_LIMIT)
                lin = jnp.clip(u[:, g * GLU_GROUP + half:(g + 1) * GLU_GROUP], -SWIGLU_LIMIT, SWIGLU_LIMIT)
                hid.append(glu * (1.0 / (1.0 + jnp.exp(-SWIGLU_ALPHA * glu))) * (lin + 1.0))
            hid = jnp.concatenate(hid, axis=1).astype(_bf16)
            y = jnp.dot(hid, w2b[...], preferred_element_type=_f32) + b2_ref[0]
            for c in range(ROW_TILES):
                ys_ref[rows(c), :] = y[:, c * LANES:(c + 1) * LANES]

        for n_sub in range(1, MOE_CHAINS + 1):
            @pl.when(jnp.logical_and(nv > (n_sub - 1) * hm, nv <= n_sub * hm))
            def _(n_sub=n_sub):
                for s in range(n_sub):
                    sub_block(s)


def _moe_call(block_e, n_valid, block_src, block_grp, next_e, xs, g2, w1, b1p, w2, b2):
    nblk = block_e.shape[0]
    tm = EXPERT_TILE
    by_e = lambda i, be, nv, src, grp, nxt: (be[i], 0, 0)
    by_src = lambda i, be, nv, src, grp, nxt: (src[i], 0)
    fix = lambda i, be, nv, src, grp, nxt: (0, 0)
    grid_spec = pltpu.PrefetchScalarGridSpec(
        num_scalar_prefetch=5,
        grid=(nblk,),
        in_specs=[pl.BlockSpec((tm * ROW_TILES, LANES), by_src), pl.BlockSpec((1, D_MODEL), fix),
                  pl.BlockSpec(memory_space=pl.ANY), pl.BlockSpec((1, 1, 2 * D_FF), by_e),
                  pl.BlockSpec(memory_space=pl.ANY), pl.BlockSpec((1, 1, D_MODEL), by_e)],
        out_specs=pl.BlockSpec((tm * ROW_TILES, LANES), by_src),
        scratch_shapes=[pltpu.VMEM((2, D_MODEL, 2 * D_FF), _f32), pltpu.VMEM((2, D_FF, D_MODEL), _f32),
                        pltpu.VMEM((D_MODEL, 2 * D_FF), _bf16), pltpu.VMEM((D_FF, D_MODEL), _bf16),
                        pltpu.SemaphoreType.DMA((2, 2))],
    )
    return pl.pallas_call(
        _moe_kernel,
        grid_spec=grid_spec,
        out_shape=jax.ShapeDtypeStruct(xs.shape, _f32),
        compiler_params=pltpu.CompilerParams(dimension_semantics=("arbitrary",),
                                             vmem_limit_bytes=VMEM_LIMIT),
        name="moe_experts",
    )(block_e, n_valid, block_src, block_grp, next_e, xs, g2, w1, b1p, w2, b2)


def _combine_kernel(dcur_ref, dnext_ref, x1_ref, ys_hbm, g_ref, gf_ref, o_ref, gbuf, sem):
    tc = COMBINE_TILE
    i = pl.program_id(0)
    slot = i % 2

    def row_copy(d_ref, s, r, k):
        dst = gbuf.at[s, k, pl.ds(r * ROW_TILES, ROW_TILES)]
        return pltpu.make_async_copy(_token_row(ys_hbm, d_ref[0, k * tc + r]), dst, sem.at[s])

    def issue(d_ref, s):
        def group(g, c):
            for rr in range(DMA_GROUP):
                for k in range(TOP_K):
                    row_copy(d_ref, s, g * DMA_GROUP + rr, k).start(priority=k % 2)
            return c
        lax.fori_loop(0, tc // DMA_GROUP, group, 0)

    @pl.when(i == 0)
    def _():
        issue(dcur_ref, 0)

    @pl.when(i + 1 < pl.num_programs(0))
    def _():
        issue(dnext_ref, 1 - slot)

    def wait_group(g, c):
        for _ in range(DMA_GROUP * TOP_K):
            row_copy(dcur_ref, slot, 0, 0).wait()
        return c

    lax.fori_loop(0, tc // DMA_GROUP, wait_group, 0)

    g = g_ref[...]
    acc = _load_token_rows(x1_ref, tc)
    for k in range(TOP_K):
        acc = acc + g[:, k:k + 1] * _load_token_rows(gbuf.at[slot, k], tc)
    o_ref[...] = _rms(acc, gf_ref[...])


def _combine_call(dest_tiles, x1r, ys, gates_t, gf):
    nt = dest_tiles.shape[0]
    tc = COMBINE_TILE
    row = lambda i: (i, 0)
    return pl.pallas_call(
        _combine_kernel,
        grid=(nt,),
        in_specs=[pl.BlockSpec((None, 1, TOP_K * tc), lambda i: (i, 0, 0), memory_space=pltpu.SMEM),
                  pl.BlockSpec((None, 1, TOP_K * tc), lambda i: (jnp.minimum(i + 1, nt - 1), 0, 0),
                               memory_space=pltpu.SMEM),
                  pl.BlockSpec((tc * ROW_TILES, LANES), row), pl.BlockSpec(memory_space=pl.ANY),
                  pl.BlockSpec((tc, TOP_K), row), pl.BlockSpec((1, D_MODEL), lambda i: (0, 0))],
        out_specs=pl.BlockSpec((tc, D_MODEL), row),
        out_shape=jax.ShapeDtypeStruct((nt * tc, D_MODEL), _f32),
        scratch_shapes=[pltpu.VMEM((2, TOP_K, tc * ROW_TILES, LANES), _f32), pltpu.SemaphoreType.DMA((2,))],
        compiler_params=pltpu.CompilerParams(dimension_semantics=("arbitrary",),
                                             vmem_limit_bytes=VMEM_LIMIT),
        name="combine_norm",
    )(dest_tiles, dest_tiles, x1r, ys, gates_t, gf)


def _plan_blocks(route_i, counts, n_tokens):
    tm = EXPERT_TILE
    nblk = (n_tokens * TOP_K + N_EXPERTS * tm) // tm
    counts = counts.astype(jnp.int32)
    blocks_e = (counts + tm - 1) // tm
    blk_end = jnp.cumsum(blocks_e)
    pad_start = (blk_end - blocks_e) * tm
    experts = jnp.arange(N_EXPERTS, dtype=jnp.int32)
    hot = route_i[:TOP_K, :, None] == experts
    dest = jnp.sum(jnp.where(hot, pad_start, 0), axis=-1) + route_i[TOP_K:]
    n_used = blk_end[-1]
    blk = jnp.arange(nblk, dtype=jnp.int32)
    src = jnp.minimum(blk, jnp.maximum(n_used - 1, 0))
    be = jnp.minimum(jnp.sum((blk_end[None, :] <= src[:, None]).astype(jnp.int32), axis=1), N_EXPERTS - 1)
    hot_b = be[:, None] == experts
    pick = lambda table: jnp.sum(jnp.where(hot_b, table, 0), axis=1)
    n_valid = jnp.where(blk < n_used, jnp.clip(pick(counts) - (src * tm - pick(pad_start)), 0, tm), 0)
    used = blocks_e > 0
    grp_e = jnp.cumsum(used.astype(jnp.int32)) - 1
    later = jnp.where((experts[None, :] > experts[:, None]) & used[None, :], experts[None, :], N_EXPERTS)
    nxt_e = jnp.min(later, axis=1)
    nxt_e = jnp.where(nxt_e < N_EXPERTS, nxt_e, -1)
    return be, n_valid.astype(jnp.int32), src, pick(grp_e), pick(nxt_e), dest


def _split_bf16(w):
    hi = w.astype(_bf16)
    return jnp.stack([hi, (w - hi.astype(_f32)).astype(_bf16)])


def _tile_dest(dest, tile):
    k, t = dest.shape
    return dest.reshape(k, t // tile, tile).transpose(1, 0, 2).reshape(t // tile, 1, k * tile)


def kernel(x, positions, norm1_g, w_in, b_in, sinks, gn_dil, gn_swa, w_out, b_out,
           norm2_g, w_router, b_router, w_mlp1, b_mlp1, w_mlp2, b_mlp2, norm_f_g):
    batch, seq, d = x.shape
    T = batch * seq
    x2 = x.reshape(T, d)
    depth = norm1_g.shape[0]
    assert depth == 1, "the final norm is fused into the last layer's combine kernel"
    inv = ROPE_THETA ** (-jnp.arange(0, HEAD_DIM, 2, dtype=_f32) / HEAD_DIM)
    inv = jnp.tile(inv, LANES // (HEAD_DIM // 2)).reshape(1, LANES)
    pos = positions.reshape(T, 1)
    for layer in range(depth):
        qa, ka, va, qb, kb, vb = _inproj_call(
            x2, pos, inv, norm1_g[layer].reshape(1, d), w_in[layer].astype(_bf16), b_in[layer].reshape(1, D_IN))
        o_dil = _attn_call(qa, ka, va, None, batch=batch, seq=seq,
                           configs=tuple((w // r, r) for w, r in DILATED_CONFIGS), kv_group=1)
        sink_lanes = jnp.repeat(sinks[layer].astype(_f32), HEAD_DIM).reshape(1, W_SWA)
        o_swa = _attn_call(qb, kb, vb, sink_lanes, batch=batch, seq=seq,
                           configs=((SWA_WINDOW - 1, 1),), kv_group=N_HEADS_SWA // N_KV_SWA // PAIR)
        x1r, route_i, route_g, counts = _outproj_call(
            o_dil, o_swa, x2, gn_dil[layer].reshape(1, W_DIL), gn_swa[layer].reshape(1, W_SWA),
            w_out[layer].astype(_bf16), b_out[layer].reshape(1, d), norm2_g[layer].reshape(1, d),
            _split_bf16(w_router[layer].T), b_router[layer].reshape(N_EXPERTS, 1))
        block_e, n_valid, block_src, block_grp, next_e, dest = _plan_blocks(route_i, counts[:, 0], T)
        xs = _dispatch_call(_tile_dest(dest, DISPATCH_TILE), x1r, block_e.shape[0] * EXPERT_TILE)
        half = GLU_GROUP // 2
        b1p = b_mlp1[layer].reshape(N_EXPERTS, 2 * D_FF // GLU_GROUP, half, 2)
        b1p = jnp.swapaxes(b1p, 2, 3).reshape(N_EXPERTS, 1, 2 * D_FF)
        ys = _moe_call(block_e, n_valid, block_src, block_grp, next_e, xs, norm2_g[layer].reshape(1, d),
                       w_mlp1[layer], b1p, w_mlp2[layer], b_mlp2[layer].reshape(N_EXPERTS, 1, d))
        x2 = _combine_call(_tile_dest(dest, COMBINE_TILE), x1r, ys, route_g[:TOP_K].T, norm_f_g.reshape(1, d))
    return x2.reshape(batch, seq, d)
```

```python
import functools

import jax
import jax.numpy as jnp
from jax import lax
from jax.experimental import pallas as pl
from jax.experimental.pallas import tpu as pltpu

D_MODEL = 1024
HEAD_DIM = 64
N_HEADS_DIL = 8
DILATED_CONFIGS = ((128, 1), (512, 4), (2048, 16))
N_HEADS_SWA = 8
N_KV_SWA = 2
SWA_WINDOW = 128
ATTN_BLOCK = 128
ROPE_THETA = 10000.0
N_EXPERTS = 32
TOP_K = 4
D_FF = D_MODEL
SWIGLU_ALPHA = 1.702
SWIGLU_LIMIT = 7.0
NORM_EPS = 1e-5
LOG2E = 1.4426950408889634

W_DIL = N_HEADS_DIL * HEAD_DIM
W_SWA = N_HEADS_SWA * HEAD_DIM
W_KV_SWA = N_KV_SWA * HEAD_DIM
D_IN = 3 * W_DIL + W_SWA + 2 * W_KV_SWA

LANES = 128
PAIR = LANES // HEAD_DIM
SUBLANES = 8
ROW_TILES = D_MODEL // LANES
TOKEN_TILE = 1024
EXPERT_TILE = 1024
DISPATCH_TILE = 2048
COMBINE_TILE = 256
DMA_GROUP = 8
ATTN_UNROLL = 32
MOE_CHAINS = 4
INPROJ_TILE = 2048
INPROJ_CHAINS = 8
OUTPROJ_CHAINS = 2
GLU_GROUP = 256
VMEM_LIMIT = 56 * 1024 * 1024

_f32 = jnp.float32
_bf16 = jnp.bfloat16


def _rms(x, g):
    return x * lax.rsqrt(jnp.mean(x * x, axis=-1, keepdims=True) + NORM_EPS) * g


def _store_token_rows(ref, x):
    tm = x.shape[0]
    for c in range(ROW_TILES):
        ref[pl.ds(c, tm, stride=ROW_TILES), :] = x[:, c * LANES:(c + 1) * LANES]


def _load_token_rows(ref, tm):
    return jnp.concatenate([ref[pl.ds(c, tm, stride=ROW_TILES), :] for c in range(ROW_TILES)], axis=1)


def _token_row(ref, tok):
    return ref.at[pl.ds(pl.multiple_of(tok * ROW_TILES, ROW_TILES), ROW_TILES)]


def _inproj_kernel(x_ref, pos_ref, inv_ref, g_ref, w_ref, b_ref, *out_refs):
    sub = x_ref.shape[0] // INPROJ_CHAINS
    for s in range(INPROJ_CHAINS):
        rows = slice(s * sub, (s + 1) * sub)
        _inproj_rows(x_ref[rows, :], pos_ref[rows, :], inv_ref, g_ref, w_ref, b_ref,
                     [r.at[rows, :] for r in out_refs])


def _inproj_rows(x, pos, inv_ref, g_ref, w_ref, b_ref, out_refs):
    qa_ref, ka_ref, va_ref, qb_ref, kb_ref, vb_ref = out_refs
    h = _rms(x, g_ref[...]).astype(_bf16)
    proj = jnp.dot(h, w_ref[...], preferred_element_type=_f32) + b_ref[...]

    ang = pos.astype(_f32) * inv_ref[...]
    lane = lax.broadcasted_iota(jnp.int32, ang.shape, 1)
    first_half = (lane % HEAD_DIM) < (HEAD_DIM // 2)
    cos = jnp.cos(ang)
    sin = jnp.where(first_half, -jnp.sin(ang), jnp.sin(ang))

    def rope(t, scale):
        outs = []
        for c in range(t.shape[1] // LANES):
            tc = t[:, c * LANES:(c + 1) * LANES]
            partner = jnp.where(first_half, pltpu.roll(tc, LANES - HEAD_DIM // 2, 1),
                                pltpu.roll(tc, HEAD_DIM // 2, 1))
            outs.append((tc * cos + partner * sin) * scale)
        return jnp.concatenate(outs, axis=1) if len(outs) > 1 else outs[0]

    scale = HEAD_DIM ** -0.5 * LOG2E
    o = 0
    qa_ref[...] = rope(proj[:, o:o + W_DIL], scale).astype(_bf16); o += W_DIL
    ka_ref[...] = rope(proj[:, o:o + W_DIL], 1.0).astype(_bf16); o += W_DIL
    va_ref[...] = proj[:, o:o + W_DIL].astype(_bf16); o += W_DIL
    qb_ref[...] = rope(proj[:, o:o + W_SWA], scale).astype(_bf16); o += W_SWA
    kb = rope(proj[:, o:o + W_KV_SWA], 1.0); o += W_KV_SWA
    vb = proj[:, o:o + W_KV_SWA]

    def dup_heads(t):
        sw = pltpu.roll(t, HEAD_DIM, 1)
        lo = lane < HEAD_DIM
        return jnp.concatenate([jnp.where(lo, t, sw), jnp.where(lo, sw, t)], axis=1)

    kb_ref[...] = dup_heads(kb).astype(_bf16)
    vb_ref[...] = dup_heads(vb).astype(_bf16)


def _inproj_call(x2, pos, inv, g, w, b):
    T = x2.shape[0]
    tm = INPROJ_TILE
    row = lambda i: (i, 0)
    fix = lambda i: (0, 0)
    widths = (W_DIL, W_DIL, W_DIL, W_SWA, PAIR * W_KV_SWA, PAIR * W_KV_SWA)
    return pl.pallas_call(
        _inproj_kernel,
        grid=(T // tm,),
        in_specs=[pl.BlockSpec((tm, D_MODEL), row), pl.BlockSpec((tm, 1), row),
                  pl.BlockSpec((1, LANES), fix), pl.BlockSpec((1, D_MODEL), fix),
                  pl.BlockSpec((D_MODEL, D_IN), fix), pl.BlockSpec((1, D_IN), fix)],
        out_specs=[pl.BlockSpec((tm, wd), row) for wd in widths],
        out_shape=[jax.ShapeDtypeStruct((T, wd), _bf16) for wd in widths],
        compiler_params=pltpu.CompilerParams(dimension_semantics=("parallel",),
                                             vmem_limit_bytes=VMEM_LIMIT),
        name="inproj",
    )(x2, pos, inv, g, w, b)


def _attn_kernel(*refs, configs, seq, with_sink):
    if with_sink:
        q_ref, k_ref, v_ref, sink_ref, o_ref = refs[:5]
        scratch = refs[5:]
    else:
        q_ref, k_ref, v_ref, o_ref = refs[:4]
        scratch = refs[4:]
    n_cfg = len(configs)
    n_acc = n_cfg if n_cfg > 1 else 0
    obufs = scratch[:n_acc]
    mbufs = scratch[n_acc:2 * n_acc]
    lbufs = scratch[2 * n_acc:3 * n_acc]
    extra = list(scratch[3 * n_acc:])
    blk = ATTN_BLOCK
    dilated = [r for _, r in configs if r > 1]
    if dilated:
        qf, kf, vf = extra[:3]
        qf[...] = q_ref[...].astype(_f32)
        kf[...] = k_ref[...].astype(_f32)
        vf[...] = v_ref[...].astype(_f32)
        extra = extra[3:]

    lane = lax.broadcasted_iota(jnp.int32, (blk, LANES), 1)
    head0 = lane < HEAD_DIM
    lane_row = lax.broadcasted_iota(jnp.int32, (1, LANES), 1)
    head_rows = [(lane_row < HEAD_DIM).astype(_bf16), (lane_row >= HEAD_DIM).astype(_bf16)]
    qi = lax.broadcasted_iota(jnp.int32, (blk, 2 * blk), 0)
    kj = lax.broadcasted_iota(jnp.int32, (blk, 2 * blk), 1)
    dist = qi - kj + blk

    def merge(os_, ms_, ls_):
        ms_ = list(ms_)
        if with_sink:
            ms_.append(jnp.broadcast_to(sink_ref[...] * LOG2E, ms_[0].shape))
        mx = functools.reduce(jnp.maximum, ms_)
        es = [jnp.exp2(m - mx) for m in ms_]
        num = functools.reduce(lambda a, b: a + b, [e * o for e, o in zip(es, os_)])
        den = functools.reduce(lambda a, b: a + b, [e * l for e, l in zip(es, ls_)])
        if with_sink:
            den = den + es[-1]
        return num / den

    base_perm = None
    for ci, (max_dist, r) in enumerate(configs):
        nb = seq // r // blk
        band = (dist >= 0) & (dist <= max_dist)
        obuf, mbuf, lbuf = (obufs[ci], mbufs[ci], lbufs[ci]) if n_cfg > 1 else (None, None, None)
        if r == 1:
            qp, kp, vp = q_ref, k_ref, v_ref
        else:
            perm_bf16 = extra[:3]
            extra = extra[3:]
            later = [r2 for _, r2 in configs[ci + 1:] if r2 % r == 0]
            perm_f32 = []
            if later:
                perm_f32 = extra[:3]
                extra = extra[3:]
            r_src, srcs = 1, (qf, kf, vf)
            if base_perm is not None and r % base_perm[0] == 0:
                r_src, srcs = base_perm
            if perm_f32:
                base_perm = (r, perm_f32)
            qp, kp, vp = perm_bf16
            piece = 2 * blk
            step = r // r_src
            per_res = seq // r // piece
            len_src = seq // r_src

            def permute(idx, carry, r_src=r_src, srcs=srcs, step=step, per_res=per_res, len_src=len_src,
                        perm_bf16=perm_bf16, perm_f32=perm_f32):
                c = idx // per_res
                start = (c % r_src) * len_src + c // r_src + (step * piece) * (idx % per_res)
                src = pl.ds(start, piece, stride=step)
                dst = pl.ds(pl.multiple_of(idx * piece, piece), piece)
                for t in range(3):
                    val = srcs[t][src, :]
                    perm_bf16[t][dst, :] = val.astype(_bf16)
                    if perm_f32:
                        perm_f32[t][dst, :] = val
                return carry

            lax.fori_loop(0, seq // piece, permute, 0, unroll=True)

        def body(n, carry, r=r, nb=nb, band=band, obuf=obuf, lbuf=lbuf, qp=qp, kp=kp, vp=vp):
            j = n % nb
            cur = pl.ds(pl.multiple_of(n * blk, blk), blk)
            prev = pl.ds(pl.multiple_of(jnp.maximum(n - 1, 0) * blk, blk), blk)
            q = qp[cur, :]
            k2 = jnp.concatenate([kp[prev, :], kp[cur, :]], axis=0)
            v2 = jnp.concatenate([vp[prev, :], vp[cur, :]], axis=0)
            mask = band & (kj >= jnp.where(j > 0, 0, blk))
            q2 = jnp.concatenate([q * hm for hm in head_rows], axis=0)
            s2 = lax.dot_general(q2, k2, (((1,), (1,)), ((), ())), preferred_element_type=_f32)
            ps, maxs, sums = [], [], []
            for h in range(PAIR):
                s = jnp.where(mask, s2[h * blk:(h + 1) * blk], -jnp.inf)
                m = jnp.max(s, axis=-1, keepdims=True)
                p = jnp.exp2(s - m)
                ps.append(p.astype(_bf16))
                maxs.append(jnp.broadcast_to(m, (blk, LANES)))
                sums.append(jnp.broadcast_to(jnp.sum(p, axis=-1, keepdims=True), (blk, LANES)))
            o2 = jnp.dot(jnp.concatenate(ps, axis=0), v2, preferred_element_type=_f32)
            outs = [o2[:blk], o2[blk:]]
            start = n // nb + (r * blk) * j
            dst = pl.ds(pl.multiple_of(start, blk), blk) if r == 1 else pl.ds(start, blk, stride=r)
            o_pair = jnp.where(head0, outs[0], outs[1])
            m_pair = jnp.where(head0, maxs[0], maxs[1])
            l_pair = jnp.where(head0, sums[0], sums[1])
            if n_cfg == 1:
                o_ref[dst, :] = merge([o_pair], [m_pair], [l_pair])
            else:
                obuf[dst, :] = o_pair
                mbuf[dst, :] = m_pair
                lbuf[dst, :] = l_pair
            return carry

        lax.fori_loop(0, seq // blk, body, 0, unroll=ATTN_UNROLL)

    if n_cfg > 1:
        chunk = 4 * blk

        def finish(i, carry):
            sl = pl.ds(pl.multiple_of(i * chunk, chunk), chunk)
            o_ref[sl, :] = merge([b[sl, :] for b in obufs], [b[sl, :] for b in mbufs], [b[sl, :] for b in lbufs])
            return carry

        lax.fori_loop(0, seq // chunk, finish, 0)


def _attn_call(q, k, v, sinks, *, batch, seq, configs, kv_group):
    n_pairs = q.shape[1] // LANES
    with_sink = sinks is not None
    qmap = lambda b, p: (b, p)
    kvmap = lambda b, p: (b, p // kv_group)
    in_specs = [pl.BlockSpec((seq, LANES), qmap), pl.BlockSpec((seq, LANES), kvmap),
                pl.BlockSpec((seq, LANES), kvmap)]
    args = [q, k, v]
    if with_sink:
        in_specs.append(pl.BlockSpec((1, LANES), lambda b, p: (0, p)))
        args.append(sinks)
    buf = lambda dt: pltpu.VMEM((seq, LANES), dt)
    scratch = [buf(_f32) for _ in range(3 * len(configs) if len(configs) > 1 else 0)]
    if any(r > 1 for _, r in configs):
        scratch += [buf(_f32) for _ in range(3)]
    for ci, (_, r) in enumerate(configs):
        if r > 1:
            scratch += [buf(_bf16) for _ in range(3)]
            if any(r2 % r == 0 for _, r2 in configs[ci + 1:]):
                scratch += [buf(_f32) for _ in range(3)]
    return pl.pallas_call(
        functools.partial(_attn_kernel, configs=configs, seq=seq, with_sink=with_sink),
        grid=(batch, n_pairs),
        in_specs=in_specs,
        out_specs=pl.BlockSpec((seq, LANES), qmap),
        out_shape=jax.ShapeDtypeStruct(q.shape, _f32),
        scratch_shapes=scratch,
        compiler_params=pltpu.CompilerParams(dimension_semantics=("parallel", "parallel"),
                                             vmem_limit_bytes=VMEM_LIMIT),
        name="attn_sink" if with_sink else "attn_dilated",
    )(*args)


def _outproj_kernel(od_ref, os_ref, x_ref, gd_ref, gs_ref, w_ref, b_ref, g2_ref, wr_ref, br_ref,
                    x1_ref, ri_ref, rg_ref, cnt_ref, carry_ref, before_ref):
    tm = x_ref.shape[0]

    @pl.when(pl.program_id(0) == 0)
    def _():
        carry_ref[...] = jnp.zeros_like(carry_ref)
        ti = lax.broadcasted_iota(jnp.int32, (tm, tm), 0)
        tj = lax.broadcasted_iota(jnp.int32, (tm, tm), 1)
        before_ref[...] = (ti < tj).astype(_bf16)

    sub = tm // OUTPROJ_CHAINS
    h2 = []
    for s in range(OUTPROJ_CHAINS):
        rows = slice(s * sub, (s + 1) * sub)
        mixed = jnp.concatenate([_rms(od_ref[rows, :], gd_ref[...]), _rms(os_ref[rows, :], gs_ref[...])], axis=1)
        x1 = x_ref[rows, :] + jnp.dot(mixed.astype(_bf16), w_ref[...], preferred_element_type=_f32) + b_ref[...]
        _store_token_rows(x1_ref.at[pl.ds(s * sub * ROW_TILES, sub * ROW_TILES)], x1)
        h2.append(_rms(x1, g2_ref[...]))
    h2 = jnp.concatenate(h2, axis=0)
    h_hi = h2.astype(_bf16)
    h_lo = (h2 - h_hi.astype(_f32)).astype(_bf16)
    nt_dot = lambda a, b: lax.dot_general(a, b, (((1,), (1,)), ((), ())), preferred_element_type=_f32)
    logits = (nt_dot(wr_ref[0], h_hi) + nt_dot(wr_ref[0], h_lo) + nt_dot(wr_ref[1], h_hi)) + br_ref[...]
    eiota = lax.broadcasted_iota(jnp.int32, logits.shape, 0).astype(_f32)
    vals, idxs, sels = [], [], []
    l = logits
    for _ in range(TOP_K):
        m = jnp.max(l, axis=0, keepdims=True)
        idx = jnp.min(jnp.where(l == m, eiota, float(N_EXPERTS)), axis=0, keepdims=True)
        sel = eiota == idx
        vals.append(m); idxs.append(idx); sels.append(sel)
        l = jnp.where(sel, -jnp.inf, l)
    exps = [jnp.exp(v - vals[0]) for v in vals]
    den = functools.reduce(lambda a, b: a + b, exps)
    gates = [e / den for e in exps]

    chosen = functools.reduce(jnp.logical_or, sels).astype(_f32)
    prefix = jnp.dot(chosen.astype(_bf16), before_ref[...], preferred_element_type=_f32) + carry_ref[...]
    ranks = [jnp.sum(jnp.where(s, prefix, 0.0), axis=0, keepdims=True).astype(jnp.int32) for s in sels]
    carry_ref[...] = carry_ref[...] + jnp.sum(chosen, axis=1, keepdims=True)

    ri_ref[...] = jnp.concatenate([ix.astype(jnp.int32) for ix in idxs] + ranks, axis=0)
    rg_ref[...] = jnp.concatenate(gates + gates, axis=0)
    cnt_ref[...] = jnp.broadcast_to(carry_ref[...], cnt_ref.shape)


def _outproj_call(od, osw, x2, gd, gs, w, b, g2, wr_t, br):
    T = x2.shape[0]
    tm = TOKEN_TILE
    row = lambda i: (i, 0)
    col = lambda i: (0, i)
    fix = lambda i: (0, 0)
    return pl.pallas_call(
        _outproj_kernel,
        grid=(T // tm,),
        in_specs=[pl.BlockSpec((tm, W_DIL), row), pl.BlockSpec((tm, W_SWA), row),
                  pl.BlockSpec((tm, D_MODEL), row),
                  pl.BlockSpec((1, W_DIL), fix), pl.BlockSpec((1, W_SWA), fix),
                  pl.BlockSpec((D_MODEL, D_MODEL), fix), pl.BlockSpec((1, D_MODEL), fix),
                  pl.BlockSpec((1, D_MODEL), fix),
                  pl.BlockSpec((2, N_EXPERTS, D_MODEL), lambda i: (0, 0, 0)), pl.BlockSpec((N_EXPERTS, 1), fix)],
        out_specs=[pl.BlockSpec((tm * ROW_TILES, LANES), row), pl.BlockSpec((2 * TOP_K, tm), col),
                   pl.BlockSpec((2 * TOP_K, tm), col), pl.BlockSpec((N_EXPERTS, LANES), fix)],
        out_shape=[jax.ShapeDtypeStruct((T * ROW_TILES, LANES), _f32),
                   jax.ShapeDtypeStruct((2 * TOP_K, T), jnp.int32),
                   jax.ShapeDtypeStruct((2 * TOP_K, T), _f32),
                   jax.ShapeDtypeStruct((N_EXPERTS, LANES), _f32)],
        scratch_shapes=[pltpu.VMEM((N_EXPERTS, 1), _f32), pltpu.VMEM((tm, tm), _bf16)],
        compiler_params=pltpu.CompilerParams(dimension_semantics=("arbitrary",),
                                             vmem_limit_bytes=VMEM_LIMIT),
        name="outproj_router",
    )(od, osw, x2, gd, gs, w, b, g2, wr_t, br)


def _dispatch_kernel(d_ref, x1_ref, xs_hbm, sem):
    td = DISPATCH_TILE

    def start_group(g, c):
        for rr in range(DMA_GROUP):
            r = g * DMA_GROUP + rr
            src = _token_row(x1_ref, r)
            for k in range(TOP_K):
                pltpu.make_async_copy(src, _token_row(xs_hbm, d_ref[0, k * td + r]), sem).start(priority=k % 2)
        return c

    lax.fori_loop(0, td // DMA_GROUP, start_group, 0)

    def wait_group(g, c):
        for _ in range(DMA_GROUP * TOP_K):
            pltpu.make_async_copy(_token_row(x1_ref, 0), _token_row(xs_hbm, 0), sem).wait()
        return c

    lax.fori_loop(0, td // DMA_GROUP, wait_group, 0)


def _dispatch_call(dest_tiles, x1r, n_slots):
    nt = dest_tiles.shape[0]
    return pl.pallas_call(
        _dispatch_kernel,
        grid=(nt,),
        in_specs=[pl.BlockSpec((None, 1, TOP_K * DISPATCH_TILE), lambda i: (i, 0, 0), memory_space=pltpu.SMEM),
                  pl.BlockSpec((DISPATCH_TILE * ROW_TILES, LANES), lambda i: (i, 0))],
        out_specs=pl.BlockSpec(memory_space=pl.ANY),
        out_shape=jax.ShapeDtypeStruct((n_slots * ROW_TILES, LANES), _f32),
        scratch_shapes=[pltpu.SemaphoreType.DMA(())],
        compiler_params=pltpu.CompilerParams(dimension_semantics=("arbitrary",),
                                             vmem_limit_bytes=VMEM_LIMIT),
        name="dispatch_rows",
    )(dest_tiles, x1r)


def _moe_kernel(be_ref, nv_ref, src_ref, grp_ref, nxt_ref, xs_ref, g2_ref, w1_hbm, b1_ref, w2_hbm, b2_ref,
                ys_ref, st1, st2, w1p, w2b, sem):
    i = pl.program_id(0)
    nv = nv_ref[i]
    tm = EXPERT_TILE

    def weight_copies(e, slot):
        return (pltpu.make_async_copy(w1_hbm.at[e], st1.at[slot], sem.at[0, slot]),
                pltpu.make_async_copy(w2_hbm.at[e], st2.at[slot], sem.at[1, slot]))

    @pl.when(nv > 0)
    def _():
        first_of_expert = jnp.logical_or(i == 0, be_ref[i] != be_ref[jnp.maximum(i - 1, 0)])

        @pl.when(first_of_expert)
        def _():
            slot = grp_ref[i] % 2

            @pl.when(i == 0)
            def _():
                for c in weight_copies(be_ref[0], 0):
                    c.start()

            for c in weight_copies(be_ref[i], slot):
                c.wait()

            @pl.when(nxt_ref[i] >= 0)
            def _():
                for c in weight_copies(nxt_ref[i], 1 - slot):
                    c.start()

            src = lax.broadcasted_iota(jnp.int32, (GLU_GROUP, GLU_GROUP), 0)
            dst = lax.broadcasted_iota(jnp.int32, (GLU_GROUP, GLU_GROUP), 1)
            half = GLU_GROUP // 2
            perm = (src == jnp.where(dst < half, 2 * dst, 2 * (dst - half) + 1)).astype(_bf16)
            for g in range(2 * D_FF // GLU_GROUP):
                cols = slice(g * GLU_GROUP, (g + 1) * GLU_GROUP)
                w1p[:, cols] = jnp.dot(st1[slot, :, cols].astype(_bf16), perm,
                                       preferred_element_type=_f32).astype(_bf16)
            w2b[...] = st2[slot].astype(_bf16)

        hm = tm // MOE_CHAINS

        def sub_block(s):
            rows = lambda c: pl.ds(s * hm * ROW_TILES + c, hm, stride=ROW_TILES)
            x = jnp.concatenate([xs_ref[rows(c), :] for c in range(ROW_TILES)], axis=1)
            x = jnp.where(lax.broadcasted_iota(jnp.int32, (hm, 1), 0) < nv - s * hm, x, 0.0)
            h = _rms(x, g2_ref[...]).astype(_bf16)
            u = jnp.dot(h, w1p[...], preferred_element_type=_f32) + b1_ref[0]
            half = GLU_GROUP // 2
            hid = []
            for g in range(2 * D_FF // GLU_GROUP):
                glu = jnp.minimum(u[:, g * GLU_GROUP:g * GLU_GROUP + half], SWIGLU_LIMIT)
                lin = jnp.clip(u[:, g * GLU_GROUP + half:(g + 1) * GLU_GROUP], -SWIGLU_LIMIT, SWIGLU_LIMIT)
                hid.append(glu * (1.0 / (1.0 + jnp.exp(-SWIGLU_ALPHA * glu))) * (lin + 1.0))
            hid = jnp.concatenate(hid, axis=1).astype(_bf16)
            y = jnp.dot(hid, w2b[...], preferred_element_type=_f32) + b2_ref[0]
            for c in range(ROW_TILES):
                ys_ref[rows(c), :] = y[:, c * LANES:(c + 1) * LANES]

        for n_sub in range(1, MOE_CHAINS + 1):
            @pl.when(jnp.logical_and(nv > (n_sub - 1) * hm, nv <= n_sub * hm))
            def _(n_sub=n_sub):
                for s in range(n_sub):
                    sub_block(s)


def _moe_call(block_e, n_valid, block_src, block_grp, next_e, xs, g2, w1, b1p, w2, b2):
    nblk = block_e.shape[0]
    tm = EXPERT_TILE
    by_e = lambda i, be, nv, src, grp, nxt: (be[i], 0, 0)
    by_src = lambda i, be, nv, src, grp, nxt: (src[i], 0)
    fix = lambda i, be, nv, src, grp, nxt: (0, 0)
    grid_spec = pltpu.PrefetchScalarGridSpec(
        num_scalar_prefetch=5,
        grid=(nblk,),
        in_specs=[pl.BlockSpec((tm * ROW_TILES, LANES), by_src), pl.BlockSpec((1, D_MODEL), fix),
                  pl.BlockSpec(memory_space=pl.ANY), pl.BlockSpec((1, 1, 2 * D_FF), by_e),
                  pl.BlockSpec(memory_space=pl.ANY), pl.BlockSpec((1, 1, D_MODEL), by_e)],
        out_specs=pl.BlockSpec((tm * ROW_TILES, LANES), by_src),
        scratch_shapes=[pltpu.VMEM((2, D_MODEL, 2 * D_FF), _f32), pltpu.VMEM((2, D_FF, D_MODEL), _f32),
                        pltpu.VMEM((D_MODEL, 2 * D_FF), _bf16), pltpu.VMEM((D_FF, D_MODEL), _bf16),
                        pltpu.SemaphoreType.DMA((2, 2))],
    )
    return pl.pallas_call(
        _moe_kernel,
        grid_spec=grid_spec,
        out_shape=jax.ShapeDtypeStruct(xs.shape, _f32),
        compiler_params=pltpu.CompilerParams(dimension_semantics=("arbitrary",),
                                             vmem_limit_bytes=VMEM_LIMIT),
        name="moe_experts",
    )(block_e, n_valid, block_src, block_grp, next_e, xs, g2, w1, b1p, w2, b2)


def _combine_kernel(dcur_ref, dnext_ref, x1_ref, ys_hbm, g_ref, gf_ref, o_ref, gbuf, sem):
    tc = COMBINE_TILE
    i = pl.program_id(0)
    slot = i % 2

    def row_copy(d_ref, s, r, k):
        dst = gbuf.at[s, k, pl.ds(r * ROW_TILES, ROW_TILES)]
        return pltpu.make_async_copy(_token_row(ys_hbm, d_ref[0, k * tc + r]), dst, sem.at[s])

    def issue(d_ref, s):
        def group(g, c):
            for rr in range(DMA_GROUP):
                for k in range(TOP_K):
                    row_copy(d_ref, s, g * DMA_GROUP + rr, k).start(priority=k % 2)
            return c
        lax.fori_loop(0, tc // DMA_GROUP, group, 0)

    @pl.when(i == 0)
    def _():
        issue(dcur_ref, 0)

    @pl.when(i + 1 < pl.num_programs(0))
    def _():
        issue(dnext_ref, 1 - slot)

    def wait_group(g, c):
        for _ in range(DMA_GROUP * TOP_K):
            row_copy(dcur_ref, slot, 0, 0).wait()
        return c

    lax.fori_loop(0, tc // DMA_GROUP, wait_group, 0)

    g = g_ref[...]
    acc = _load_token_rows(x1_ref, tc)
    for k in range(TOP_K):
        acc = acc + g[:, k:k + 1] * _load_token_rows(gbuf.at[slot, k], tc)
    o_ref[...] = _rms(acc, gf_ref[...])


def _combine_call(dest_tiles, x1r, ys, gates_t, gf):
    nt = dest_tiles.shape[0]
    tc = COMBINE_TILE
    row = lambda i: (i, 0)
    return pl.pallas_call(
        _combine_kernel,
        grid=(nt,),
        in_specs=[pl.BlockSpec((None, 1, TOP_K * tc), lambda i: (i, 0, 0), memory_space=pltpu.SMEM),
                  pl.BlockSpec((None, 1, TOP_K * tc), lambda i: (jnp.minimum(i + 1, nt - 1), 0, 0),
                               memory_space=pltpu.SMEM),
                  pl.BlockSpec((tc * ROW_TILES, LANES), row), pl.BlockSpec(memory_space=pl.ANY),
                  pl.BlockSpec((tc, TOP_K), row), pl.BlockSpec((1, D_MODEL), lambda i: (0, 0))],
        out_specs=pl.BlockSpec((tc, D_MODEL), row),
        out_shape=jax.ShapeDtypeStruct((nt * tc, D_MODEL), _f32),
        scratch_shapes=[pltpu.VMEM((2, TOP_K, tc * ROW_TILES, LANES), _f32), pltpu.SemaphoreType.DMA((2,))],
        compiler_params=pltpu.CompilerParams(dimension_semantics=("arbitrary",),
                                             vmem_limit_bytes=VMEM_LIMIT),
        name="combine_norm",
    )(dest_tiles, dest_tiles, x1r, ys, gates_t, gf)


def _plan_blocks(route_i, counts, n_tokens):
    tm = EXPERT_TILE
    nblk = (n_tokens * TOP_K + N_EXPERTS * tm) // tm
    counts = counts.astype(jnp.int32)
    blocks_e = (counts + tm - 1) // tm
    blk_end = jnp.cumsum(blocks_e)
    pad_start = (blk_end - blocks_e) * tm
    experts = jnp.arange(N_EXPERTS, dtype=jnp.int32)
    hot = route_i[:TOP_K, :, None] == experts
    dest = jnp.sum(jnp.where(hot, pad_start, 0), axis=-1) + route_i[TOP_K:]
    n_used = blk_end[-1]
    blk = jnp.arange(nblk, dtype=jnp.int32)
    src = jnp.minimum(blk, jnp.maximum(n_used - 1, 0))
    be = jnp.minimum(jnp.sum((blk_end[None, :] <= src[:, None]).astype(jnp.int32), axis=1), N_EXPERTS - 1)
    hot_b = be[:, None] == experts
    pick = lambda table: jnp.sum(jnp.where(hot_b, table, 0), axis=1)
    n_valid = jnp.where(blk < n_used, jnp.clip(pick(counts) - (src * tm - pick(pad_start)), 0, tm), 0)
    used = blocks_e > 0
    grp_e = jnp.cumsum(used.astype(jnp.int32)) - 1
    later = jnp.where((experts[None, :] > experts[:, None]) & used[None, :], experts[None, :], N_EXPERTS)
    nxt_e = jnp.min(later, axis=1)
    nxt_e = jnp.where(nxt_e < N_EXPERTS, nxt_e, -1)
    return be, n_valid.astype(jnp.int32), src, pick(grp_e), pick(nxt_e), dest


def _split_bf16(w):
    hi = w.astype(_bf16)
    return jnp.stack([hi, (w - hi.astype(_f32)).astype(_bf16)])


def _tile_dest(dest, tile):
    k, t = dest.shape
    return dest.reshape(k, t // tile, tile).transpose(1, 0, 2).reshape(t // tile, 1, k * tile)


def kernel(x, positions, norm1_g, w_in, b_in, sinks, gn_dil, gn_swa, w_out, b_out,
           norm2_g, w_router, b_router, w_mlp1, b_mlp1, w_mlp2, b_mlp2, norm_f_g):
    batch, seq, d = x.shape
    T = batch * seq
    x2 = x.reshape(T, d)
    depth = norm1_g.shape[0]
    assert depth == 1, "the final norm is fused into the last layer's combine kernel"
    inv = ROPE_THETA ** (-jnp.arange(0, HEAD_DIM, 2, dtype=_f32) / HEAD_DIM)
    inv = jnp.tile(inv, LANES // (HEAD_DIM // 2)).reshape(1, LANES)
    pos = positions.reshape(T, 1)
    for layer in range(depth):
        qa, ka, va, qb, kb, vb = _inproj_call(
            x2, pos, inv, norm1_g[layer].reshape(1, d), w_in[layer].astype(_bf16), b_in[layer].reshape(1, D_IN))
        o_dil = _attn_call(qa, ka, va, None, batch=batch, seq=seq,
                           configs=tuple((w // r, r) for w, r in DILATED_CONFIGS), kv_group=1)
        sink_lanes = jnp.repeat(sinks[layer].astype(_f32), HEAD_DIM).reshape(1, W_SWA)
        o_swa = _attn_call(qb, kb, vb, sink_lanes, batch=batch, seq=seq,
                           configs=((SWA_WINDOW - 1, 1),), kv_group=N_HEADS_SWA // N_KV_SWA // PAIR)
        x1r, route_i, route_g, counts = _outproj_call(
            o_dil, o_swa, x2, gn_dil[layer].reshape(1, W_DIL), gn_swa[layer].reshape(1, W_SWA),
            w_out[layer].astype(_bf16), b_out[layer].reshape(1, d), norm2_g[layer].reshape(1, d),
            _split_bf16(w_router[layer].T), b_router[layer].reshape(N_EXPERTS, 1))
        block_e, n_valid, block_src, block_grp, next_e, dest = _plan_blocks(route_i, counts[:, 0], T)
        xs = _dispatch_call(_tile_dest(dest, DISPATCH_TILE), x1r, block_e.shape[0] * EXPERT_TILE)
        half = GLU_GROUP // 2
        b1p = b_mlp1[layer].reshape(N_EXPERTS, 2 * D_FF // GLU_GROUP, half, 2)
        b1p = jnp.swapaxes(b1p, 2, 3).reshape(N_EXPERTS, 1, 2 * D_FF)
        ys = _moe_call(block_e, n_valid, block_src, block_grp, next_e, xs, norm2_g[layer].reshape(1, d),
                       w_mlp1[layer], b1p, w_mlp2[layer], b_mlp2[layer].reshape(N_EXPERTS, 1, d))
        x2 = _combine_call(_tile_dest(dest, COMBINE_TILE), x1r, ys, route_g[:TOP_K].T, norm_f_g.reshape(1, d))
    return x2.reshape(batch, seq, d)
```

```python
import functools

import jax
import jax.numpy as jnp
from jax import lax
from jax.experimental import pallas as pl
from jax.experimental.pallas import tpu as pltpu

D_MODEL = 1024
HEAD_DIM = 64
N_HEADS_DIL = 8
DILATED_CONFIGS = ((128, 1), (512, 4), (2048, 16))
N_HEADS_SWA = 8
N_KV_SWA = 2
SWA_WINDOW = 128
ATTN_BLOCK = 128
ROPE_THETA = 10000.0
N_EXPERTS = 32
TOP_K = 4
D_FF = D_MODEL
SWIGLU_ALPHA = 1.702
SWIGLU_LIMIT = 7.0
NORM_EPS = 1e-5
LOG2E = 1.4426950408889634

W_DIL = N_HEADS_DIL * HEAD_DIM
W_SWA = N_HEADS_SWA * HEAD_DIM
W_KV_SWA = N_KV_SWA * HEAD_DIM
D_IN = 3 * W_DIL + W_SWA + 2 * W_KV_SWA

LANES = 128
PAIR = LANES // HEAD_DIM
ROW_TILES = D_MODEL // LANES
TOKEN_TILE = 1024
EXPERT_TILE = 1024
DISPATCH_TILE = 2048
COMBINE_TILE = 256
DMA_GROUP = 8
ATTN_UNROLL = 32
MOE_CHAINS = 4
INPROJ_TILE = 2048
INPROJ_CHAINS = 8
OUTPROJ_CHAINS = 2
GLU_GROUP = 256
VMEM_LIMIT = 56 * 1024 * 1024

_f32 = jnp.float32
_bf16 = jnp.bfloat16


def _rms(x, g):
    return x * lax.rsqrt(jnp.mean(x * x, axis=-1, keepdims=True) + NORM_EPS) * g


def _store_token_rows(ref, x):
    tm = x.shape[0]
    for c in range(ROW_TILES):
        ref[pl.ds(c, tm, stride=ROW_TILES), :] = x[:, c * LANES:(c + 1) * LANES]


def _load_token_rows(ref, tm):
    return jnp.concatenate([ref[pl.ds(c, tm, stride=ROW_TILES), :] for c in range(ROW_TILES)], axis=1)


def _token_row(ref, tok):
    return ref.at[pl.ds(pl.multiple_of(tok * ROW_TILES, ROW_TILES), ROW_TILES)]


def _inproj_kernel(x_ref, pos_ref, inv_ref, g_ref, w_ref, b_ref, *out_refs):
    sub = x_ref.shape[0] // INPROJ_CHAINS
    for s in range(INPROJ_CHAINS):
        rows = slice(s * sub, (s + 1) * sub)
        _inproj_rows(x_ref[rows, :], pos_ref[rows, :], inv_ref, g_ref, w_ref, b_ref,
                     [r.at[rows, :] for r in out_refs])


def _inproj_rows(x, pos, inv_ref, g_ref, w_ref, b_ref, out_refs):
    qa_ref, ka_ref, va_ref, qb_ref, kb_ref, vb_ref = out_refs
    h = _rms(x, g_ref[...]).astype(_bf16)
    proj = jnp.dot(h, w_ref[...], preferred_element_type=_f32) + b_ref[...]

    ang = pos.astype(_f32) * inv_ref[...]
    lane = lax.broadcasted_iota(jnp.int32, ang.shape, 1)
    first_half = (lane % HEAD_DIM) < (HEAD_DIM // 2)
    cos = jnp.cos(ang)
    sin = jnp.where(first_half, -jnp.sin(ang), jnp.sin(ang))

    def rope(t, scale):
        outs = []
        for c in range(t.shape[1] // LANES):
            tc = t[:, c * LANES:(c + 1) * LANES]
            partner = jnp.where(first_half, pltpu.roll(tc, LANES - HEAD_DIM // 2, 1),
                                pltpu.roll(tc, HEAD_DIM // 2, 1))
            outs.append((tc * cos + partner * sin) * scale)
        return jnp.concatenate(outs, axis=1) if len(outs) > 1 else outs[0]

    scale = HEAD_DIM ** -0.5 * LOG2E
    o = 0
    qa_ref[...] = rope(proj[:, o:o + W_DIL], scale).astype(_bf16); o += W_DIL
    ka_ref[...] = rope(proj[:, o:o + W_DIL], 1.0).astype(_bf16); o += W_DIL
    va_ref[...] = proj[:, o:o + W_DIL].astype(_bf16); o += W_DIL
    qb_ref[...] = rope(proj[:, o:o + W_SWA], scale).astype(_bf16); o += W_SWA
    kb = rope(proj[:, o:o + W_KV_SWA], 1.0); o += W_KV_SWA
    vb = proj[:, o:o + W_KV_SWA]

    def dup_heads(t):
        sw = pltpu.roll(t, HEAD_DIM, 1)
        lo = lane < HEAD_DIM
        return jnp.concatenate([jnp.where(lo, t, sw), jnp.where(lo, sw, t)], axis=1)

    kb_ref[...] = dup_heads(kb).astype(_bf16)
    vb_ref[...] = dup_heads(vb).astype(_bf16)


def _inproj_call(x2, pos, inv, g, w, b):
    T = x2.shape[0]
    tm = INPROJ_TILE
    row = lambda i: (i, 0)
    fix = lambda i: (0, 0)
    widths = (W_DIL, W_DIL, W_DIL, W_SWA, PAIR * W_KV_SWA, PAIR * W_KV_SWA)
    return pl.pallas_call(
        _inproj_kernel,
        grid=(T // tm,),
        in_specs=[pl.BlockSpec((tm, D_MODEL), row), pl.BlockSpec((tm, 1), row),
                  pl.BlockSpec((1, LANES), fix), pl.BlockSpec((1, D_MODEL), fix),
                  pl.BlockSpec((D_MODEL, D_IN), fix), pl.BlockSpec((1, D_IN), fix)],
        out_specs=[pl.BlockSpec((tm, wd), row) for wd in widths],
        out_shape=[jax.ShapeDtypeStruct((T, wd), _bf16) for wd in widths],
        compiler_params=pltpu.CompilerParams(dimension_semantics=("parallel",),
                                             vmem_limit_bytes=VMEM_LIMIT),
        name="inproj",
    )(x2, pos, inv, g, w, b)


def _attn_kernel(*refs, configs, seq, with_sink):
    if with_sink:
        q_ref, k_ref, v_ref, sink_ref, o_ref = refs[:5]
        scratch = refs[5:]
    else:
        q_ref, k_ref, v_ref, o_ref = refs[:4]
        scratch = refs[4:]
    n_cfg = len(configs)
    n_acc = n_cfg if n_cfg > 1 else 0
    obufs = scratch[:n_acc]
    mbufs = scratch[n_acc:2 * n_acc]
    lbufs = scratch[2 * n_acc:3 * n_acc]
    extra = list(scratch[3 * n_acc:])
    blk = ATTN_BLOCK
    dilated = [r for _, r in configs if r > 1]
    if dilated:
        qf, kf, vf = extra[:3]
        qf[...] = q_ref[...].astype(_f32)
        kf[...] = k_ref[...].astype(_f32)
        vf[...] = v_ref[...].astype(_f32)
        extra = extra[3:]

    lane = lax.broadcasted_iota(jnp.int32, (blk, LANES), 1)
    head0 = lane < HEAD_DIM
    lane_row = lax.broadcasted_iota(jnp.int32, (1, LANES), 1)
    head_rows = [(lane_row < HEAD_DIM).astype(_bf16), (lane_row >= HEAD_DIM).astype(_bf16)]
    qi = lax.broadcasted_iota(jnp.int32, (blk, 2 * blk), 0)
    kj = lax.broadcasted_iota(jnp.int32, (blk, 2 * blk), 1)
    dist = qi - kj + blk

    def merge(os_, ms_, ls_):
        ms_ = list(ms_)
        if with_sink:
            ms_.append(jnp.broadcast_to(sink_ref[...] * LOG2E, ms_[0].shape))
        mx = functools.reduce(jnp.maximum, ms_)
        es = [jnp.exp2(m - mx) for m in ms_]
        num = functools.reduce(lambda a, b: a + b, [e * o for e, o in zip(es, os_)])
        den = functools.reduce(lambda a, b: a + b, [e * l for e, l in zip(es, ls_)])
        if with_sink:
            den = den + es[-1]
        return num / den

    base_perm = None
    for ci, (max_dist, r) in enumerate(configs):
        nb = seq // r // blk
        band = (dist >= 0) & (dist <= max_dist)
        obuf, mbuf, lbuf = (obufs[ci], mbufs[ci], lbufs[ci]) if n_cfg > 1 else (None, None, None)
        if r == 1:
            qp, kp, vp = q_ref, k_ref, v_ref
        else:
            perm_bf16 = extra[:3]
            extra = extra[3:]
            later = [r2 for _, r2 in configs[ci + 1:] if r2 % r == 0]
            perm_f32 = []
            if later:
                perm_f32 = extra[:3]
                extra = extra[3:]
            r_src, srcs = 1, (qf, kf, vf)
            if base_perm is not None and r % base_perm[0] == 0:
                r_src, srcs = base_perm
            if perm_f32:
                base_perm = (r, perm_f32)
            qp, kp, vp = perm_bf16
            piece = 2 * blk
            step = r // r_src
            per_res = seq // r // piece
            len_src = seq // r_src

            def permute(idx, carry, r_src=r_src, srcs=srcs, step=step, per_res=per_res, len_src=len_src,
                        perm_bf16=perm_bf16, perm_f32=perm_f32):
                c = idx // per_res
                start = (c % r_src) * len_src + c // r_src + (step * piece) * (idx % per_res)
                src = pl.ds(start, piece, stride=step)
                dst = pl.ds(pl.multiple_of(idx * piece, piece), piece)
                for t in range(3):
                    val = srcs[t][src, :]
                    perm_bf16[t][dst, :] = val.astype(_bf16)
                    if perm_f32:
                        perm_f32[t][dst, :] = val
                return carry

            lax.fori_loop(0, seq // piece, permute, 0, unroll=True)

        def body(n, carry, r=r, nb=nb, band=band, obuf=obuf, lbuf=lbuf, qp=qp, kp=kp, vp=vp):
            j = n % nb
            cur = pl.ds(pl.multiple_of(n * blk, blk), blk)
            prev = pl.ds(pl.multiple_of(jnp.maximum(n - 1, 0) * blk, blk), blk)
            q = qp[cur, :]
            k2 = jnp.concatenate([kp[prev, :], kp[cur, :]], axis=0)
            v2 = jnp.concatenate([vp[prev, :], vp[cur, :]], axis=0)
            mask = band & (kj >= jnp.where(j > 0, 0, blk))
            q2 = jnp.concatenate([q * hm for hm in head_rows], axis=0)
            s2 = lax.dot_general(q2, k2, (((1,), (1,)), ((), ())), preferred_element_type=_f32)
            ps, maxs, sums = [], [], []
            for h in range(PAIR):
                s = jnp.where(mask, s2[h * blk:(h + 1) * blk], -jnp.inf)
                m = jnp.max(s, axis=-1, keepdims=True)
                p = jnp.exp2(s - m)
                ps.append(p.astype(_bf16))
                maxs.append(jnp.broadcast_to(m, (blk, LANES)))
                sums.append(jnp.broadcast_to(jnp.sum(p, axis=-1, keepdims=True), (blk, LANES)))
            o2 = jnp.dot(jnp.concatenate(ps, axis=0), v2, preferred_element_type=_f32)
            outs = [o2[:blk], o2[blk:]]
            start = n // nb + (r * blk) * j
            dst = pl.ds(pl.multiple_of(start, blk), blk) if r == 1 else pl.ds(start, blk, stride=r)
            o_pair = jnp.where(head0, outs[0], outs[1])
            m_pair = jnp.where(head0, maxs[0], maxs[1])
            l_pair = jnp.where(head0, sums[0], sums[1])
            if n_cfg == 1:
                o_ref[dst, :] = merge([o_pair], [m_pair], [l_pair])
            else:
                obuf[dst, :] = o_pair
                mbuf[dst, :] = m_pair
                lbuf[dst, :] = l_pair
            return carry

        lax.fori_loop(0, seq // blk, body, 0, unroll=ATTN_UNROLL)

    if n_cfg > 1:
        chunk = 4 * blk

        def finish(i, carry):
            sl = pl.ds(pl.multiple_of(i * chunk, chunk), chunk)
            o_ref[sl, :] = merge([b[sl, :] for b in obufs], [b[sl, :] for b in mbufs], [b[sl, :] for b in lbufs])
            return carry

        lax.fori_loop(0, seq // chunk, finish, 0)


def _attn_call(q, k, v, sinks, *, batch, seq, configs, kv_group):
    n_pairs = q.shape[1] // LANES
    with_sink = sinks is not None
    qmap = lambda b, p: (b, p)
    kvmap = lambda b, p: (b, p // kv_group)
    in_specs = [pl.BlockSpec((seq, LANES), qmap), pl.BlockSpec((seq, LANES), kvmap),
                pl.BlockSpec((seq, LANES), kvmap)]
    args = [q, k, v]
    if with_sink:
        in_specs.append(pl.BlockSpec((1, LANES), lambda b, p: (0, p)))
        args.append(sinks)
    buf = lambda dt: pltpu.VMEM((seq, LANES), dt)
    scratch = [buf(_f32) for _ in range(3 * len(configs) if len(configs) > 1 else 0)]
    if any(r > 1 for _, r in configs):
        scratch += [buf(_f32) for _ in range(3)]
    for ci, (_, r) in enumerate(configs):
        if r > 1:
            scratch += [buf(_bf16) for _ in range(3)]
            if any(r2 % r == 0 for _, r2 in configs[ci + 1:]):
                scratch += [buf(_f32) for _ in range(3)]
    return pl.pallas_call(
        functools.partial(_attn_kernel, configs=configs, seq=seq, with_sink=with_sink),
        grid=(batch, n_pairs),
        in_specs=in_specs,
        out_specs=pl.BlockSpec((seq, LANES), qmap),
        out_shape=jax.ShapeDtypeStruct(q.shape, _f32),
        scratch_shapes=scratch,
        compiler_params=pltpu.CompilerParams(dimension_semantics=("parallel", "parallel"),
                                             vmem_limit_bytes=VMEM_LIMIT),
        name="attn_sink" if with_sink else "attn_dilated",
    )(*args)


def _outproj_kernel(od_ref, os_ref, x_ref, gd_ref, gs_ref, w_ref, b_ref, g2_ref, wr_ref, br_ref,
                    x1_ref, ri_ref, rg_ref, cnt_ref, carry_ref, before_ref):
    tm = x_ref.shape[0]

    @pl.when(pl.program_id(0) == 0)
    def _():
        carry_ref[...] = jnp.zeros_like(carry_ref)
        ti = lax.broadcasted_iota(jnp.int32, (tm, tm), 0)
        tj = lax.broadcasted_iota(jnp.int32, (tm, tm), 1)
        before_ref[...] = (ti < tj).astype(_bf16)

    sub = tm // OUTPROJ_CHAINS
    h2 = []
    for s in range(OUTPROJ_CHAINS):
        rows = slice(s * sub, (s + 1) * sub)
        mixed = jnp.concatenate([_rms(od_ref[rows, :], gd_ref[...]), _rms(os_ref[rows, :], gs_ref[...])], axis=1)
        x1 = x_ref[rows, :] + jnp.dot(mixed.astype(_bf16), w_ref[...], preferred_element_type=_f32) + b_ref[...]
        _store_token_rows(x1_ref.at[pl.ds(s * sub * ROW_TILES, sub * ROW_TILES)], x1)
        h2.append(_rms(x1, g2_ref[...]))
    h2 = jnp.concatenate(h2, axis=0)
    h_hi = h2.astype(_bf16)
    h_lo = (h2 - h_hi.astype(_f32)).astype(_bf16)
    nt_dot = lambda a, b: lax.dot_general(a, b, (((1,), (1,)), ((), ())), preferred_element_type=_f32)
    logits = (nt_dot(wr_ref[0], h_hi) + nt_dot(wr_ref[0], h_lo) + nt_dot(wr_ref[1], h_hi)) + br_ref[...]
    eiota = lax.broadcasted_iota(jnp.int32, logits.shape, 0).astype(_f32)
    vals, idxs, sels = [], [], []
    l = logits
    for _ in range(TOP_K):
        m = jnp.max(l, axis=0, keepdims=True)
        idx = jnp.min(jnp.where(l == m, eiota, float(N_EXPERTS)), axis=0, keepdims=True)
        sel = eiota == idx
        vals.append(m); idxs.append(idx); sels.append(sel)
        l = jnp.where(sel, -jnp.inf, l)
    exps = [jnp.exp(v - vals[0]) for v in vals]
    den = functools.reduce(lambda a, b: a + b, exps)
    gates = [e / den for e in exps]

    chosen = functools.reduce(jnp.logical_or, sels).astype(_f32)
    prefix = jnp.dot(chosen.astype(_bf16), before_ref[...], preferred_element_type=_f32) + carry_ref[...]
    ranks = [jnp.sum(jnp.where(s, prefix, 0.0), axis=0, keepdims=True).astype(jnp.int32) for s in sels]
    carry_ref[...] = carry_ref[...] + jnp.sum(chosen, axis=1, keepdims=True)

    ri_ref[...] = jnp.concatenate([ix.astype(jnp.int32) for ix in idxs] + ranks, axis=0)
    rg_ref[...] = jnp.concatenate(gates + gates, axis=0)
    cnt_ref[...] = jnp.broadcast_to(carry_ref[...], cnt_ref.shape)


def _outproj_call(od, osw, x2, gd, gs, w, b, g2, wr_t, br):
    T = x2.shape[0]
    tm = TOKEN_TILE
    row = lambda i: (i, 0)
    col = lambda i: (0, i)
    fix = lambda i: (0, 0)
    return pl.pallas_call(
        _outproj_kernel,
        grid=(T // tm,),
        in_specs=[pl.BlockSpec((tm, W_DIL), row), pl.BlockSpec((tm, W_SWA), row),
                  pl.BlockSpec((tm, D_MODEL), row),
                  pl.BlockSpec((1, W_DIL), fix), pl.BlockSpec((1, W_SWA), fix),
                  pl.BlockSpec((D_MODEL, D_MODEL), fix), pl.BlockSpec((1, D_MODEL), fix),
                  pl.BlockSpec((1, D_MODEL), fix),
                  pl.BlockSpec((2, N_EXPERTS, D_MODEL), lambda i: (0, 0, 0)), pl.BlockSpec((N_EXPERTS, 1), fix)],
        out_specs=[pl.BlockSpec((tm * ROW_TILES, LANES), row), pl.BlockSpec((2 * TOP_K, tm), col),
                   pl.BlockSpec((2 * TOP_K, tm), col), pl.BlockSpec((N_EXPERTS, LANES), fix)],
        out_shape=[jax.ShapeDtypeStruct((T * ROW_TILES, LANES), _f32),
                   jax.ShapeDtypeStruct((2 * TOP_K, T), jnp.int32),
                   jax.ShapeDtypeStruct((2 * TOP_K, T), _f32),
                   jax.ShapeDtypeStruct((N_EXPERTS, LANES), _f32)],
        scratch_shapes=[pltpu.VMEM((N_EXPERTS, 1), _f32), pltpu.VMEM((tm, tm), _bf16)],
        compiler_params=pltpu.CompilerParams(dimension_semantics=("arbitrary",),
                                             vmem_limit_bytes=VMEM_LIMIT),
        name="outproj_router",
    )(od, osw, x2, gd, gs, w, b, g2, wr_t, br)


def _dispatch_kernel(*refs):
    d_ref = refs[:TOP_K]
    x1_ref, xs_hbm, sem = refs[TOP_K:]
    td = DISPATCH_TILE

    def start_group(g, c):
        for rr in range(DMA_GROUP):
            r = g * DMA_GROUP + rr
            src = _token_row(x1_ref, r)
            for k in range(TOP_K):
                pltpu.make_async_copy(src, _token_row(xs_hbm, d_ref[k][r]), sem).start(priority=k % 2)
        return c

    lax.fori_loop(0, td // DMA_GROUP, start_group, 0)

    def wait_group(g, c):
        for _ in range(DMA_GROUP * TOP_K):
            pltpu.make_async_copy(_token_row(x1_ref, 0), _token_row(xs_hbm, 0), sem).wait()
        return c

    lax.fori_loop(0, td // DMA_GROUP, wait_group, 0)


def _dispatch_call(dest, x1r, n_slots):
    nt = dest.shape[1] // DISPATCH_TILE
    return pl.pallas_call(
        _dispatch_kernel,
        grid=(nt,),
        in_specs=[pl.BlockSpec((DISPATCH_TILE,), lambda i: (i,), memory_space=pltpu.SMEM) for _ in range(TOP_K)]
                 + [pl.BlockSpec((DISPATCH_TILE * ROW_TILES, LANES), lambda i: (i, 0))],
        out_specs=pl.BlockSpec(memory_space=pl.ANY),
        out_shape=jax.ShapeDtypeStruct((n_slots * ROW_TILES, LANES), _f32),
        scratch_shapes=[pltpu.SemaphoreType.DMA(())],
        compiler_params=pltpu.CompilerParams(dimension_semantics=("arbitrary",),
                                             vmem_limit_bytes=VMEM_LIMIT),
        name="dispatch_rows",
    )(*[dest[k] for k in range(TOP_K)], x1r)


def _moe_kernel(be_ref, nv_ref, src_ref, grp_ref, nxt_ref, xs_ref, g2_ref, w1_hbm, b1_ref, w2_hbm, b2_ref,
                ys_ref, st1, st2, w1p, w2b, sem):
    i = pl.program_id(0)
    nv = nv_ref[i]
    tm = EXPERT_TILE

    def weight_copies(e, slot):
        return (pltpu.make_async_copy(w1_hbm.at[e], st1.at[slot], sem.at[0, slot]),
                pltpu.make_async_copy(w2_hbm.at[e], st2.at[slot], sem.at[1, slot]))

    @pl.when(nv > 0)
    def _():
        first_of_expert = jnp.logical_or(i == 0, be_ref[i] != be_ref[jnp.maximum(i - 1, 0)])

        @pl.when(first_of_expert)
        def _():
            slot = grp_ref[i] % 2

            @pl.when(i == 0)
            def _():
                for c in weight_copies(be_ref[0], 0):
                    c.start()

            for c in weight_copies(be_ref[i], slot):
                c.wait()

            @pl.when(nxt_ref[i] >= 0)
            def _():
                for c in weight_copies(nxt_ref[i], 1 - slot):
                    c.start()

            src = lax.broadcasted_iota(jnp.int32, (GLU_GROUP, GLU_GROUP), 0)
            dst = lax.broadcasted_iota(jnp.int32, (GLU_GROUP, GLU_GROUP), 1)
            half = GLU_GROUP // 2
            perm = (src == jnp.where(dst < half, 2 * dst, 2 * (dst - half) + 1)).astype(_bf16)
            for g in range(2 * D_FF // GLU_GROUP):
                cols = slice(g * GLU_GROUP, (g + 1) * GLU_GROUP)
                w1p[:, cols] = jnp.dot(st1[slot, :, cols].astype(_bf16), perm,
                                       preferred_element_type=_f32).astype(_bf16)
            w2b[...] = st2[slot].astype(_bf16)

        hm = tm // MOE_CHAINS

        def sub_block(s):
            rows = lambda c: pl.ds(s * hm * ROW_TILES + c, hm, stride=ROW_TILES)
            x = jnp.concatenate([xs_ref[rows(c), :] for c in range(ROW_TILES)], axis=1)
            x = jnp.where(lax.broadcasted_iota(jnp.int32, (hm, 1), 0) < nv - s * hm, x, 0.0)
            h = _rms(x, g2_ref[...]).astype(_bf16)
            u = jnp.dot(h, w1p[...], preferred_element_type=_f32) + b1_ref[0]
            half = GLU_GROUP // 2
            hid = []
            for g in range(2 * D_FF // GLU_GROUP):
                glu = jnp.minimum(u[:, g * GLU_GROUP:g * GLU_GROUP + half], SWIGLU_LIMIT)
                lin = jnp.clip(u[:, g * GLU_GROUP + half:(g + 1) * GLU_GROUP], -SWIGLU_LIMIT, SWIGLU_LIMIT)
                hid.append(glu * (1.0 / (1.0 + jnp.exp(-SWIGLU_ALPHA * glu))) * (lin + 1.0))
            hid = jnp.concatenate(hid, axis=1).astype(_bf16)
            y = jnp.dot(hid, w2b[...], preferred_element_type=_f32) + b2_ref[0]
            for c in range(ROW_TILES):
                ys_ref[rows(c), :] = y[:, c * LANES:(c + 1) * LANES]

        for n_sub in range(1, MOE_CHAINS + 1):
            @pl.when(jnp.logical_and(nv > (n_sub - 1) * hm, nv <= n_sub * hm))
            def _(n_sub=n_sub):
                for s in range(n_sub):
                    sub_block(s)


def _moe_call(block_e, n_valid, block_src, block_grp, next_e, xs, g2, w1, b1p, w2, b2):
    nblk = block_e.shape[0]
    tm = EXPERT_TILE
    by_e = lambda i, be, nv, src, grp, nxt: (be[i], 0, 0)
    by_src = lambda i, be, nv, src, grp, nxt: (src[i], 0)
    fix = lambda i, be, nv, src, grp, nxt: (0, 0)
    grid_spec = pltpu.PrefetchScalarGridSpec(
        num_scalar_prefetch=5,
        grid=(nblk,),
        in_specs=[pl.BlockSpec((tm * ROW_TILES, LANES), by_src), pl.BlockSpec((1, D_MODEL), fix),
                  pl.BlockSpec(memory_space=pl.ANY), pl.BlockSpec((1, 1, 2 * D_FF), by_e),
                  pl.BlockSpec(memory_space=pl.ANY), pl.BlockSpec((1, 1, D_MODEL), by_e)],
        out_specs=pl.BlockSpec((tm * ROW_TILES, LANES), by_src),
        scratch_shapes=[pltpu.VMEM((2, D_MODEL, 2 * D_FF), _f32), pltpu.VMEM((2, D_FF, D_MODEL), _f32),
                        pltpu.VMEM((D_MODEL, 2 * D_FF), _bf16), pltpu.VMEM((D_FF, D_MODEL), _bf16),
                        pltpu.SemaphoreType.DMA((2, 2))],
    )
    return pl.pallas_call(
        _moe_kernel,
        grid_spec=grid_spec,
        out_shape=jax.ShapeDtypeStruct(xs.shape, _f32),
        compiler_params=pltpu.CompilerParams(dimension_semantics=("arbitrary",),
                                             vmem_limit_bytes=VMEM_LIMIT),
        name="moe_experts",
    )(block_e, n_valid, block_src, block_grp, next_e, xs, g2, w1, b1p, w2, b2)


def _combine_kernel(*refs):
    dcur_ref, dnext_ref = refs[:TOP_K], refs[TOP_K:2 * TOP_K]
    x1_ref, ys_hbm, g_ref, gf_ref, o_ref, gbuf, sem = refs[2 * TOP_K:]
    tc = COMBINE_TILE
    i = pl.program_id(0)
    slot = i % 2

    def row_copy(d_ref, s, r, k):
        dst = gbuf.at[s, k, pl.ds(r * ROW_TILES, ROW_TILES)]
        return pltpu.make_async_copy(_token_row(ys_hbm, d_ref[k][r]), dst, sem.at[s])

    def issue(d_ref, s):
        def group(g, c):
            for rr in range(DMA_GROUP):
                for k in range(TOP_K):
                    row_copy(d_ref, s, g * DMA_GROUP + rr, k).start(priority=k % 2)
            return c
        lax.fori_loop(0, tc // DMA_GROUP, group, 0)

    @pl.when(i == 0)
    def _():
        issue(dcur_ref, 0)

    @pl.when(i + 1 < pl.num_programs(0))
    def _():
        issue(dnext_ref, 1 - slot)

    def wait_group(g, c):
        for _ in range(DMA_GROUP * TOP_K):
            row_copy(dcur_ref, slot, 0, 0).wait()
        return c

    lax.fori_loop(0, tc // DMA_GROUP, wait_group, 0)

    g = g_ref[...]
    acc = _load_token_rows(x1_ref, tc)
    for k in range(TOP_K):
        acc = acc + g[:, k:k + 1] * _load_token_rows(gbuf.at[slot, k], tc)
    o_ref[...] = _rms(acc, gf_ref[...])


def _combine_call(dest, x1r, ys, gates_t, gf):
    tc = COMBINE_TILE
    nt = dest.shape[1] // tc
    row = lambda i: (i, 0)
    slots = lambda tile: pl.BlockSpec((tc,), lambda i: (tile(i),), memory_space=pltpu.SMEM)
    return pl.pallas_call(
        _combine_kernel,
        grid=(nt,),
        in_specs=[slots(lambda i: i) for _ in range(TOP_K)]
                 + [slots(lambda i: jnp.minimum(i + 1, nt - 1)) for _ in range(TOP_K)]
                 + [pl.BlockSpec((tc * ROW_TILES, LANES), row), pl.BlockSpec(memory_space=pl.ANY),
                    pl.BlockSpec((tc, TOP_K), row), pl.BlockSpec((1, D_MODEL), lambda i: (0, 0))],
        out_specs=pl.BlockSpec((tc, D_MODEL), row),
        out_shape=jax.ShapeDtypeStruct((nt * tc, D_MODEL), _f32),
        scratch_shapes=[pltpu.VMEM((2, TOP_K, tc * ROW_TILES, LANES), _f32), pltpu.SemaphoreType.DMA((2,))],
        compiler_params=pltpu.CompilerParams(dimension_semantics=("arbitrary",),
                                             vmem_limit_bytes=VMEM_LIMIT),
        name="combine_norm",
    )(*([dest[k] for k in range(TOP_K)] * 2), x1r, ys, gates_t, gf)


def _plan_blocks(route_i, counts, n_tokens):
    tm = EXPERT_TILE
    nblk = (n_tokens * TOP_K + N_EXPERTS * tm) // tm
    counts = counts.astype(jnp.int32)
    blocks_e = (counts + tm - 1) // tm
    blk_end = jnp.cumsum(blocks_e)
    pad_start = (blk_end - blocks_e) * tm
    experts = jnp.arange(N_EXPERTS, dtype=jnp.int32)
    hot = route_i[:TOP_K, :, None] == experts
    dest = jnp.sum(jnp.where(hot, pad_start, 0), axis=-1) + route_i[TOP_K:]
    n_used = blk_end[-1]
    blk = jnp.arange(nblk, dtype=jnp.int32)
    src = jnp.minimum(blk, jnp.maximum(n_used - 1, 0))
    be = jnp.minimum(jnp.sum((blk_end[None, :] <= src[:, None]).astype(jnp.int32), axis=1), N_EXPERTS - 1)
    hot_b = be[:, None] == experts
    pick = lambda table: jnp.sum(jnp.where(hot_b, table, 0), axis=1)
    n_valid = jnp.where(blk < n_used, jnp.clip(pick(counts) - (src * tm - pick(pad_start)), 0, tm), 0)
    used = blocks_e > 0
    grp_e = jnp.cumsum(used.astype(jnp.int32)) - 1
    later = jnp.where((experts[None, :] > experts[:, None]) & used[None, :], experts[None, :], N_EXPERTS)
    nxt_e = jnp.min(later, axis=1)
    nxt_e = jnp.where(nxt_e < N_EXPERTS, nxt_e, -1)
    return be, n_valid.astype(jnp.int32), src, pick(grp_e), pick(nxt_e), dest


def _split_bf16(w):
    hi = w.astype(_bf16)
    return jnp.stack([hi, (w - hi.astype(_f32)).astype(_bf16)])


def kernel(x, positions, norm1_g, w_in, b_in, sinks, gn_dil, gn_swa, w_out, b_out,
           norm2_g, w_router, b_router, w_mlp1, b_mlp1, w_mlp2, b_mlp2, norm_f_g):
    batch, seq, d = x.shape
    T = batch * seq
    x2 = x.reshape(T, d)
    depth = norm1_g.shape[0]
    assert depth == 1, "the final norm is fused into the last layer's combine kernel"
    inv = ROPE_THETA ** (-jnp.arange(0, HEAD_DIM, 2, dtype=_f32) / HEAD_DIM)
    inv = jnp.tile(inv, LANES // (HEAD_DIM // 2)).reshape(1, LANES)
    pos = positions.reshape(T, 1)
    for layer in range(depth):
        qa, ka, va, qb, kb, vb = _inproj_call(
            x2, pos, inv, norm1_g[layer].reshape(1, d), w_in[layer].astype(_bf16), b_in[layer].reshape(1, D_IN))
        o_dil = _attn_call(qa, ka, va, None, batch=batch, seq=seq,
                           configs=tuple((w // r, r) for w, r in DILATED_CONFIGS), kv_group=1)
        sink_lanes = jnp.repeat(sinks[layer].astype(_f32), HEAD_DIM).reshape(1, W_SWA)
        o_swa = _attn_call(qb, kb, vb, sink_lanes, batch=batch, seq=seq,
                           configs=((SWA_WINDOW - 1, 1),), kv_group=N_HEADS_SWA // N_KV_SWA // PAIR)
        x1r, route_i, route_g, counts = _outproj_call(
            o_dil, o_swa, x2, gn_dil[layer].reshape(1, W_DIL), gn_swa[layer].reshape(1, W_SWA),
            w_out[layer].astype(_bf16), b_out[layer].reshape(1, d), norm2_g[layer].reshape(1, d),
            _split_bf16(w_router[layer].T), b_router[layer].reshape(N_EXPERTS, 1))
        block_e, n_valid, block_src, block_grp, next_e, dest = _plan_blocks(route_i, counts[:, 0], T)
        xs = _dispatch_call(dest, x1r, block_e.shape[0] * EXPERT_TILE)
        half = GLU_GROUP // 2
        b1p = b_mlp1[layer].reshape(N_EXPERTS, 2 * D_FF // GLU_GROUP, half, 2)
        b1p = jnp.swapaxes(b1p, 2, 3).reshape(N_EXPERTS, 1, 2 * D_FF)
        ys = _moe_call(block_e, n_valid, block_src, block_grp, next_e, xs, norm2_g[layer].reshape(1, d),
                       w_mlp1[layer], b1p, w_mlp2[layer], b_mlp2[layer].reshape(N_EXPERTS, 1, d))
        x2 = _combine_call(dest, x1r, ys, route_g[:TOP_K].T, norm_f_g.reshape(1, d))
    return x2.reshape(batch, seq, d)
```

```python
import functools

import jax
import jax.numpy as jnp
from jax import lax
from jax.experimental import pallas as pl
from jax.experimental.pallas import tpu as pltpu

D_MODEL = 1024
HEAD_DIM = 64
N_HEADS_DIL = 8
DILATED_CONFIGS = ((128, 1), (512, 4), (2048, 16))
N_HEADS_SWA = 8
N_KV_SWA = 2
SWA_WINDOW = 128
ATTN_BLOCK = 128
ROPE_THETA = 10000.0
N_EXPERTS = 32
TOP_K = 4
D_FF = D_MODEL
SWIGLU_ALPHA = 1.702
SWIGLU_LIMIT = 7.0
NORM_EPS = 1e-5
LOG2E = 1.4426950408889634

W_DIL = N_HEADS_DIL * HEAD_DIM
W_SWA = N_HEADS_SWA * HEAD_DIM
W_KV_SWA = N_KV_SWA * HEAD_DIM
D_IN = 3 * W_DIL + W_SWA + 2 * W_KV_SWA

LANES = 128
PAIR = LANES // HEAD_DIM
ROW_TILES = D_MODEL // LANES
TOKEN_TILE = 1024
EXPERT_TILE = 1024
DISPATCH_TILE = 2048
COMBINE_TILE = 256
DMA_GROUP = 8
ATTN_UNROLL = 32
MOE_CHAINS = 4
INPROJ_TILE = 2048
INPROJ_CHAINS = 8
OUTPROJ_CHAINS = 2
RANK_CHUNK = 256
GLU_GROUP = 256
VMEM_LIMIT = 56 * 1024 * 1024

_f32 = jnp.float32
_bf16 = jnp.bfloat16


def _rms(x, g):
    return x * lax.rsqrt(jnp.mean(x * x, axis=-1, keepdims=True) + NORM_EPS) * g


def _store_token_rows(ref, x):
    tm = x.shape[0]
    for c in range(ROW_TILES):
        ref[pl.ds(c, tm, stride=ROW_TILES), :] = x[:, c * LANES:(c + 1) * LANES]


def _load_token_rows(ref, tm):
    return jnp.concatenate([ref[pl.ds(c, tm, stride=ROW_TILES), :] for c in range(ROW_TILES)], axis=1)


def _token_row(ref, tok):
    return ref.at[pl.ds(pl.multiple_of(tok * ROW_TILES, ROW_TILES), ROW_TILES)]


def _inproj_kernel(x_ref, pos_ref, inv_ref, g_ref, w_ref, b_ref, *out_refs):
    sub = x_ref.shape[0] // INPROJ_CHAINS
    for s in range(INPROJ_CHAINS):
        rows = slice(s * sub, (s + 1) * sub)
        _inproj_rows(x_ref[rows, :], pos_ref[rows, :], inv_ref, g_ref, w_ref, b_ref,
                     [r.at[rows, :] for r in out_refs])


def _inproj_rows(x, pos, inv_ref, g_ref, w_ref, b_ref, out_refs):
    qa_ref, ka_ref, va_ref, qb_ref, kb_ref, vb_ref = out_refs
    h = _rms(x, g_ref[...]).astype(_bf16)
    proj = jnp.dot(h, w_ref[...], preferred_element_type=_f32) + b_ref[...]

    ang = pos.astype(_f32) * inv_ref[...]
    lane = lax.broadcasted_iota(jnp.int32, ang.shape, 1)
    first_half = (lane % HEAD_DIM) < (HEAD_DIM // 2)
    cos = jnp.cos(ang)
    sin = jnp.where(first_half, -jnp.sin(ang), jnp.sin(ang))

    def rope(t, scale):
        outs = []
        for c in range(t.shape[1] // LANES):
            tc = t[:, c * LANES:(c + 1) * LANES]
            partner = jnp.where(first_half, pltpu.roll(tc, LANES - HEAD_DIM // 2, 1),
                                pltpu.roll(tc, HEAD_DIM // 2, 1))
            outs.append((tc * cos + partner * sin) * scale)
        return jnp.concatenate(outs, axis=1) if len(outs) > 1 else outs[0]

    scale = HEAD_DIM ** -0.5 * LOG2E
    o = 0
    qa_ref[...] = rope(proj[:, o:o + W_DIL], scale).astype(_bf16); o += W_DIL
    ka_ref[...] = rope(proj[:, o:o + W_DIL], 1.0).astype(_bf16); o += W_DIL
    va_ref[...] = proj[:, o:o + W_DIL].astype(_bf16); o += W_DIL
    qb_ref[...] = rope(proj[:, o:o + W_SWA], scale).astype(_bf16); o += W_SWA
    kb = rope(proj[:, o:o + W_KV_SWA], 1.0); o += W_KV_SWA
    vb = proj[:, o:o + W_KV_SWA]

    def dup_heads(t):
        sw = pltpu.roll(t, HEAD_DIM, 1)
        lo = lane < HEAD_DIM
        return jnp.concatenate([jnp.where(lo, t, sw), jnp.where(lo, sw, t)], axis=1)

    kb_ref[...] = dup_heads(kb).astype(_bf16)
    vb_ref[...] = dup_heads(vb).astype(_bf16)


def _inproj_call(x2, pos, inv, g, w, b):
    T = x2.shape[0]
    tm = INPROJ_TILE
    row = lambda i: (i, 0)
    fix = lambda i: (0, 0)
    widths = (W_DIL, W_DIL, W_DIL, W_SWA, PAIR * W_KV_SWA, PAIR * W_KV_SWA)
    return pl.pallas_call(
        _inproj_kernel,
        grid=(T // tm,),
        in_specs=[pl.BlockSpec((tm, D_MODEL), row), pl.BlockSpec((tm, 1), row),
                  pl.BlockSpec((1, LANES), fix), pl.BlockSpec((1, D_MODEL), fix),
                  pl.BlockSpec((D_MODEL, D_IN), fix), pl.BlockSpec((1, D_IN), fix)],
        out_specs=[pl.BlockSpec((tm, wd), row) for wd in widths],
        out_shape=[jax.ShapeDtypeStruct((T, wd), _bf16) for wd in widths],
        compiler_params=pltpu.CompilerParams(dimension_semantics=("parallel",),
                                             vmem_limit_bytes=VMEM_LIMIT),
        name="inproj",
    )(x2, pos, inv, g, w, b)


def _attn_kernel(*refs, configs, seq, with_sink):
    if with_sink:
        q_ref, k_ref, v_ref, sink_ref, o_ref = refs[:5]
        scratch = refs[5:]
    else:
        q_ref, k_ref, v_ref, o_ref = refs[:4]
        scratch = refs[4:]
    n_cfg = len(configs)
    n_acc = n_cfg if n_cfg > 1 else 0
    obufs = scratch[:n_acc]
    mbufs = scratch[n_acc:2 * n_acc]
    lbufs = scratch[2 * n_acc:3 * n_acc]
    extra = list(scratch[3 * n_acc:])
    blk = ATTN_BLOCK
    dilated = [r for _, r in configs if r > 1]
    if dilated:
        qf, kf, vf = extra[:3]
        qf[...] = q_ref[...].astype(_f32)
        kf[...] = k_ref[...].astype(_f32)
        vf[...] = v_ref[...].astype(_f32)
        extra = extra[3:]

    lane = lax.broadcasted_iota(jnp.int32, (blk, LANES), 1)
    head0 = lane < HEAD_DIM
    lane_row = lax.broadcasted_iota(jnp.int32, (1, LANES), 1)
    head_rows = [(lane_row < HEAD_DIM).astype(_bf16), (lane_row >= HEAD_DIM).astype(_bf16)]
    qi = lax.broadcasted_iota(jnp.int32, (blk, 2 * blk), 0)
    kj = lax.broadcasted_iota(jnp.int32, (blk, 2 * blk), 1)
    dist = qi - kj + blk

    def merge(os_, ms_, ls_):
        ms_ = list(ms_)
        if with_sink:
            ms_.append(jnp.broadcast_to(sink_ref[...] * LOG2E, ms_[0].shape))
        mx = functools.reduce(jnp.maximum, ms_)
        es = [jnp.exp2(m - mx) for m in ms_]
        num = functools.reduce(lambda a, b: a + b, [e * o for e, o in zip(es, os_)])
        den = functools.reduce(lambda a, b: a + b, [e * l for e, l in zip(es, ls_)])
        if with_sink:
            den = den + es[-1]
        return num / den

    base_perm = None
    for ci, (max_dist, r) in enumerate(configs):
        nb = seq // r // blk
        band = (dist >= 0) & (dist <= max_dist)
        obuf, mbuf, lbuf = (obufs[ci], mbufs[ci], lbufs[ci]) if n_cfg > 1 else (None, None, None)
        if r == 1:
            qp, kp, vp = q_ref, k_ref, v_ref
        else:
            perm_bf16 = extra[:3]
            extra = extra[3:]
            later = [r2 for _, r2 in configs[ci + 1:] if r2 % r == 0]
            perm_f32 = []
            if later:
                perm_f32 = extra[:3]
                extra = extra[3:]
            r_src, srcs = 1, (qf, kf, vf)
            if base_perm is not None and r % base_perm[0] == 0:
                r_src, srcs = base_perm
            if perm_f32:
                base_perm = (r, perm_f32)
            qp, kp, vp = perm_bf16
            piece = 2 * blk
            step = r // r_src
            per_res = seq // r // piece
            len_src = seq // r_src

            def permute(idx, carry, r_src=r_src, srcs=srcs, step=step, per_res=per_res, len_src=len_src,
                        perm_bf16=perm_bf16, perm_f32=perm_f32):
                c = idx // per_res
                start = (c % r_src) * len_src + c // r_src + (step * piece) * (idx % per_res)
                src = pl.ds(start, piece, stride=step)
                dst = pl.ds(pl.multiple_of(idx * piece, piece), piece)
                for t in range(3):
                    val = srcs[t][src, :]
                    perm_bf16[t][dst, :] = val.astype(_bf16)
                    if perm_f32:
                        perm_f32[t][dst, :] = val
                return carry

            lax.fori_loop(0, seq // piece, permute, 0, unroll=True)

        def body(n, carry, r=r, nb=nb, band=band, obuf=obuf, lbuf=lbuf, qp=qp, kp=kp, vp=vp):
            j = n % nb
            cur = pl.ds(pl.multiple_of(n * blk, blk), blk)
            prev = pl.ds(pl.multiple_of(jnp.maximum(n - 1, 0) * blk, blk), blk)
            q = qp[cur, :]
            k2 = jnp.concatenate([kp[prev, :], kp[cur, :]], axis=0)
            v2 = jnp.concatenate([vp[prev, :], vp[cur, :]], axis=0)
            mask = band & (kj >= jnp.where(j > 0, 0, blk))
            q2 = jnp.concatenate([q * hm for hm in head_rows], axis=0)
            s2 = lax.dot_general(q2, k2, (((1,), (1,)), ((), ())), preferred_element_type=_f32)
            ps, maxs, sums = [], [], []
            for h in range(PAIR):
                s = jnp.where(mask, s2[h * blk:(h + 1) * blk], -jnp.inf)
                m = jnp.max(s, axis=-1, keepdims=True)
                p = jnp.exp2(s - m)
                ps.append(p.astype(_bf16))
                maxs.append(jnp.broadcast_to(m, (blk, LANES)))
                sums.append(jnp.broadcast_to(jnp.sum(p, axis=-1, keepdims=True), (blk, LANES)))
            o2 = jnp.dot(jnp.concatenate(ps, axis=0), v2, preferred_element_type=_f32)
            outs = [o2[:blk], o2[blk:]]
            start = n // nb + (r * blk) * j
            dst = pl.ds(pl.multiple_of(start, blk), blk) if r == 1 else pl.ds(start, blk, stride=r)
            o_pair = jnp.where(head0, outs[0], outs[1])
            m_pair = jnp.where(head0, maxs[0], maxs[1])
            l_pair = jnp.where(head0, sums[0], sums[1])
            if n_cfg == 1:
                o_ref[dst, :] = merge([o_pair], [m_pair], [l_pair])
            else:
                obuf[dst, :] = o_pair
                mbuf[dst, :] = m_pair
                lbuf[dst, :] = l_pair
            return carry

        lax.fori_loop(0, seq // blk, body, 0, unroll=ATTN_UNROLL)

    if n_cfg > 1:
        chunk = 4 * blk

        def finish(i, carry):
            sl = pl.ds(pl.multiple_of(i * chunk, chunk), chunk)
            o_ref[sl, :] = merge([b[sl, :] for b in obufs], [b[sl, :] for b in mbufs], [b[sl, :] for b in lbufs])
            return carry

        lax.fori_loop(0, seq // chunk, finish, 0)


def _attn_call(q, k, v, sinks, *, batch, seq, configs, kv_group):
    n_pairs = q.shape[1] // LANES
    with_sink = sinks is not None
    qmap = lambda b, p: (b, p)
    kvmap = lambda b, p: (b, p // kv_group)
    in_specs = [pl.BlockSpec((seq, LANES), qmap), pl.BlockSpec((seq, LANES), kvmap),
                pl.BlockSpec((seq, LANES), kvmap)]
    args = [q, k, v]
    if with_sink:
        in_specs.append(pl.BlockSpec((1, LANES), lambda b, p: (0, p)))
        args.append(sinks)
    buf = lambda dt: pltpu.VMEM((seq, LANES), dt)
    scratch = [buf(_f32) for _ in range(3 * len(configs) if len(configs) > 1 else 0)]
    if any(r > 1 for _, r in configs):
        scratch += [buf(_f32) for _ in range(3)]
    for ci, (_, r) in enumerate(configs):
        if r > 1:
            scratch += [buf(_bf16) for _ in range(3)]
            if any(r2 % r == 0 for _, r2 in configs[ci + 1:]):
                scratch += [buf(_f32) for _ in range(3)]
    return pl.pallas_call(
        functools.partial(_attn_kernel, configs=configs, seq=seq, with_sink=with_sink),
        grid=(batch, n_pairs),
        in_specs=in_specs,
        out_specs=pl.BlockSpec((seq, LANES), qmap),
        out_shape=jax.ShapeDtypeStruct(q.shape, _f32),
        scratch_shapes=scratch,
        compiler_params=pltpu.CompilerParams(dimension_semantics=("parallel", "parallel"),
                                             vmem_limit_bytes=VMEM_LIMIT),
        name="attn_sink" if with_sink else "attn_dilated",
    )(*args)


def _outproj_kernel(od_ref, os_ref, x_ref, gd_ref, gs_ref, w_ref, b_ref, g2_ref, wr_ref, br_ref,
                    x1_ref, ri_ref, rg_ref, cnt_ref, carry_ref, before_ref):
    tm = x_ref.shape[0]

    @pl.when(pl.program_id(0) == 0)
    def _():
        carry_ref[...] = jnp.zeros_like(carry_ref)
        ti = lax.broadcasted_iota(jnp.int32, (RANK_CHUNK, RANK_CHUNK), 0)
        tj = lax.broadcasted_iota(jnp.int32, (RANK_CHUNK, RANK_CHUNK), 1)
        before_ref[...] = (ti < tj).astype(_bf16)

    sub = tm // OUTPROJ_CHAINS
    h2 = []
    for s in range(OUTPROJ_CHAINS):
        rows = slice(s * sub, (s + 1) * sub)
        mixed = jnp.concatenate([_rms(od_ref[rows, :], gd_ref[...]), _rms(os_ref[rows, :], gs_ref[...])], axis=1)
        x1 = x_ref[rows, :] + jnp.dot(mixed.astype(_bf16), w_ref[...], preferred_element_type=_f32) + b_ref[...]
        _store_token_rows(x1_ref.at[pl.ds(s * sub * ROW_TILES, sub * ROW_TILES)], x1)
        h2.append(_rms(x1, g2_ref[...]))
    h2 = jnp.concatenate(h2, axis=0)
    h_hi = h2.astype(_bf16)
    h_lo = (h2 - h_hi.astype(_f32)).astype(_bf16)
    nt_dot = lambda a, b: lax.dot_general(a, b, (((1,), (1,)), ((), ())), preferred_element_type=_f32)
    logits = (nt_dot(wr_ref[0], h_hi) + nt_dot(wr_ref[0], h_lo) + nt_dot(wr_ref[1], h_hi)) + br_ref[...]
    eiota = lax.broadcasted_iota(jnp.int32, logits.shape, 0).astype(_f32)
    vals, idxs, sels = [], [], []
    l = logits
    for _ in range(TOP_K):
        m = jnp.max(l, axis=0, keepdims=True)
        idx = jnp.min(jnp.where(l == m, eiota, float(N_EXPERTS)), axis=0, keepdims=True)
        sel = eiota == idx
        vals.append(m); idxs.append(idx); sels.append(sel)
        l = jnp.where(sel, -jnp.inf, l)
    exps = [jnp.exp(v - vals[0]) for v in vals]
    den = functools.reduce(lambda a, b: a + b, exps)
    gates = [e / den for e in exps]

    chosen = functools.reduce(jnp.logical_or, sels).astype(_f32)
    carry = carry_ref[...]
    prefix = []
    for c in range(tm // RANK_CHUNK):
        part = chosen[:, c * RANK_CHUNK:(c + 1) * RANK_CHUNK]
        prefix.append(jnp.dot(part.astype(_bf16), before_ref[...], preferred_element_type=_f32) + carry)
        carry = carry + jnp.sum(part, axis=1, keepdims=True)
    prefix = jnp.concatenate(prefix, axis=1)
    ranks = [jnp.sum(jnp.where(s, prefix, 0.0), axis=0, keepdims=True).astype(jnp.int32) for s in sels]
    carry_ref[...] = carry

    ri_ref[...] = jnp.concatenate([ix.astype(jnp.int32) for ix in idxs] + ranks, axis=0)
    rg_ref[...] = jnp.concatenate(gates + gates, axis=0)
    cnt_ref[...] = jnp.broadcast_to(carry_ref[...], cnt_ref.shape)


def _outproj_call(od, osw, x2, gd, gs, w, b, g2, wr_t, br):
    T = x2.shape[0]
    tm = TOKEN_TILE
    row = lambda i: (i, 0)
    col = lambda i: (0, i)
    fix = lambda i: (0, 0)
    return pl.pallas_call(
        _outproj_kernel,
        grid=(T // tm,),
        in_specs=[pl.BlockSpec((tm, W_DIL), row), pl.BlockSpec((tm, W_SWA), row),
                  pl.BlockSpec((tm, D_MODEL), row),
                  pl.BlockSpec((1, W_DIL), fix), pl.BlockSpec((1, W_SWA), fix),
                  pl.BlockSpec((D_MODEL, D_MODEL), fix), pl.BlockSpec((1, D_MODEL), fix),
                  pl.BlockSpec((1, D_MODEL), fix),
                  pl.BlockSpec((2, N_EXPERTS, D_MODEL), lambda i: (0, 0, 0)), pl.BlockSpec((N_EXPERTS, 1), fix)],
        out_specs=[pl.BlockSpec((tm * ROW_TILES, LANES), row), pl.BlockSpec((2 * TOP_K, tm), col),
                   pl.BlockSpec((2 * TOP_K, tm), col), pl.BlockSpec((N_EXPERTS, LANES), fix)],
        out_shape=[jax.ShapeDtypeStruct((T * ROW_TILES, LANES), _f32),
                   jax.ShapeDtypeStruct((2 * TOP_K, T), jnp.int32),
                   jax.ShapeDtypeStruct((2 * TOP_K, T), _f32),
                   jax.ShapeDtypeStruct((N_EXPERTS, LANES), _f32)],
        scratch_shapes=[pltpu.VMEM((N_EXPERTS, 1), _f32), pltpu.VMEM((RANK_CHUNK, RANK_CHUNK), _bf16)],
        compiler_params=pltpu.CompilerParams(dimension_semantics=("arbitrary",),
                                             vmem_limit_bytes=VMEM_LIMIT),
        name="outproj_router",
    )(od, osw, x2, gd, gs, w, b, g2, wr_t, br)


def _dispatch_kernel(*refs):
    d_ref = refs[:TOP_K]
    x1_ref, xs_hbm, sem = refs[TOP_K:]
    td = DISPATCH_TILE

    def start_group(g, c):
        for rr in range(DMA_GROUP):
            r = g * DMA_GROUP + rr
            src = _token_row(x1_ref, r)
            for k in range(TOP_K):
                pltpu.make_async_copy(src, _token_row(xs_hbm, d_ref[k][r]), sem).start(priority=k % 2)
        return c

    lax.fori_loop(0, td // DMA_GROUP, start_group, 0)

    def wait_group(g, c):
        for _ in range(DMA_GROUP * TOP_K):
            pltpu.make_async_copy(_token_row(x1_ref, 0), _token_row(xs_hbm, 0), sem).wait()
        return c

    lax.fori_loop(0, td // DMA_GROUP, wait_group, 0)


def _dispatch_call(dest, x1r, n_slots):
    nt = dest.shape[1] // DISPATCH_TILE
    return pl.pallas_call(
        _dispatch_kernel,
        grid=(nt,),
        in_specs=[pl.BlockSpec((DISPATCH_TILE,), lambda i: (i,), memory_space=pltpu.SMEM) for _ in range(TOP_K)]
                 + [pl.BlockSpec((DISPATCH_TILE * ROW_TILES, LANES), lambda i: (i, 0))],
        out_specs=pl.BlockSpec(memory_space=pl.ANY),
        out_shape=jax.ShapeDtypeStruct((n_slots * ROW_TILES, LANES), _f32),
        scratch_shapes=[pltpu.SemaphoreType.DMA(())],
        compiler_params=pltpu.CompilerParams(dimension_semantics=("arbitrary",),
                                             vmem_limit_bytes=VMEM_LIMIT),
        name="dispatch_rows",
    )(*[dest[k] for k in range(TOP_K)], x1r)


def _moe_kernel(be_ref, nv_ref, src_ref, grp_ref, nxt_ref, xs_ref, g2_ref, w1_hbm, b1_ref, w2_hbm, b2_ref,
                ys_ref, st1, st2, w1p, w2b, sem):
    i = pl.program_id(0)
    nv = nv_ref[i]
    tm = EXPERT_TILE

    def weight_copies(e, slot):
        return (pltpu.make_async_copy(w1_hbm.at[e], st1.at[slot], sem.at[0, slot]),
                pltpu.make_async_copy(w2_hbm.at[e], st2.at[slot], sem.at[1, slot]))

    @pl.when(nv > 0)
    def _():
        first_of_expert = jnp.logical_or(i == 0, be_ref[i] != be_ref[jnp.maximum(i - 1, 0)])

        @pl.when(first_of_expert)
        def _():
            slot = grp_ref[i] % 2

            @pl.when(i == 0)
            def _():
                for c in weight_copies(be_ref[0], 0):
                    c.start()

            for c in weight_copies(be_ref[i], slot):
                c.wait()

            @pl.when(nxt_ref[i] >= 0)
            def _():
                for c in weight_copies(nxt_ref[i], 1 - slot):
                    c.start()

            src = lax.broadcasted_iota(jnp.int32, (GLU_GROUP, GLU_GROUP), 0)
            dst = lax.broadcasted_iota(jnp.int32, (GLU_GROUP, GLU_GROUP), 1)
            half = GLU_GROUP // 2
            perm = (src == jnp.where(dst < half, 2 * dst, 2 * (dst - half) + 1)).astype(_bf16)
            for g in range(2 * D_FF // GLU_GROUP):
                cols = slice(g * GLU_GROUP, (g + 1) * GLU_GROUP)
                w1p[:, cols] = jnp.dot(st1[slot, :, cols].astype(_bf16), perm,
                                       preferred_element_type=_f32).astype(_bf16)
            w2b[...] = st2[slot].astype(_bf16)

        hm = tm // MOE_CHAINS

        def sub_block(s):
            rows = lambda c: pl.ds(s * hm * ROW_TILES + c, hm, stride=ROW_TILES)
            x = jnp.concatenate([xs_ref[rows(c), :] for c in range(ROW_TILES)], axis=1)
            x = jnp.where(lax.broadcasted_iota(jnp.int32, (hm, 1), 0) < nv - s * hm, x, 0.0)
            h = _rms(x, g2_ref[...]).astype(_bf16)
            u = jnp.dot(h, w1p[...], preferred_element_type=_f32) + b1_ref[0]
            half = GLU_GROUP // 2
            hid = []
            for g in range(2 * D_FF // GLU_GROUP):
                glu = jnp.minimum(u[:, g * GLU_GROUP:g * GLU_GROUP + half], SWIGLU_LIMIT)
                lin = jnp.clip(u[:, g * GLU_GROUP + half:(g + 1) * GLU_GROUP], -SWIGLU_LIMIT, SWIGLU_LIMIT)
                hid.append(glu * (1.0 / (1.0 + jnp.exp(-SWIGLU_ALPHA * glu))) * (lin + 1.0))
            hid = jnp.concatenate(hid, axis=1).astype(_bf16)
            y = jnp.dot(hid, w2b[...], preferred_element_type=_f32) + b2_ref[0]
            for c in range(ROW_TILES):
                ys_ref[rows(c), :] = y[:, c * LANES:(c + 1) * LANES]

        for n_sub in range(1, MOE_CHAINS + 1):
            @pl.when(jnp.logical_and(nv > (n_sub - 1) * hm, nv <= n_sub * hm))
            def _(n_sub=n_sub):
                for s in range(n_sub):
                    sub_block(s)


def _moe_call(block_e, n_valid, block_src, block_grp, next_e, xs, g2, w1, b1p, w2, b2):
    nblk = block_e.shape[0]
    tm = EXPERT_TILE
    by_e = lambda i, be, nv, src, grp, nxt: (be[i], 0, 0)
    by_src = lambda i, be, nv, src, grp, nxt: (src[i], 0)
    fix = lambda i, be, nv, src, grp, nxt: (0, 0)
    grid_spec = pltpu.PrefetchScalarGridSpec(
        num_scalar_prefetch=5,
        grid=(nblk,),
        in_specs=[pl.BlockSpec((tm * ROW_TILES, LANES), by_src), pl.BlockSpec((1, D_MODEL), fix),
                  pl.BlockSpec(memory_space=pl.ANY), pl.BlockSpec((1, 1, 2 * D_FF), by_e),
                  pl.BlockSpec(memory_space=pl.ANY), pl.BlockSpec((1, 1, D_MODEL), by_e)],
        out_specs=pl.BlockSpec((tm * ROW_TILES, LANES), by_src),
        scratch_shapes=[pltpu.VMEM((2, D_MODEL, 2 * D_FF), _f32), pltpu.VMEM((2, D_FF, D_MODEL), _f32),
                        pltpu.VMEM((D_MODEL, 2 * D_FF), _bf16), pltpu.VMEM((D_FF, D_MODEL), _bf16),
                        pltpu.SemaphoreType.DMA((2, 2))],
    )
    return pl.pallas_call(
        _moe_kernel,
        grid_spec=grid_spec,
        out_shape=jax.ShapeDtypeStruct(xs.shape, _f32),
        compiler_params=pltpu.CompilerParams(dimension_semantics=("arbitrary",),
                                             vmem_limit_bytes=VMEM_LIMIT),
        name="moe_experts",
    )(block_e, n_valid, block_src, block_grp, next_e, xs, g2, w1, b1p, w2, b2)


def _combine_kernel(*refs):
    dcur_ref, dnext_ref = refs[:TOP_K], refs[TOP_K:2 * TOP_K]
    x1_ref, ys_hbm, g_ref, gf_ref, o_ref, gbuf, sem = refs[2 * TOP_K:]
    tc = COMBINE_TILE
    i = pl.program_id(0)
    slot = i % 2

    def row_copy(d_ref, s, r, k):
        dst = gbuf.at[s, k, pl.ds(r * ROW_TILES, ROW_TILES)]
        return pltpu.make_async_copy(_token_row(ys_hbm, d_ref[k][r]), dst, sem.at[s])

    def issue(d_ref, s):
        def group(g, c):
            for rr in range(DMA_GROUP):
                for k in range(TOP_K):
                    row_copy(d_ref, s, g * DMA_GROUP + rr, k).start(priority=k % 2)
            return c
        lax.fori_loop(0, tc // DMA_GROUP, group, 0)

    @pl.when(i == 0)
    def _():
        issue(dcur_ref, 0)

    @pl.when(i + 1 < pl.num_programs(0))
    def _():
        issue(dnext_ref, 1 - slot)

    def wait_group(g, c):
        for _ in range(DMA_GROUP * TOP_K):
            row_copy(dcur_ref, slot, 0, 0).wait()
        return c

    lax.fori_loop(0, tc // DMA_GROUP, wait_group, 0)

    g = g_ref[...]
    acc = _load_token_rows(x1_ref, tc)
    for k in range(TOP_K):
        acc = acc + g[:, k:k + 1] * _load_token_rows(gbuf.at[slot, k], tc)
    o_ref[...] = _rms(acc, gf_ref[...])


def _combine_call(dest, x1r, ys, gates_t, gf):
    tc = COMBINE_TILE
    nt = dest.shape[1] // tc
    row = lambda i: (i, 0)
    slots = lambda tile: pl.BlockSpec((tc,), lambda i: (tile(i),), memory_space=pltpu.SMEM)
    return pl.pallas_call(
        _combine_kernel,
        grid=(nt,),
        in_specs=[slots(lambda i: i) for _ in range(TOP_K)]
                 + [slots(lambda i: jnp.minimum(i + 1, nt - 1)) for _ in range(TOP_K)]
                 + [pl.BlockSpec((tc * ROW_TILES, LANES), row), pl.BlockSpec(memory_space=pl.ANY),
                    pl.BlockSpec((tc, TOP_K), row), pl.BlockSpec((1, D_MODEL), lambda i: (0, 0))],
        out_specs=pl.BlockSpec((tc, D_MODEL), row),
        out_shape=jax.ShapeDtypeStruct((nt * tc, D_MODEL), _f32),
        scratch_shapes=[pltpu.VMEM((2, TOP_K, tc * ROW_TILES, LANES), _f32), pltpu.SemaphoreType.DMA((2,))],
        compiler_params=pltpu.CompilerParams(dimension_semantics=("arbitrary",),
                                             vmem_limit_bytes=VMEM_LIMIT),
        name="combine_norm",
    )(*([dest[k] for k in range(TOP_K)] * 2), x1r, ys, gates_t, gf)


def _plan_blocks(route_i, counts, n_tokens):
    tm = EXPERT_TILE
    nblk = (n_tokens * TOP_K + N_EXPERTS * tm) // tm
    counts = counts.astype(jnp.int32)
    blocks_e = (counts + tm - 1) // tm
    blk_end = jnp.cumsum(blocks_e)
    pad_start = (blk_end - blocks_e) * tm
    experts = jnp.arange(N_EXPERTS, dtype=jnp.int32)
    hot = route_i[:TOP_K, :, None] == experts
    dest = jnp.sum(jnp.where(hot, pad_start, 0), axis=-1) + route_i[TOP_K:]
    n_used = blk_end[-1]
    blk = jnp.arange(nblk, dtype=jnp.int32)
    src = jnp.minimum(blk, jnp.maximum(n_used - 1, 0))
    be = jnp.minimum(jnp.sum((blk_end[None, :] <= src[:, None]).astype(jnp.int32), axis=1), N_EXPERTS - 1)
    hot_b = be[:, None] == experts
    pick = lambda table: jnp.sum(jnp.where(hot_b, table, 0), axis=1)
    n_valid = jnp.where(blk < n_used, jnp.clip(pick(counts) - (src * tm - pick(pad_start)), 0, tm), 0)
    used = blocks_e > 0
    grp_e = jnp.cumsum(used.astype(jnp.int32)) - 1
    later = jnp.where((experts[None, :] > experts[:, None]) & used[None, :], experts[None, :], N_EXPERTS)
    nxt_e = jnp.min(later, axis=1)
    nxt_e = jnp.where(nxt_e < N_EXPERTS, nxt_e, -1)
    return be, n_valid.astype(jnp.int32), src, pick(grp_e), pick(nxt_e), dest


def _split_bf16(w):
    hi = w.astype(_bf16)
    return jnp.stack([hi, (w - hi.astype(_f32)).astype(_bf16)])


def kernel(x, positions, norm1_g, w_in, b_in, sinks, gn_dil, gn_swa, w_out, b_out,
           norm2_g, w_router, b_router, w_mlp1, b_mlp1, w_mlp2, b_mlp2, norm_f_g):
    batch, seq, d = x.shape
    T = batch * seq
    x2 = x.reshape(T, d)
    depth = norm1_g.shape[0]
    assert depth == 1, "the final norm is fused into the last layer's combine kernel"
    inv = ROPE_THETA ** (-jnp.arange(0, HEAD_DIM, 2, dtype=_f32) / HEAD_DIM)
    inv = jnp.tile(inv, LANES // (HEAD_DIM // 2)).reshape(1, LANES)
    pos = positions.reshape(T, 1)
    for layer in range(depth):
        qa, ka, va, qb, kb, vb = _inproj_call(
            x2, pos, inv, norm1_g[layer].reshape(1, d), w_in[layer].astype(_bf16), b_in[layer].reshape(1, D_IN))
        o_dil = _attn_call(qa, ka, va, None, batch=batch, seq=seq,
                           configs=tuple((w // r, r) for w, r in DILATED_CONFIGS), kv_group=1)
        sink_lanes = jnp.repeat(sinks[layer].astype(_f32), HEAD_DIM).reshape(1, W_SWA)
        o_swa = _attn_call(qb, kb, vb, sink_lanes, batch=batch, seq=seq,
                           configs=((SWA_WINDOW - 1, 1),), kv_group=N_HEADS_SWA // N_KV_SWA // PAIR)
        x1r, route_i, route_g, counts = _outproj_call(
            o_dil, o_swa, x2, gn_dil[layer].reshape(1, W_DIL), gn_swa[layer].reshape(1, W_SWA),
            w_out[layer].astype(_bf16), b_out[layer].reshape(1, d), norm2_g[layer].reshape(1, d),
            _split_bf16(w_router[layer].T), b_router[layer].reshape(N_EXPERTS, 1))
        block_e, n_valid, block_src, block_grp, next_e, dest = _plan_blocks(route_i, counts[:, 0], T)
        xs = _dispatch_call(dest, x1r, block_e.shape[0] * EXPERT_TILE)
        half = GLU_GROUP // 2
        b1p = b_mlp1[layer].reshape(N_EXPERTS, 2 * D_FF // GLU_GROUP, half, 2)
        b1p = jnp.swapaxes(b1p, 2, 3).reshape(N_EXPERTS, 1, 2 * D_FF)
        ys = _moe_call(block_e, n_valid, block_src, block_grp, next_e, xs, norm2_g[layer].reshape(1, d),
                       w_mlp1[layer], b1p, w_mlp2[layer], b_mlp2[layer].reshape(N_EXPERTS, 1, d))
        x2 = _combine_call(dest, x1r, ys, route_g[:TOP_K].T, norm_f_g.reshape(1, d))
    return x2.reshape(batch, seq, d)
```

```python
import functools

import jax
import jax.numpy as jnp
from jax import lax
from jax.experimental import pallas as pl
from jax.experimental.pallas import tpu as pltpu

D_MODEL = 1024
HEAD_DIM = 64
N_HEADS_DIL = 8
DILATED_CONFIGS = ((128, 1), (512, 4), (2048, 16))
N_HEADS_SWA = 8
N_KV_SWA = 2
SWA_WINDOW = 128
ATTN_BLOCK = 128
ROPE_THETA = 10000.0
N_EXPERTS = 32
TOP_K = 4
D_FF = D_MODEL
SWIGLU_ALPHA = 1.702
SWIGLU_LIMIT = 7.0
NORM_EPS = 1e-5
LOG2E = 1.4426950408889634

W_DIL = N_HEADS_DIL * HEAD_DIM
W_SWA = N_HEADS_SWA * HEAD_DIM
W_KV_SWA = N_KV_SWA * HEAD_DIM
D_IN = 3 * W_DIL + W_SWA + 2 * W_KV_SWA

LANES = 128
PAIR = LANES // HEAD_DIM
ROW_TILES = D_MODEL // LANES
TOKEN_TILE = 1024
EXPERT_TILE = 1024
DISPATCH_TILE = 4096
COMBINE_TILE = 256
DMA_GROUP = 8
ATTN_UNROLL = 32
MOE_CHAINS = 4
INPROJ_TILE = 2048
INPROJ_CHAINS = 8
OUTPROJ_CHAINS = 2
RANK_CHUNK = 256
GLU_GROUP = 256
VMEM_LIMIT = 56 * 1024 * 1024

_f32 = jnp.float32
_bf16 = jnp.bfloat16


def _rms(x, g):
    return x * lax.rsqrt(jnp.mean(x * x, axis=-1, keepdims=True) + NORM_EPS) * g


def _store_token_rows(ref, x):
    tm = x.shape[0]
    for c in range(ROW_TILES):
        ref[pl.ds(c, tm, stride=ROW_TILES), :] = x[:, c * LANES:(c + 1) * LANES]


def _load_token_rows(ref, tm):
    return jnp.concatenate([ref[pl.ds(c, tm, stride=ROW_TILES), :] for c in range(ROW_TILES)], axis=1)


def _token_row(ref, tok):
    return ref.at[pl.ds(pl.multiple_of(tok * ROW_TILES, ROW_TILES), ROW_TILES)]


def _inproj_kernel(x_ref, pos_ref, inv_ref, g_ref, w_ref, b_ref, *out_refs):
    sub = x_ref.shape[0] // INPROJ_CHAINS
    for s in range(INPROJ_CHAINS):
        rows = slice(s * sub, (s + 1) * sub)
        _inproj_rows(x_ref[rows, :], pos_ref[rows, :], inv_ref, g_ref, w_ref, b_ref,
                     [r.at[rows, :] for r in out_refs])


def _inproj_rows(x, pos, inv_ref, g_ref, w_ref, b_ref, out_refs):
    qa_ref, ka_ref, va_ref, qb_ref, kb_ref, vb_ref = out_refs
    h = _rms(x, g_ref[...]).astype(_bf16)
    proj = jnp.dot(h, w_ref[...], preferred_element_type=_f32) + b_ref[...]

    ang = pos.astype(_f32) * inv_ref[...]
    lane = lax.broadcasted_iota(jnp.int32, ang.shape, 1)
    first_half = (lane % HEAD_DIM) < (HEAD_DIM // 2)
    cos = jnp.cos(ang)
    sin = jnp.where(first_half, -jnp.sin(ang), jnp.sin(ang))

    def rope(t, scale):
        outs = []
        for c in range(t.shape[1] // LANES):
            tc = t[:, c * LANES:(c + 1) * LANES]
            partner = jnp.where(first_half, pltpu.roll(tc, LANES - HEAD_DIM // 2, 1),
                                pltpu.roll(tc, HEAD_DIM // 2, 1))
            outs.append((tc * cos + partner * sin) * scale)
        return jnp.concatenate(outs, axis=1) if len(outs) > 1 else outs[0]

    scale = HEAD_DIM ** -0.5 * LOG2E
    o = 0
    qa_ref[...] = rope(proj[:, o:o + W_DIL], scale).astype(_bf16); o += W_DIL
    ka_ref[...] = rope(proj[:, o:o + W_DIL], 1.0).astype(_bf16); o += W_DIL
    va_ref[...] = proj[:, o:o + W_DIL].astype(_bf16); o += W_DIL
    qb_ref[...] = rope(proj[:, o:o + W_SWA], scale).astype(_bf16); o += W_SWA
    kb = rope(proj[:, o:o + W_KV_SWA], 1.0); o += W_KV_SWA
    vb = proj[:, o:o + W_KV_SWA]

    def dup_heads(t):
        sw = pltpu.roll(t, HEAD_DIM, 1)
        lo = lane < HEAD_DIM
        return jnp.concatenate([jnp.where(lo, t, sw), jnp.where(lo, sw, t)], axis=1)

    kb_ref[...] = dup_heads(kb).astype(_bf16)
    vb_ref[...] = dup_heads(vb).astype(_bf16)


def _inproj_call(x2, pos, inv, g, w, b):
    T = x2.shape[0]
    tm = INPROJ_TILE
    row = lambda i: (i, 0)
    fix = lambda i: (0, 0)
    widths = (W_DIL, W_DIL, W_DIL, W_SWA, PAIR * W_KV_SWA, PAIR * W_KV_SWA)
    return pl.pallas_call(
        _inproj_kernel,
        grid=(T // tm,),
        in_specs=[pl.BlockSpec((tm, D_MODEL), row), pl.BlockSpec((tm, 1), row),
                  pl.BlockSpec((1, LANES), fix), pl.BlockSpec((1, D_MODEL), fix),
                  pl.BlockSpec((D_MODEL, D_IN), fix), pl.BlockSpec((1, D_IN), fix)],
        out_specs=[pl.BlockSpec((tm, wd), row) for wd in widths],
        out_shape=[jax.ShapeDtypeStruct((T, wd), _bf16) for wd in widths],
        compiler_params=pltpu.CompilerParams(dimension_semantics=("parallel",),
                                             vmem_limit_bytes=VMEM_LIMIT),
        name="inproj",
    )(x2, pos, inv, g, w, b)


def _attn_kernel(*refs, configs, seq, with_sink):
    if with_sink:
        q_ref, k_ref, v_ref, sink_ref, o_ref = refs[:5]
        scratch = refs[5:]
    else:
        q_ref, k_ref, v_ref, o_ref = refs[:4]
        scratch = refs[4:]
    n_cfg = len(configs)
    n_acc = n_cfg if n_cfg > 1 else 0
    obufs = scratch[:n_acc]
    mbufs = scratch[n_acc:2 * n_acc]
    lbufs = scratch[2 * n_acc:3 * n_acc]
    extra = list(scratch[3 * n_acc:])
    blk = ATTN_BLOCK
    dilated = [r for _, r in configs if r > 1]
    if dilated:
        qf, kf, vf = extra[:3]
        qf[...] = q_ref[...].astype(_f32)
        kf[...] = k_ref[...].astype(_f32)
        vf[...] = v_ref[...].astype(_f32)
        extra = extra[3:]

    lane = lax.broadcasted_iota(jnp.int32, (blk, LANES), 1)
    head0 = lane < HEAD_DIM
    lane_row = lax.broadcasted_iota(jnp.int32, (1, LANES), 1)
    head_rows = [(lane_row < HEAD_DIM).astype(_bf16), (lane_row >= HEAD_DIM).astype(_bf16)]
    qi = lax.broadcasted_iota(jnp.int32, (blk, 2 * blk), 0)
    kj = lax.broadcasted_iota(jnp.int32, (blk, 2 * blk), 1)
    dist = qi - kj + blk

    def merge(os_, ms_, ls_):
        ms_ = list(ms_)
        if with_sink:
            ms_.append(jnp.broadcast_to(sink_ref[...] * LOG2E, ms_[0].shape))
        mx = functools.reduce(jnp.maximum, ms_)
        es = [jnp.exp2(m - mx) for m in ms_]
        num = functools.reduce(lambda a, b: a + b, [e * o for e, o in zip(es, os_)])
        den = functools.reduce(lambda a, b: a + b, [e * l for e, l in zip(es, ls_)])
        if with_sink:
            den = den + es[-1]
        return num / den

    base_perm = None
    for ci, (max_dist, r) in enumerate(configs):
        nb = seq // r // blk
        band = (dist >= 0) & (dist <= max_dist)
        obuf, mbuf, lbuf = (obufs[ci], mbufs[ci], lbufs[ci]) if n_cfg > 1 else (None, None, None)
        if r == 1:
            qp, kp, vp = q_ref, k_ref, v_ref
        else:
            perm_bf16 = extra[:3]
            extra = extra[3:]
            later = [r2 for _, r2 in configs[ci + 1:] if r2 % r == 0]
            perm_f32 = []
            if later:
                perm_f32 = extra[:3]
                extra = extra[3:]
            r_src, srcs = 1, (qf, kf, vf)
            if base_perm is not None and r % base_perm[0] == 0:
                r_src, srcs = base_perm
            if perm_f32:
                base_perm = (r, perm_f32)
            qp, kp, vp = perm_bf16
            piece = 2 * blk
            step = r // r_src
            per_res = seq // r // piece
            len_src = seq // r_src

            def permute(idx, carry, r_src=r_src, srcs=srcs, step=step, per_res=per_res, len_src=len_src,
                        perm_bf16=perm_bf16, perm_f32=perm_f32):
                c = idx // per_res
                start = (c % r_src) * len_src + c // r_src + (step * piece) * (idx % per_res)
                src = pl.ds(start, piece, stride=step)
                dst = pl.ds(pl.multiple_of(idx * piece, piece), piece)
                for t in range(3):
                    val = srcs[t][src, :]
                    perm_bf16[t][dst, :] = val.astype(_bf16)
                    if perm_f32:
                        perm_f32[t][dst, :] = val
                return carry

            lax.fori_loop(0, seq // piece, permute, 0, unroll=True)

        def body(n, carry, r=r, nb=nb, band=band, obuf=obuf, lbuf=lbuf, qp=qp, kp=kp, vp=vp):
            j = n % nb
            cur = pl.ds(pl.multiple_of(n * blk, blk), blk)
            prev = pl.ds(pl.multiple_of(jnp.maximum(n - 1, 0) * blk, blk), blk)
            q = qp[cur, :]
            k2 = jnp.concatenate([kp[prev, :], kp[cur, :]], axis=0)
            v2 = jnp.concatenate([vp[prev, :], vp[cur, :]], axis=0)
            mask = band & (kj >= jnp.where(j > 0, 0, blk))
            q2 = jnp.concatenate([q * hm for hm in head_rows], axis=0)
            s2 = lax.dot_general(q2, k2, (((1,), (1,)), ((), ())), preferred_element_type=_f32)
            ps, maxs, sums = [], [], []
            for h in range(PAIR):
                s = jnp.where(mask, s2[h * blk:(h + 1) * blk], -jnp.inf)
                m = jnp.max(s, axis=-1, keepdims=True)
                p = jnp.exp2(s - m)
                ps.append(p.astype(_bf16))
                maxs.append(jnp.broadcast_to(m, (blk, LANES)))
                sums.append(jnp.broadcast_to(jnp.sum(p, axis=-1, keepdims=True), (blk, LANES)))
            o2 = jnp.dot(jnp.concatenate(ps, axis=0), v2, preferred_element_type=_f32)
            outs = [o2[:blk], o2[blk:]]
            start = n // nb + (r * blk) * j
            dst = pl.ds(pl.multiple_of(start, blk), blk) if r == 1 else pl.ds(start, blk, stride=r)
            o_pair = jnp.where(head0, outs[0], outs[1])
            m_pair = jnp.where(head0, maxs[0], maxs[1])
            l_pair = jnp.where(head0, sums[0], sums[1])
            if n_cfg == 1:
                o_ref[dst, :] = merge([o_pair], [m_pair], [l_pair])
            else:
                obuf[dst, :] = o_pair
                mbuf[dst, :] = m_pair
                lbuf[dst, :] = l_pair
            return carry

        lax.fori_loop(0, seq // blk, body, 0, unroll=ATTN_UNROLL)

    if n_cfg > 1:
        chunk = 4 * blk

        def finish(i, carry):
            sl = pl.ds(pl.multiple_of(i * chunk, chunk), chunk)
            o_ref[sl, :] = merge([b[sl, :] for b in obufs], [b[sl, :] for b in mbufs], [b[sl, :] for b in lbufs])
            return carry

        lax.fori_loop(0, seq // chunk, finish, 0)


def _attn_call(q, k, v, sinks, *, batch, seq, configs, kv_group):
    n_pairs = q.shape[1] // LANES
    with_sink = sinks is not None
    qmap = lambda b, p: (b, p)
    kvmap = lambda b, p: (b, p // kv_group)
    in_specs = [pl.BlockSpec((seq, LANES), qmap), pl.BlockSpec((seq, LANES), kvmap),
                pl.BlockSpec((seq, LANES), kvmap)]
    args = [q, k, v]
    if with_sink:
        in_specs.append(pl.BlockSpec((1, LANES), lambda b, p: (0, p)))
        args.append(sinks)
    buf = lambda dt: pltpu.VMEM((seq, LANES), dt)
    scratch = [buf(_f32) for _ in range(3 * len(configs) if len(configs) > 1 else 0)]
    if any(r > 1 for _, r in configs):
        scratch += [buf(_f32) for _ in range(3)]
    for ci, (_, r) in enumerate(configs):
        if r > 1:
            scratch += [buf(_bf16) for _ in range(3)]
            if any(r2 % r == 0 for _, r2 in configs[ci + 1:]):
                scratch += [buf(_f32) for _ in range(3)]
    return pl.pallas_call(
        functools.partial(_attn_kernel, configs=configs, seq=seq, with_sink=with_sink),
        grid=(batch, n_pairs),
        in_specs=in_specs,
        out_specs=pl.BlockSpec((seq, LANES), qmap),
        out_shape=jax.ShapeDtypeStruct(q.shape, _f32),
        scratch_shapes=scratch,
        compiler_params=pltpu.CompilerParams(dimension_semantics=("parallel", "parallel"),
                                             vmem_limit_bytes=VMEM_LIMIT),
        name="attn_sink" if with_sink else "attn_dilated",
    )(*args)


def _outproj_kernel(od_ref, os_ref, x_ref, gd_ref, gs_ref, w_ref, b_ref, g2_ref, wr_ref, br_ref,
                    x1_ref, ri_ref, rg_ref, cnt_ref, carry_ref, before_ref):
    tm = x_ref.shape[0]

    @pl.when(pl.program_id(0) == 0)
    def _():
        carry_ref[...] = jnp.zeros_like(carry_ref)
        ti = lax.broadcasted_iota(jnp.int32, (RANK_CHUNK, RANK_CHUNK), 0)
        tj = lax.broadcasted_iota(jnp.int32, (RANK_CHUNK, RANK_CHUNK), 1)
        before_ref[...] = (ti < tj).astype(_bf16)

    sub = tm // OUTPROJ_CHAINS
    h2 = []
    for s in range(OUTPROJ_CHAINS):
        rows = slice(s * sub, (s + 1) * sub)
        mixed = jnp.concatenate([_rms(od_ref[rows, :], gd_ref[...]), _rms(os_ref[rows, :], gs_ref[...])], axis=1)
        x1 = x_ref[rows, :] + jnp.dot(mixed.astype(_bf16), w_ref[...], preferred_element_type=_f32) + b_ref[...]
        _store_token_rows(x1_ref.at[pl.ds(s * sub * ROW_TILES, sub * ROW_TILES)], x1)
        h2.append(_rms(x1, g2_ref[...]))
    h2 = jnp.concatenate(h2, axis=0)
    h_hi = h2.astype(_bf16)
    h_lo = (h2 - h_hi.astype(_f32)).astype(_bf16)
    nt_dot = lambda a, b: lax.dot_general(a, b, (((1,), (1,)), ((), ())), preferred_element_type=_f32)
    logits = (nt_dot(wr_ref[0], h_hi) + nt_dot(wr_ref[0], h_lo) + nt_dot(wr_ref[1], h_hi)) + br_ref[...]
    eiota = lax.broadcasted_iota(jnp.int32, logits.shape, 0).astype(_f32)
    vals, idxs, sels = [], [], []
    l = logits
    for _ in range(TOP_K):
        m = jnp.max(l, axis=0, keepdims=True)
        idx = jnp.min(jnp.where(l == m, eiota, float(N_EXPERTS)), axis=0, keepdims=True)
        sel = eiota == idx
        vals.append(m); idxs.append(idx); sels.append(sel)
        l = jnp.where(sel, -jnp.inf, l)
    exps = [jnp.exp(v - vals[0]) for v in vals]
    den = functools.reduce(lambda a, b: a + b, exps)
    gates = [e / den for e in exps]

    chosen = functools.reduce(jnp.logical_or, sels).astype(_f32)
    carry = carry_ref[...]
    prefix = []
    for c in range(tm // RANK_CHUNK):
        part = chosen[:, c * RANK_CHUNK:(c + 1) * RANK_CHUNK]
        prefix.append(jnp.dot(part.astype(_bf16), before_ref[...], preferred_element_type=_f32) + carry)
        carry = carry + jnp.sum(part, axis=1, keepdims=True)
    prefix = jnp.concatenate(prefix, axis=1)
    ranks = [jnp.sum(jnp.where(s, prefix, 0.0), axis=0, keepdims=True).astype(jnp.int32) for s in sels]
    carry_ref[...] = carry

    ri_ref[...] = jnp.concatenate([ix.astype(jnp.int32) for ix in idxs] + ranks, axis=0)
    rg_ref[...] = jnp.concatenate(gates + gates, axis=0)
    cnt_ref[...] = jnp.broadcast_to(carry_ref[...], cnt_ref.shape)


def _outproj_call(od, osw, x2, gd, gs, w, b, g2, wr_t, br):
    T = x2.shape[0]
    tm = TOKEN_TILE
    row = lambda i: (i, 0)
    col = lambda i: (0, i)
    fix = lambda i: (0, 0)
    return pl.pallas_call(
        _outproj_kernel,
        grid=(T // tm,),
        in_specs=[pl.BlockSpec((tm, W_DIL), row), pl.BlockSpec((tm, W_SWA), row),
                  pl.BlockSpec((tm, D_MODEL), row),
                  pl.BlockSpec((1, W_DIL), fix), pl.BlockSpec((1, W_SWA), fix),
                  pl.BlockSpec((D_MODEL, D_MODEL), fix), pl.BlockSpec((1, D_MODEL), fix),
                  pl.BlockSpec((1, D_MODEL), fix),
                  pl.BlockSpec((2, N_EXPERTS, D_MODEL), lambda i: (0, 0, 0)), pl.BlockSpec((N_EXPERTS, 1), fix)],
        out_specs=[pl.BlockSpec((tm * ROW_TILES, LANES), row), pl.BlockSpec((2 * TOP_K, tm), col),
                   pl.BlockSpec((2 * TOP_K, tm), col), pl.BlockSpec((N_EXPERTS, LANES), fix)],
        out_shape=[jax.ShapeDtypeStruct((T * ROW_TILES, LANES), _f32),
                   jax.ShapeDtypeStruct((2 * TOP_K, T), jnp.int32),
                   jax.ShapeDtypeStruct((2 * TOP_K, T), _f32),
                   jax.ShapeDtypeStruct((N_EXPERTS, LANES), _f32)],
        scratch_shapes=[pltpu.VMEM((N_EXPERTS, 1), _f32), pltpu.VMEM((RANK_CHUNK, RANK_CHUNK), _bf16)],
        compiler_params=pltpu.CompilerParams(dimension_semantics=("arbitrary",),
                                             vmem_limit_bytes=VMEM_LIMIT),
        name="outproj_router",
    )(od, osw, x2, gd, gs, w, b, g2, wr_t, br)


def _dispatch_kernel(*refs):
    d_ref = refs[:TOP_K]
    x1_ref, xs_hbm, sem = refs[TOP_K:]
    td = DISPATCH_TILE

    def start_group(g, c):
        for rr in range(DMA_GROUP):
            r = g * DMA_GROUP + rr
            src = _token_row(x1_ref, r)
            for k in range(TOP_K):
                pltpu.make_async_copy(src, _token_row(xs_hbm, d_ref[k][r]), sem).start(priority=k % 2)
        return c

    lax.fori_loop(0, td // DMA_GROUP, start_group, 0)

    def wait_group(g, c):
        for _ in range(DMA_GROUP * TOP_K):
            pltpu.make_async_copy(_token_row(x1_ref, 0), _token_row(xs_hbm, 0), sem).wait()
        return c

    lax.fori_loop(0, td // DMA_GROUP, wait_group, 0)


def _dispatch_call(dest, x1r, n_slots):
    nt = dest.shape[1] // DISPATCH_TILE
    return pl.pallas_call(
        _dispatch_kernel,
        grid=(nt,),
        in_specs=[pl.BlockSpec((DISPATCH_TILE,), lambda i: (i,), memory_space=pltpu.SMEM) for _ in range(TOP_K)]
                 + [pl.BlockSpec((DISPATCH_TILE * ROW_TILES, LANES), lambda i: (i, 0))],
        out_specs=pl.BlockSpec(memory_space=pl.ANY),
        out_shape=jax.ShapeDtypeStruct((n_slots * ROW_TILES, LANES), _f32),
        scratch_shapes=[pltpu.SemaphoreType.DMA(())],
        compiler_params=pltpu.CompilerParams(dimension_semantics=("arbitrary",),
                                             vmem_limit_bytes=VMEM_LIMIT),
        name="dispatch_rows",
    )(*[dest[k] for k in range(TOP_K)], x1r)


def _moe_kernel(be_ref, nv_ref, src_ref, grp_ref, nxt_ref, xs_ref, g2_ref, w1_hbm, b1_ref, w2_hbm, b2_ref,
                ys_ref, st1, st2, w1p, w2b, sem):
    i = pl.program_id(0)
    nv = nv_ref[i]
    tm = EXPERT_TILE

    def weight_copies(e, slot):
        return (pltpu.make_async_copy(w1_hbm.at[e], st1.at[slot], sem.at[0, slot]),
                pltpu.make_async_copy(w2_hbm.at[e], st2.at[slot], sem.at[1, slot]))

    @pl.when(nv > 0)
    def _():
        first_of_expert = jnp.logical_or(i == 0, be_ref[i] != be_ref[jnp.maximum(i - 1, 0)])

        @pl.when(first_of_expert)
        def _():
            slot = grp_ref[i] % 2

            @pl.when(i == 0)
            def _():
                for c in weight_copies(be_ref[0], 0):
                    c.start()

            for c in weight_copies(be_ref[i], slot):
                c.wait()

            @pl.when(nxt_ref[i] >= 0)
            def _():
                for c in weight_copies(nxt_ref[i], 1 - slot):
                    c.start()

            src = lax.broadcasted_iota(jnp.int32, (GLU_GROUP, GLU_GROUP), 0)
            dst = lax.broadcasted_iota(jnp.int32, (GLU_GROUP, GLU_GROUP), 1)
            half = GLU_GROUP // 2
            perm = (src == jnp.where(dst < half, 2 * dst, 2 * (dst - half) + 1)).astype(_bf16)
            for g in range(2 * D_FF // GLU_GROUP):
                cols = slice(g * GLU_GROUP, (g + 1) * GLU_GROUP)
                w1p[:, cols] = jnp.dot(st1[slot, :, cols].astype(_bf16), perm,
                                       preferred_element_type=_f32).astype(_bf16)
            w2b[...] = st2[slot].astype(_bf16)

        hm = tm // MOE_CHAINS

        def sub_block(s):
            rows = lambda c: pl.ds(s * hm * ROW_TILES + c, hm, stride=ROW_TILES)
            x = jnp.concatenate([xs_ref[rows(c), :] for c in range(ROW_TILES)], axis=1)
            x = jnp.where(lax.broadcasted_iota(jnp.int32, (hm, 1), 0) < nv - s * hm, x, 0.0)
            h = _rms(x, g2_ref[...]).astype(_bf16)
            u = jnp.dot(h, w1p[...], preferred_element_type=_f32) + b1_ref[0]
            half = GLU_GROUP // 2
            hid = []
            for g in range(2 * D_FF // GLU_GROUP):
                glu = jnp.minimum(u[:, g * GLU_GROUP:g * GLU_GROUP + half], SWIGLU_LIMIT)
                lin = jnp.clip(u[:, g * GLU_GROUP + half:(g + 1) * GLU_GROUP], -SWIGLU_LIMIT, SWIGLU_LIMIT)
                hid.append(glu * (1.0 / (1.0 + jnp.exp(-SWIGLU_ALPHA * glu))) * (lin + 1.0))
            hid = jnp.concatenate(hid, axis=1).astype(_bf16)
            y = jnp.dot(hid, w2b[...], preferred_element_type=_f32) + b2_ref[0]
            for c in range(ROW_TILES):
                ys_ref[rows(c), :] = y[:, c * LANES:(c + 1) * LANES]

        for n_sub in range(1, MOE_CHAINS + 1):
            @pl.when(jnp.logical_and(nv > (n_sub - 1) * hm, nv <= n_sub * hm))
            def _(n_sub=n_sub):
                for s in range(n_sub):
                    sub_block(s)


def _moe_call(block_e, n_valid, block_src, block_grp, next_e, xs, g2, w1, b1p, w2, b2):
    nblk = block_e.shape[0]
    tm = EXPERT_TILE
    by_e = lambda i, be, nv, src, grp, nxt: (be[i], 0, 0)
    by_src = lambda i, be, nv, src, grp, nxt: (src[i], 0)
    fix = lambda i, be, nv, src, grp, nxt: (0, 0)
    grid_spec = pltpu.PrefetchScalarGridSpec(
        num_scalar_prefetch=5,
        grid=(nblk,),
        in_specs=[pl.BlockSpec((tm * ROW_TILES, LANES), by_src), pl.BlockSpec((1, D_MODEL), fix),
                  pl.BlockSpec(memory_space=pl.ANY), pl.BlockSpec((1, 1, 2 * D_FF), by_e),
                  pl.BlockSpec(memory_space=pl.ANY), pl.BlockSpec((1, 1, D_MODEL), by_e)],
        out_specs=pl.BlockSpec((tm * ROW_TILES, LANES), by_src),
        scratch_shapes=[pltpu.VMEM((2, D_MODEL, 2 * D_FF), _f32), pltpu.VMEM((2, D_FF, D_MODEL), _f32),
                        pltpu.VMEM((D_MODEL, 2 * D_FF), _bf16), pltpu.VMEM((D_FF, D_MODEL), _bf16),
                        pltpu.SemaphoreType.DMA((2, 2))],
    )
    return pl.pallas_call(
        _moe_kernel,
        grid_spec=grid_spec,
        out_shape=jax.ShapeDtypeStruct(xs.shape, _f32),
        compiler_params=pltpu.CompilerParams(dimension_semantics=("arbitrary",),
                                             vmem_limit_bytes=VMEM_LIMIT),
        name="moe_experts",
    )(block_e, n_valid, block_src, block_grp, next_e, xs, g2, w1, b1p, w2, b2)


def _combine_kernel(*refs):
    dcur_ref, dnext_ref = refs[:TOP_K], refs[TOP_K:2 * TOP_K]
    x1_ref, ys_hbm, g_ref, gf_ref, o_ref, gbuf, sem = refs[2 * TOP_K:]
    tc = COMBINE_TILE
    i = pl.program_id(0)
    slot = i % 2

    def row_copy(d_ref, s, r, k):
        dst = gbuf.at[s, k, pl.ds(r * ROW_TILES, ROW_TILES)]
        return pltpu.make_async_copy(_token_row(ys_hbm, d_ref[k][r]), dst, sem.at[s])

    def issue(d_ref, s):
        def group(g, c):
            for rr in range(DMA_GROUP):
                for k in range(TOP_K):
                    row_copy(d_ref, s, g * DMA_GROUP + rr, k).start(priority=k % 2)
            return c
        lax.fori_loop(0, tc // DMA_GROUP, group, 0)

    @pl.when(i == 0)
    def _():
        issue(dcur_ref, 0)

    @pl.when(i + 1 < pl.num_programs(0))
    def _():
        issue(dnext_ref, 1 - slot)

    def wait_group(g, c):
        for _ in range(DMA_GROUP * TOP_K):
            row_copy(dcur_ref, slot, 0, 0).wait()
        return c

    lax.fori_loop(0, tc // DMA_GROUP, wait_group, 0)

    g = g_ref[...]
    acc = _load_token_rows(x1_ref, tc)
    for k in range(TOP_K):
        acc = acc + g[:, k:k + 1] * _load_token_rows(gbuf.at[slot, k], tc)
    o_ref[...] = _rms(acc, gf_ref[...])


def _combine_call(dest, x1r, ys, gates_t, gf):
    tc = COMBINE_TILE
    nt = dest.shape[1] // tc
    row = lambda i: (i, 0)
    slots = lambda tile: pl.BlockSpec((tc,), lambda i: (tile(i),), memory_space=pltpu.SMEM)
    return pl.pallas_call(
        _combine_kernel,
        grid=(nt,),
        in_specs=[slots(lambda i: i) for _ in range(TOP_K)]
                 + [slots(lambda i: jnp.minimum(i + 1, nt - 1)) for _ in range(TOP_K)]
                 + [pl.BlockSpec((tc * ROW_TILES, LANES), row), pl.BlockSpec(memory_space=pl.ANY),
                    pl.BlockSpec((tc, TOP_K), row), pl.BlockSpec((1, D_MODEL), lambda i: (0, 0))],
        out_specs=pl.BlockSpec((tc, D_MODEL), row),
        out_shape=jax.ShapeDtypeStruct((nt * tc, D_MODEL), _f32),
        scratch_shapes=[pltpu.VMEM((2, TOP_K, tc * ROW_TILES, LANES), _f32), pltpu.SemaphoreType.DMA((2,))],
        compiler_params=pltpu.CompilerParams(dimension_semantics=("arbitrary",),
                                             vmem_limit_bytes=VMEM_LIMIT),
        name="combine_norm",
    )(*([dest[k] for k in range(TOP_K)] * 2), x1r, ys, gates_t, gf)


def _plan_blocks(route_i, counts, n_tokens):
    tm = EXPERT_TILE
    nblk = (n_tokens * TOP_K + N_EXPERTS * tm) // tm
    counts = counts.astype(jnp.int32)
    blocks_e = (counts + tm - 1) // tm
    blk_end = jnp.cumsum(blocks_e)
    pad_start = (blk_end - blocks_e) * tm
    experts = jnp.arange(N_EXPERTS, dtype=jnp.int32)
    hot = route_i[:TOP_K, :, None] == experts
    dest = jnp.sum(jnp.where(hot, pad_start, 0), axis=-1) + route_i[TOP_K:]
    n_used = blk_end[-1]
    blk = jnp.arange(nblk, dtype=jnp.int32)
    src = jnp.minimum(blk, jnp.maximum(n_used - 1, 0))
    be = jnp.minimum(jnp.sum((blk_end[None, :] <= src[:, None]).astype(jnp.int32), axis=1), N_EXPERTS - 1)
    hot_b = be[:, None] == experts
    pick = lambda table: jnp.sum(jnp.where(hot_b, table, 0), axis=1)
    n_valid = jnp.where(blk < n_used, jnp.clip(pick(counts) - (src * tm - pick(pad_start)), 0, tm), 0)
    used = blocks_e > 0
    grp_e = jnp.cumsum(used.astype(jnp.int32)) - 1
    later = jnp.where((experts[None, :] > experts[:, None]) & used[None, :], experts[None, :], N_EXPERTS)
    nxt_e = jnp.min(later, axis=1)
    nxt_e = jnp.where(nxt_e < N_EXPERTS, nxt_e, -1)
    return be, n_valid.astype(jnp.int32), src, pick(grp_e), pick(nxt_e), dest


def _split_bf16(w):
    hi = w.astype(_bf16)
    return jnp.stack([hi, (w - hi.astype(_f32)).astype(_bf16)])


def kernel(x, positions, norm1_g, w_in, b_in, sinks, gn_dil, gn_swa, w_out, b_out,
           norm2_g, w_router, b_router, w_mlp1, b_mlp1, w_mlp2, b_mlp2, norm_f_g):
    batch, seq, d = x.shape
    T = batch * seq
    x2 = x.reshape(T, d)
    depth = norm1_g.shape[0]
    assert depth == 1, "the final norm is fused into the last layer's combine kernel"
    inv = ROPE_THETA ** (-jnp.arange(0, HEAD_DIM, 2, dtype=_f32) / HEAD_DIM)
    inv = jnp.tile(inv, LANES // (HEAD_DIM // 2)).reshape(1, LANES)
    pos = positions.reshape(T, 1)
    for layer in range(depth):
        qa, ka, va, qb, kb, vb = _inproj_call(
            x2, pos, inv, norm1_g[layer].reshape(1, d), w_in[layer].astype(_bf16), b_in[layer].reshape(1, D_IN))
        o_dil = _attn_call(qa, ka, va, None, batch=batch, seq=seq,
                           configs=tuple((w // r, r) for w, r in DILATED_CONFIGS), kv_group=1)
        sink_lanes = jnp.repeat(sinks[layer].astype(_f32), HEAD_DIM).reshape(1, W_SWA)
        o_swa = _attn_call(qb, kb, vb, sink_lanes, batch=batch, seq=seq,
                           configs=((SWA_WINDOW - 1, 1),), kv_group=N_HEADS_SWA // N_KV_SWA // PAIR)
        x1r, route_i, route_g, counts = _outproj_call(
            o_dil, o_swa, x2, gn_dil[layer].reshape(1, W_DIL), gn_swa[layer].reshape(1, W_SWA),
            w_out[layer].astype(_bf16), b_out[layer].reshape(1, d), norm2_g[layer].reshape(1, d),
            _split_bf16(w_router[layer].T), b_router[layer].reshape(N_EXPERTS, 1))
        block_e, n_valid, block_src, block_grp, next_e, dest = _plan_blocks(route_i, counts[:, 0], T)
        xs = _dispatch_call(dest, x1r, block_e.shape[0] * EXPERT_TILE)
        half = GLU_GROUP // 2
        b1p = b_mlp1[layer].reshape(N_EXPERTS, 2 * D_FF // GLU_GROUP, half, 2)
        b1p = jnp.swapaxes(b1p, 2, 3).reshape(N_EXPERTS, 1, 2 * D_FF)
        ys = _moe_call(block_e, n_valid, block_src, block_grp, next_e, xs, norm2_g[layer].reshape(1, d),
                       w_mlp1[layer], b1p, w_mlp2[layer], b_mlp2[layer].reshape(N_EXPERTS, 1, d))
        x2 = _combine_call(dest, x1r, ys, route_g[:TOP_K].T, norm_f_g.reshape(1, d))
    return x2.reshape(batch, seq, d)
```

```python
import functools

import jax
import jax.numpy as jnp
from jax import lax
from jax.experimental import pallas as pl
from jax.experimental.pallas import tpu as pltpu

D_MODEL = 1024
HEAD_DIM = 64
N_HEADS_DIL = 8
DILATED_CONFIGS = ((128, 1), (512, 4), (2048, 16))
N_HEADS_SWA = 8
N_KV_SWA = 2
SWA_WINDOW = 128
ATTN_BLOCK = 128
ROPE_THETA = 10000.0
N_EXPERTS = 32
TOP_K = 4
D_FF = D_MODEL
SWIGLU_ALPHA = 1.702
SWIGLU_LIMIT = 7.0
NORM_EPS = 1e-5
LOG2E = 1.4426950408889634

W_DIL = N_HEADS_DIL * HEAD_DIM
W_SWA = N_HEADS_SWA * HEAD_DIM
W_KV_SWA = N_KV_SWA * HEAD_DIM
D_IN = 3 * W_DIL + W_SWA + 2 * W_KV_SWA

LANES = 128
PAIR = LANES // HEAD_DIM
ROW_TILES = D_MODEL // LANES
TOKEN_TILE = 1024
EXPERT_TILE = 1024
DISPATCH_TILE = 2048
COMBINE_TILE = 256
DMA_GROUP = 32
ATTN_UNROLL = 32
MOE_CHAINS = 4
INPROJ_TILE = 2048
INPROJ_CHAINS = 8
OUTPROJ_CHAINS = 2
RANK_CHUNK = 256
GLU_GROUP = 256
VMEM_LIMIT = 56 * 1024 * 1024

_f32 = jnp.float32
_bf16 = jnp.bfloat16


def _rms(x, g):
    return x * lax.rsqrt(jnp.mean(x * x, axis=-1, keepdims=True) + NORM_EPS) * g


def _store_token_rows(ref, x):
    tm = x.shape[0]
    for c in range(ROW_TILES):
        ref[pl.ds(c, tm, stride=ROW_TILES), :] = x[:, c * LANES:(c + 1) * LANES]


def _load_token_rows(ref, tm):
    return jnp.concatenate([ref[pl.ds(c, tm, stride=ROW_TILES), :] for c in range(ROW_TILES)], axis=1)


def _token_row(ref, tok):
    return ref.at[pl.ds(pl.multiple_of(tok * ROW_TILES, ROW_TILES), ROW_TILES)]


def _inproj_kernel(x_ref, pos_ref, inv_ref, g_ref, w_ref, b_ref, *out_refs):
    sub = x_ref.shape[0] // INPROJ_CHAINS
    for s in range(INPROJ_CHAINS):
        rows = slice(s * sub, (s + 1) * sub)
        _inproj_rows(x_ref[rows, :], pos_ref[rows, :], inv_ref, g_ref, w_ref, b_ref,
                     [r.at[rows, :] for r in out_refs])


def _inproj_rows(x, pos, inv_ref, g_ref, w_ref, b_ref, out_refs):
    qa_ref, ka_ref, va_ref, qb_ref, kb_ref, vb_ref = out_refs
    h = _rms(x, g_ref[...]).astype(_bf16)
    proj = jnp.dot(h, w_ref[...], preferred_element_type=_f32) + b_ref[...]

    ang = pos.astype(_f32) * inv_ref[...]
    lane = lax.broadcasted_iota(jnp.int32, ang.shape, 1)
    first_half = (lane % HEAD_DIM) < (HEAD_DIM // 2)
    cos = jnp.cos(ang)
    sin = jnp.where(first_half, -jnp.sin(ang), jnp.sin(ang))

    def rope(t, scale):
        outs = []
        for c in range(t.shape[1] // LANES):
            tc = t[:, c * LANES:(c + 1) * LANES]
            partner = jnp.where(first_half, pltpu.roll(tc, LANES - HEAD_DIM // 2, 1),
                                pltpu.roll(tc, HEAD_DIM // 2, 1))
            outs.append((tc * cos + partner * sin) * scale)
        return jnp.concatenate(outs, axis=1) if len(outs) > 1 else outs[0]

    scale = HEAD_DIM ** -0.5 * LOG2E
    o = 0
    qa_ref[...] = rope(proj[:, o:o + W_DIL], scale).astype(_bf16); o += W_DIL
    ka_ref[...] = rope(proj[:, o:o + W_DIL], 1.0).astype(_bf16); o += W_DIL
    va_ref[...] = proj[:, o:o + W_DIL].astype(_bf16); o += W_DIL
    qb_ref[...] = rope(proj[:, o:o + W_SWA], scale).astype(_bf16); o += W_SWA
    kb = rope(proj[:, o:o + W_KV_SWA], 1.0); o += W_KV_SWA
    vb = proj[:, o:o + W_KV_SWA]

    def dup_heads(t):
        sw = pltpu.roll(t, HEAD_DIM, 1)
        lo = lane < HEAD_DIM
        return jnp.concatenate([jnp.where(lo, t, sw), jnp.where(lo, sw, t)], axis=1)

    kb_ref[...] = dup_heads(kb).astype(_bf16)
    vb_ref[...] = dup_heads(vb).astype(_bf16)


def _inproj_call(x2, pos, inv, g, w, b):
    T = x2.shape[0]
    tm = INPROJ_TILE
    row = lambda i: (i, 0)
    fix = lambda i: (0, 0)
    widths = (W_DIL, W_DIL, W_DIL, W_SWA, PAIR * W_KV_SWA, PAIR * W_KV_SWA)
    return pl.pallas_call(
        _inproj_kernel,
        grid=(T // tm,),
        in_specs=[pl.BlockSpec((tm, D_MODEL), row), pl.BlockSpec((tm, 1), row),
                  pl.BlockSpec((1, LANES), fix), pl.BlockSpec((1, D_MODEL), fix),
                  pl.BlockSpec((D_MODEL, D_IN), fix), pl.BlockSpec((1, D_IN), fix)],
        out_specs=[pl.BlockSpec((tm, wd), row) for wd in widths],
        out_shape=[jax.ShapeDtypeStruct((T, wd), _bf16) for wd in widths],
        compiler_params=pltpu.CompilerParams(dimension_semantics=("parallel",),
                                             vmem_limit_bytes=VMEM_LIMIT),
        name="inproj",
    )(x2, pos, inv, g, w, b)


def _attn_kernel(*refs, configs, seq, with_sink):
    if with_sink:
        q_ref, k_ref, v_ref, sink_ref, o_ref = refs[:5]
        scratch = refs[5:]
    else:
        q_ref, k_ref, v_ref, o_ref = refs[:4]
        scratch = refs[4:]
    n_cfg = len(configs)
    n_acc = n_cfg if n_cfg > 1 else 0
    obufs = scratch[:n_acc]
    mbufs = scratch[n_acc:2 * n_acc]
    lbufs = scratch[2 * n_acc:3 * n_acc]
    extra = list(scratch[3 * n_acc:])
    blk = ATTN_BLOCK
    dilated = [r for _, r in configs if r > 1]
    if dilated:
        qf, kf, vf = extra[:3]
        qf[...] = q_ref[...].astype(_f32)
        kf[...] = k_ref[...].astype(_f32)
        vf[...] = v_ref[...].astype(_f32)
        extra = extra[3:]

    lane = lax.broadcasted_iota(jnp.int32, (blk, LANES), 1)
    head0 = lane < HEAD_DIM
    lane_row = lax.broadcasted_iota(jnp.int32, (1, LANES), 1)
    head_rows = [(lane_row < HEAD_DIM).astype(_bf16), (lane_row >= HEAD_DIM).astype(_bf16)]
    qi = lax.broadcasted_iota(jnp.int32, (blk, 2 * blk), 0)
    kj = lax.broadcasted_iota(jnp.int32, (blk, 2 * blk), 1)
    dist = qi - kj + blk

    def merge(os_, ms_, ls_):
        ms_ = list(ms_)
        if with_sink:
            ms_.append(jnp.broadcast_to(sink_ref[...] * LOG2E, ms_[0].shape))
        mx = functools.reduce(jnp.maximum, ms_)
        es = [jnp.exp2(m - mx) for m in ms_]
        num = functools.reduce(lambda a, b: a + b, [e * o for e, o in zip(es, os_)])
        den = functools.reduce(lambda a, b: a + b, [e * l for e, l in zip(es, ls_)])
        if with_sink:
            den = den + es[-1]
        return num / den

    base_perm = None
    for ci, (max_dist, r) in enumerate(configs):
        nb = seq // r // blk
        band = (dist >= 0) & (dist <= max_dist)
        obuf, mbuf, lbuf = (obufs[ci], mbufs[ci], lbufs[ci]) if n_cfg > 1 else (None, None, None)
        if r == 1:
            qp, kp, vp = q_ref, k_ref, v_ref
        else:
            perm_bf16 = extra[:3]
            extra = extra[3:]
            later = [r2 for _, r2 in configs[ci + 1:] if r2 % r == 0]
            perm_f32 = []
            if later:
                perm_f32 = extra[:3]
                extra = extra[3:]
            r_src, srcs = 1, (qf, kf, vf)
            if base_perm is not None and r % base_perm[0] == 0:
                r_src, srcs = base_perm
            if perm_f32:
                base_perm = (r, perm_f32)
            qp, kp, vp = perm_bf16
            piece = 2 * blk
            step = r // r_src
            per_res = seq // r // piece
            len_src = seq // r_src

            def permute(idx, carry, r_src=r_src, srcs=srcs, step=step, per_res=per_res, len_src=len_src,
                        perm_bf16=perm_bf16, perm_f32=perm_f32):
                c = idx // per_res
                start = (c % r_src) * len_src + c // r_src + (step * piece) * (idx % per_res)
                src = pl.ds(start, piece, stride=step)
                dst = pl.ds(pl.multiple_of(idx * piece, piece), piece)
                for t in range(3):
                    val = srcs[t][src, :]
                    perm_bf16[t][dst, :] = val.astype(_bf16)
                    if perm_f32:
                        perm_f32[t][dst, :] = val
                return carry

            lax.fori_loop(0, seq // piece, permute, 0, unroll=True)

        def body(n, carry, r=r, nb=nb, band=band, obuf=obuf, lbuf=lbuf, qp=qp, kp=kp, vp=vp):
            j = n % nb
            cur = pl.ds(pl.multiple_of(n * blk, blk), blk)
            prev = pl.ds(pl.multiple_of(jnp.maximum(n - 1, 0) * blk, blk), blk)
            q = qp[cur, :]
            k2 = jnp.concatenate([kp[prev, :], kp[cur, :]], axis=0)
            v2 = jnp.concatenate([vp[prev, :], vp[cur, :]], axis=0)
            mask = band & (kj >= jnp.where(j > 0, 0, blk))
            q2 = jnp.concatenate([q * hm for hm in head_rows], axis=0)
            s2 = lax.dot_general(q2, k2, (((1,), (1,)), ((), ())), preferred_element_type=_f32)
            ps, maxs, sums = [], [], []
            for h in range(PAIR):
                s = jnp.where(mask, s2[h * blk:(h + 1) * blk], -jnp.inf)
                m = jnp.max(s, axis=-1, keepdims=True)
                p = jnp.exp2(s - m)
                ps.append(p.astype(_bf16))
                maxs.append(jnp.broadcast_to(m, (blk, LANES)))
                sums.append(jnp.broadcast_to(jnp.sum(p, axis=-1, keepdims=True), (blk, LANES)))
            o2 = jnp.dot(jnp.concatenate(ps, axis=0), v2, preferred_element_type=_f32)
            outs = [o2[:blk], o2[blk:]]
            start = n // nb + (r * blk) * j
            dst = pl.ds(pl.multiple_of(start, blk), blk) if r == 1 else pl.ds(start, blk, stride=r)
            o_pair = jnp.where(head0, outs[0], outs[1])
            m_pair = jnp.where(head0, maxs[0], maxs[1])
            l_pair = jnp.where(head0, sums[0], sums[1])
            if n_cfg == 1:
                o_ref[dst, :] = merge([o_pair], [m_pair], [l_pair])
            else:
                obuf[dst, :] = o_pair
                mbuf[dst, :] = m_pair
                lbuf[dst, :] = l_pair
            return carry

        lax.fori_loop(0, seq // blk, body, 0, unroll=ATTN_UNROLL)

    if n_cfg > 1:
        chunk = 4 * blk

        def finish(i, carry):
            sl = pl.ds(pl.multiple_of(i * chunk, chunk), chunk)
            o_ref[sl, :] = merge([b[sl, :] for b in obufs], [b[sl, :] for b in mbufs], [b[sl, :] for b in lbufs])
            return carry

        lax.fori_loop(0, seq // chunk, finish, 0)


def _attn_call(q, k, v, sinks, *, batch, seq, configs, kv_group):
    n_pairs = q.shape[1] // LANES
    with_sink = sinks is not None
    qmap = lambda b, p: (b, p)
    kvmap = lambda b, p: (b, p // kv_group)
    in_specs = [pl.BlockSpec((seq, LANES), qmap), pl.BlockSpec((seq, LANES), kvmap),
                pl.BlockSpec((seq, LANES), kvmap)]
    args = [q, k, v]
    if with_sink:
        in_specs.append(pl.BlockSpec((1, LANES), lambda b, p: (0, p)))
        args.append(sinks)
    buf = lambda dt: pltpu.VMEM((seq, LANES), dt)
    scratch = [buf(_f32) for _ in range(3 * len(configs) if len(configs) > 1 else 0)]
    if any(r > 1 for _, r in configs):
        scratch += [buf(_f32) for _ in range(3)]
    for ci, (_, r) in enumerate(configs):
        if r > 1:
            scratch += [buf(_bf16) for _ in range(3)]
            if any(r2 % r == 0 for _, r2 in configs[ci + 1:]):
                scratch += [buf(_f32) for _ in range(3)]
    return pl.pallas_call(
        functools.partial(_attn_kernel, configs=configs, seq=seq, with_sink=with_sink),
        grid=(batch, n_pairs),
        in_specs=in_specs,
        out_specs=pl.BlockSpec((seq, LANES), qmap),
        out_shape=jax.ShapeDtypeStruct(q.shape, _f32),
        scratch_shapes=scratch,
        compiler_params=pltpu.CompilerParams(dimension_semantics=("parallel", "parallel"),
                                             vmem_limit_bytes=VMEM_LIMIT),
        name="attn_sink" if with_sink else "attn_dilated",
    )(*args)


def _outproj_kernel(od_ref, os_ref, x_ref, gd_ref, gs_ref, w_ref, b_ref, g2_ref, wr_ref, br_ref,
                    x1_ref, ri_ref, rg_ref, cnt_ref, carry_ref, before_ref):
    tm = x_ref.shape[0]

    @pl.when(pl.program_id(0) == 0)
    def _():
        carry_ref[...] = jnp.zeros_like(carry_ref)
        ti = lax.broadcasted_iota(jnp.int32, (RANK_CHUNK, RANK_CHUNK), 0)
        tj = lax.broadcasted_iota(jnp.int32, (RANK_CHUNK, RANK_CHUNK), 1)
        before_ref[...] = (ti < tj).astype(_bf16)

    sub = tm // OUTPROJ_CHAINS
    h2 = []
    for s in range(OUTPROJ_CHAINS):
        rows = slice(s * sub, (s + 1) * sub)
        mixed = jnp.concatenate([_rms(od_ref[rows, :], gd_ref[...]), _rms(os_ref[rows, :], gs_ref[...])], axis=1)
        x1 = x_ref[rows, :] + jnp.dot(mixed.astype(_bf16), w_ref[...], preferred_element_type=_f32) + b_ref[...]
        _store_token_rows(x1_ref.at[pl.ds(s * sub * ROW_TILES, sub * ROW_TILES)], x1)
        h2.append(_rms(x1, g2_ref[...]))
    h2 = jnp.concatenate(h2, axis=0)
    h_hi = h2.astype(_bf16)
    h_lo = (h2 - h_hi.astype(_f32)).astype(_bf16)
    nt_dot = lambda a, b: lax.dot_general(a, b, (((1,), (1,)), ((), ())), preferred_element_type=_f32)
    logits = (nt_dot(wr_ref[0], h_hi) + nt_dot(wr_ref[0], h_lo) + nt_dot(wr_ref[1], h_hi)) + br_ref[...]
    eiota = lax.broadcasted_iota(jnp.int32, logits.shape, 0).astype(_f32)
    vals, idxs, sels = [], [], []
    l = logits
    for _ in range(TOP_K):
        m = jnp.max(l, axis=0, keepdims=True)
        idx = jnp.min(jnp.where(l == m, eiota, float(N_EXPERTS)), axis=0, keepdims=True)
        sel = eiota == idx
        vals.append(m); idxs.append(idx); sels.append(sel)
        l = jnp.where(sel, -jnp.inf, l)
    exps = [jnp.exp(v - vals[0]) for v in vals]
    den = functools.reduce(lambda a, b: a + b, exps)
    gates = [e / den for e in exps]

    chosen = functools.reduce(jnp.logical_or, sels).astype(_f32)
    carry = carry_ref[...]
    prefix = []
    for c in range(tm // RANK_CHUNK):
        part = chosen[:, c * RANK_CHUNK:(c + 1) * RANK_CHUNK]
        prefix.append(jnp.dot(part.astype(_bf16), before_ref[...], preferred_element_type=_f32) + carry)
        carry = carry + jnp.sum(part, axis=1, keepdims=True)
    prefix = jnp.concatenate(prefix, axis=1)
    ranks = [jnp.sum(jnp.where(s, prefix, 0.0), axis=0, keepdims=True).astype(jnp.int32) for s in sels]
    carry_ref[...] = carry

    ri_ref[...] = jnp.concatenate([ix.astype(jnp.int32) for ix in idxs] + ranks, axis=0)
    rg_ref[...] = jnp.concatenate(gates + gates, axis=0)
    cnt_ref[...] = jnp.broadcast_to(carry_ref[...], cnt_ref.shape)


def _outproj_call(od, osw, x2, gd, gs, w, b, g2, wr_t, br):
    T = x2.shape[0]
    tm = TOKEN_TILE
    row = lambda i: (i, 0)
    col = lambda i: (0, i)
    fix = lambda i: (0, 0)
    return pl.pallas_call(
        _outproj_kernel,
        grid=(T // tm,),
        in_specs=[pl.BlockSpec((tm, W_DIL), row), pl.BlockSpec((tm, W_SWA), row),
                  pl.BlockSpec((tm, D_MODEL), row),
                  pl.BlockSpec((1, W_DIL), fix), pl.BlockSpec((1, W_SWA), fix),
                  pl.BlockSpec((D_MODEL, D_MODEL), fix), pl.BlockSpec((1, D_MODEL), fix),
                  pl.BlockSpec((1, D_MODEL), fix),
                  pl.BlockSpec((2, N_EXPERTS, D_MODEL), lambda i: (0, 0, 0)), pl.BlockSpec((N_EXPERTS, 1), fix)],
        out_specs=[pl.BlockSpec((tm * ROW_TILES, LANES), row), pl.BlockSpec((2 * TOP_K, tm), col),
                   pl.BlockSpec((2 * TOP_K, tm), col), pl.BlockSpec((N_EXPERTS, LANES), fix)],
        out_shape=[jax.ShapeDtypeStruct((T * ROW_TILES, LANES), _f32),
                   jax.ShapeDtypeStruct((2 * TOP_K, T), jnp.int32),
                   jax.ShapeDtypeStruct((2 * TOP_K, T), _f32),
                   jax.ShapeDtypeStruct((N_EXPERTS, LANES), _f32)],
        scratch_shapes=[pltpu.VMEM((N_EXPERTS, 1), _f32), pltpu.VMEM((RANK_CHUNK, RANK_CHUNK), _bf16)],
        compiler_params=pltpu.CompilerParams(dimension_semantics=("arbitrary",),
                                             vmem_limit_bytes=VMEM_LIMIT),
        name="outproj_router",
    )(od, osw, x2, gd, gs, w, b, g2, wr_t, br)


def _dispatch_kernel(*refs):
    d_ref = refs[:TOP_K]
    x1_ref, xs_hbm, sem = refs[TOP_K:]
    td = DISPATCH_TILE

    def start_group(g, c):
        for rr in range(DMA_GROUP):
            r = g * DMA_GROUP + rr
            src = _token_row(x1_ref, r)
            for k in range(TOP_K):
                pltpu.make_async_copy(src, _token_row(xs_hbm, d_ref[k][r]), sem).start(priority=k % 2)
        return c

    lax.fori_loop(0, td // DMA_GROUP, start_group, 0)

    def wait_group(g, c):
        for _ in range(DMA_GROUP * TOP_K):
            pltpu.make_async_copy(_token_row(x1_ref, 0), _token_row(xs_hbm, 0), sem).wait()
        return c

    lax.fori_loop(0, td // DMA_GROUP, wait_group, 0)


def _dispatch_call(dest, x1r, n_slots):
    nt = dest.shape[1] // DISPATCH_TILE
    return pl.pallas_call(
        _dispatch_kernel,
        grid=(nt,),
        in_specs=[pl.BlockSpec((DISPATCH_TILE,), lambda i: (i,), memory_space=pltpu.SMEM) for _ in range(TOP_K)]
                 + [pl.BlockSpec((DISPATCH_TILE * ROW_TILES, LANES), lambda i: (i, 0))],
        out_specs=pl.BlockSpec(memory_space=pl.ANY),
        out_shape=jax.ShapeDtypeStruct((n_slots * ROW_TILES, LANES), _f32),
        scratch_shapes=[pltpu.SemaphoreType.DMA(())],
        compiler_params=pltpu.CompilerParams(dimension_semantics=("arbitrary",),
                                             vmem_limit_bytes=VMEM_LIMIT),
        name="dispatch_rows",
    )(*[dest[k] for k in range(TOP_K)], x1r)


def _moe_kernel(be_ref, nv_ref, src_ref, grp_ref, nxt_ref, xs_ref, g2_ref, w1_hbm, b1_ref, w2_hbm, b2_ref,
                ys_ref, st1, st2, w1p, w2b, sem):
    i = pl.program_id(0)
    nv = nv_ref[i]
    tm = EXPERT_TILE

    def weight_copies(e, slot):
        return (pltpu.make_async_copy(w1_hbm.at[e], st1.at[slot], sem.at[0, slot]),
                pltpu.make_async_copy(w2_hbm.at[e], st2.at[slot], sem.at[1, slot]))

    @pl.when(nv > 0)
    def _():
        first_of_expert = jnp.logical_or(i == 0, be_ref[i] != be_ref[jnp.maximum(i - 1, 0)])

        @pl.when(first_of_expert)
        def _():
            slot = grp_ref[i] % 2

            @pl.when(i == 0)
            def _():
                for c in weight_copies(be_ref[0], 0):
                    c.start()

            for c in weight_copies(be_ref[i], slot):
                c.wait()

            @pl.when(nxt_ref[i] >= 0)
            def _():
                for c in weight_copies(nxt_ref[i], 1 - slot):
                    c.start()

            src = lax.broadcasted_iota(jnp.int32, (GLU_GROUP, GLU_GROUP), 0)
            dst = lax.broadcasted_iota(jnp.int32, (GLU_GROUP, GLU_GROUP), 1)
            half = GLU_GROUP // 2
            perm = (src == jnp.where(dst < half, 2 * dst, 2 * (dst - half) + 1)).astype(_bf16)
            for g in range(2 * D_FF // GLU_GROUP):
                cols = slice(g * GLU_GROUP, (g + 1) * GLU_GROUP)
                w1p[:, cols] = jnp.dot(st1[slot, :, cols].astype(_bf16), perm,
                                       preferred_element_type=_f32).astype(_bf16)
            w2b[...] = st2[slot].astype(_bf16)

        hm = tm // MOE_CHAINS

        def sub_block(s):
            rows = lambda c: pl.ds(s * hm * ROW_TILES + c, hm, stride=ROW_TILES)
            x = jnp.concatenate([xs_ref[rows(c), :] for c in range(ROW_TILES)], axis=1)
            x = jnp.where(lax.broadcasted_iota(jnp.int32, (hm, 1), 0) < nv - s * hm, x, 0.0)
            h = _rms(x, g2_ref[...]).astype(_bf16)
            u = jnp.dot(h, w1p[...], preferred_element_type=_f32) + b1_ref[0]
            half = GLU_GROUP // 2
            hid = []
            for g in range(2 * D_FF // GLU_GROUP):
                glu = jnp.minimum(u[:, g * GLU_GROUP:g * GLU_GROUP + half], SWIGLU_LIMIT)
                lin = jnp.clip(u[:, g * GLU_GROUP + half:(g + 1) * GLU_GROUP], -SWIGLU_LIMIT, SWIGLU_LIMIT)
                hid.append(glu * (1.0 / (1.0 + jnp.exp(-SWIGLU_ALPHA * glu))) * (lin + 1.0))
            hid = jnp.concatenate(hid, axis=1).astype(_bf16)
            y = jnp.dot(hid, w2b[...], preferred_element_type=_f32) + b2_ref[0]
            for c in range(ROW_TILES):
                ys_ref[rows(c), :] = y[:, c * LANES:(c + 1) * LANES]

        for n_sub in range(1, MOE_CHAINS + 1):
            @pl.when(jnp.logical_and(nv > (n_sub - 1) * hm, nv <= n_sub * hm))
            def _(n_sub=n_sub):
                for s in range(n_sub):
                    sub_block(s)


def _moe_call(block_e, n_valid, block_src, block_grp, next_e, xs, g2, w1, b1p, w2, b2):
    nblk = block_e.shape[0]
    tm = EXPERT_TILE
    by_e = lambda i, be, nv, src, grp, nxt: (be[i], 0, 0)
    by_src = lambda i, be, nv, src, grp, nxt: (src[i], 0)
    fix = lambda i, be, nv, src, grp, nxt: (0, 0)
    grid_spec = pltpu.PrefetchScalarGridSpec(
        num_scalar_prefetch=5,
        grid=(nblk,),
        in_specs=[pl.BlockSpec((tm * ROW_TILES, LANES), by_src), pl.BlockSpec((1, D_MODEL), fix),
                  pl.BlockSpec(memory_space=pl.ANY), pl.BlockSpec((1, 1, 2 * D_FF), by_e),
                  pl.BlockSpec(memory_space=pl.ANY), pl.BlockSpec((1, 1, D_MODEL), by_e)],
        out_specs=pl.BlockSpec((tm * ROW_TILES, LANES), by_src),
        scratch_shapes=[pltpu.VMEM((2, D_MODEL, 2 * D_FF), _f32), pltpu.VMEM((2, D_FF, D_MODEL), _f32),
                        pltpu.VMEM((D_MODEL, 2 * D_FF), _bf16), pltpu.VMEM((D_FF, D_MODEL), _bf16),
                        pltpu.SemaphoreType.DMA((2, 2))],
    )
    return pl.pallas_call(
        _moe_kernel,
        grid_spec=grid_spec,
        out_shape=jax.ShapeDtypeStruct(xs.shape, _f32),
        compiler_params=pltpu.CompilerParams(dimension_semantics=("arbitrary",),
                                             vmem_limit_bytes=VMEM_LIMIT),
        name="moe_experts",
    )(block_e, n_valid, block_src, block_grp, next_e, xs, g2, w1, b1p, w2, b2)


def _combine_kernel(*refs):
    dcur_ref, dnext_ref = refs[:TOP_K], refs[TOP_K:2 * TOP_K]
    x1_ref, ys_hbm, g_ref, gf_ref, o_ref, gbuf, sem = refs[2 * TOP_K:]
    tc = COMBINE_TILE
    i = pl.program_id(0)
    slot = i % 2

    def row_copy(d_ref, s, r, k):
        dst = gbuf.at[s, k, pl.ds(r * ROW_TILES, ROW_TILES)]
        return pltpu.make_async_copy(_token_row(ys_hbm, d_ref[k][r]), dst, sem.at[s])

    def issue(d_ref, s):
        def group(g, c):
            for rr in range(DMA_GROUP):
                for k in range(TOP_K):
                    row_copy(d_ref, s, g * DMA_GROUP + rr, k).start(priority=k % 2)
            return c
        lax.fori_loop(0, tc // DMA_GROUP, group, 0)

    @pl.when(i == 0)
    def _():
        issue(dcur_ref, 0)

    @pl.when(i + 1 < pl.num_programs(0))
    def _():
        issue(dnext_ref, 1 - slot)

    def wait_group(g, c):
        for _ in range(DMA_GROUP * TOP_K):
            row_copy(dcur_ref, slot, 0, 0).wait()
        return c

    lax.fori_loop(0, tc // DMA_GROUP, wait_group, 0)

    g = g_ref[...]
    acc = _load_token_rows(x1_ref, tc)
    for k in range(TOP_K):
        acc = acc + g[:, k:k + 1] * _load_token_rows(gbuf.at[slot, k], tc)
    o_ref[...] = _rms(acc, gf_ref[...])


def _combine_call(dest, x1r, ys, gates_t, gf):
    tc = COMBINE_TILE
    nt = dest.shape[1] // tc
    row = lambda i: (i, 0)
    slots = lambda tile: pl.BlockSpec((tc,), lambda i: (tile(i),), memory_space=pltpu.SMEM)
    return pl.pallas_call(
        _combine_kernel,
        grid=(nt,),
        in_specs=[slots(lambda i: i) for _ in range(TOP_K)]
                 + [slots(lambda i: jnp.minimum(i + 1, nt - 1)) for _ in range(TOP_K)]
                 + [pl.BlockSpec((tc * ROW_TILES, LANES), row), pl.BlockSpec(memory_space=pl.ANY),
                    pl.BlockSpec((tc, TOP_K), row), pl.BlockSpec((1, D_MODEL), lambda i: (0, 0))],
        out_specs=pl.BlockSpec((tc, D_MODEL), row),
        out_shape=jax.ShapeDtypeStruct((nt * tc, D_MODEL), _f32),
        scratch_shapes=[pltpu.VMEM((2, TOP_K, tc * ROW_TILES, LANES), _f32), pltpu.SemaphoreType.DMA((2,))],
        compiler_params=pltpu.CompilerParams(dimension_semantics=("arbitrary",),
                                             vmem_limit_bytes=VMEM_LIMIT),
        name="combine_norm",
    )(*([dest[k] for k in range(TOP_K)] * 2), x1r, ys, gates_t, gf)


def _plan_blocks(route_i, counts, n_tokens):
    tm = EXPERT_TILE
    nblk = (n_tokens * TOP_K + N_EXPERTS * tm) // tm
    counts = counts.astype(jnp.int32)
    blocks_e = (counts + tm - 1) // tm
    blk_end = jnp.cumsum(blocks_e)
    pad_start = (blk_end - blocks_e) * tm
    experts = jnp.arange(N_EXPERTS, dtype=jnp.int32)
    hot = route_i[:TOP_K, :, None] == experts
    dest = jnp.sum(jnp.where(hot, pad_start, 0), axis=-1) + route_i[TOP_K:]
    n_used = blk_end[-1]
    blk = jnp.arange(nblk, dtype=jnp.int32)
    src = jnp.minimum(blk, jnp.maximum(n_used - 1, 0))
    be = jnp.minimum(jnp.sum((blk_end[None, :] <= src[:, None]).astype(jnp.int32), axis=1), N_EXPERTS - 1)
    hot_b = be[:, None] == experts
    pick = lambda table: jnp.sum(jnp.where(hot_b, table, 0), axis=1)
    n_valid = jnp.where(blk < n_used, jnp.clip(pick(counts) - (src * tm - pick(pad_start)), 0, tm), 0)
    used = blocks_e > 0
    grp_e = jnp.cumsum(used.astype(jnp.int32)) - 1
    later = jnp.where((experts[None, :] > experts[:, None]) & used[None, :], experts[None, :], N_EXPERTS)
    nxt_e = jnp.min(later, axis=1)
    nxt_e = jnp.where(nxt_e < N_EXPERTS, nxt_e, -1)
    return be, n_valid.astype(jnp.int32), src, pick(grp_e), pick(nxt_e), dest


def _split_bf16(w):
    hi = w.astype(_bf16)
    return jnp.stack([hi, (w - hi.astype(_f32)).astype(_bf16)])


def kernel(x, positions, norm1_g, w_in, b_in, sinks, gn_dil, gn_swa, w_out, b_out,
           norm2_g, w_router, b_router, w_mlp1, b_mlp1, w_mlp2, b_mlp2, norm_f_g):
    batch, seq, d = x.shape
    T = batch * seq
    x2 = x.reshape(T, d)
    depth = norm1_g.shape[0]
    assert depth == 1, "the final norm is fused into the last layer's combine kernel"
    inv = ROPE_THETA ** (-jnp.arange(0, HEAD_DIM, 2, dtype=_f32) / HEAD_DIM)
    inv = jnp.tile(inv, LANES // (HEAD_DIM // 2)).reshape(1, LANES)
    pos = positions.reshape(T, 1)
    for layer in range(depth):
        qa, ka, va, qb, kb, vb = _inproj_call(
            x2, pos, inv, norm1_g[layer].reshape(1, d), w_in[layer].astype(_bf16), b_in[layer].reshape(1, D_IN))
        o_dil = _attn_call(qa, ka, va, None, batch=batch, seq=seq,
                           configs=tuple((w // r, r) for w, r in DILATED_CONFIGS), kv_group=1)
        sink_lanes = jnp.repeat(sinks[layer].astype(_f32), HEAD_DIM).reshape(1, W_SWA)
        o_swa = _attn_call(qb, kb, vb, sink_lanes, batch=batch, seq=seq,
                           configs=((SWA_WINDOW - 1, 1),), kv_group=N_HEADS_SWA // N_KV_SWA // PAIR)
        x1r, route_i, route_g, counts = _outproj_call(
            o_dil, o_swa, x2, gn_dil[layer].reshape(1, W_DIL), gn_swa[layer].reshape(1, W_SWA),
            w_out[layer].astype(_bf16), b_out[layer].reshape(1, d), norm2_g[layer].reshape(1, d),
            _split_bf16(w_router[layer].T), b_router[layer].reshape(N_EXPERTS, 1))
        block_e, n_valid, block_src, block_grp, next_e, dest = _plan_blocks(route_i, counts[:, 0], T)
        xs = _dispatch_call(dest, x1r, block_e.shape[0] * EXPERT_TILE)
        half = GLU_GROUP // 2
        b1p = b_mlp1[layer].reshape(N_EXPERTS, 2 * D_FF // GLU_GROUP, half, 2)
        b1p = jnp.swapaxes(b1p, 2, 3).reshape(N_EXPERTS, 1, 2 * D_FF)
        ys = _moe_call(block_e, n_valid, block_src, block_grp, next_e, xs, norm2_g[layer].reshape(1, d),
                       w_mlp1[layer], b1p, w_mlp2[layer], b_mlp2[layer].reshape(N_EXPERTS, 1, d))
        x2 = _combine_call(dest, x1r, ys, route_g[:TOP_K].T, norm_f_g.reshape(1, d))
    return x2.reshape(batch, seq, d)
```

```python
import functools

import jax
import jax.numpy as jnp
from jax import lax
from jax.experimental import pallas as pl
from jax.experimental.pallas import tpu as pltpu

D_MODEL = 1024
HEAD_DIM = 64
N_HEADS_DIL = 8
DILATED_CONFIGS = ((128, 1), (512, 4), (2048, 16))
N_HEADS_SWA = 8
N_KV_SWA = 2
SWA_WINDOW = 128
ATTN_BLOCK = 128
ROPE_THETA = 10000.0
N_EXPERTS = 32
TOP_K = 4
D_FF = D_MODEL
SWIGLU_ALPHA = 1.702
SWIGLU_LIMIT = 7.0
NORM_EPS = 1e-5
LOG2E = 1.4426950408889634

W_DIL = N_HEADS_DIL * HEAD_DIM
W_SWA = N_HEADS_SWA * HEAD_DIM
W_KV_SWA = N_KV_SWA * HEAD_DIM
D_IN = 3 * W_DIL + W_SWA + 2 * W_KV_SWA

LANES = 128
PAIR = LANES // HEAD_DIM
ROW_TILES = D_MODEL // LANES
TOKEN_TILE = 1024
EXPERT_TILE = 1024
DISPATCH_TILE = 2048
COMBINE_TILE = 256
DMA_GROUP = 32
ATTN_UNROLL = 32
MOE_CHAINS = 4
INPROJ_TILE = 2048
INPROJ_CHAINS = 8
OUTPROJ_CHAINS = 2
RANK_CHUNK = 256
GLU_GROUP = 256
VMEM_LIMIT = 56 * 1024 * 1024

_f32 = jnp.float32
_bf16 = jnp.bfloat16


def _rms(x, g):
    return x * lax.rsqrt(jnp.mean(x * x, axis=-1, keepdims=True) + NORM_EPS) * g


def _store_token_rows(ref, x):
    tm = x.shape[0]
    for c in range(ROW_TILES):
        ref[pl.ds(c, tm, stride=ROW_TILES), :] = x[:, c * LANES:(c + 1) * LANES]


def _load_token_rows(ref, tm):
    return jnp.concatenate([ref[pl.ds(c, tm, stride=ROW_TILES), :] for c in range(ROW_TILES)], axis=1)


def _token_row(ref, tok):
    return ref.at[pl.ds(pl.multiple_of(tok * ROW_TILES, ROW_TILES), ROW_TILES)]


def _inproj_kernel(x_ref, pos_ref, inv_ref, g_ref, w_ref, b_ref, *out_refs):
    sub = x_ref.shape[0] // INPROJ_CHAINS
    for s in range(INPROJ_CHAINS):
        rows = slice(s * sub, (s + 1) * sub)
        _inproj_rows(x_ref[rows, :], pos_ref[rows, :], inv_ref, g_ref, w_ref, b_ref,
                     [r.at[rows, :] for r in out_refs])


def _inproj_rows(x, pos, inv_ref, g_ref, w_ref, b_ref, out_refs):
    qa_ref, ka_ref, va_ref, qb_ref, kb_ref, vb_ref = out_refs
    h = _rms(x, g_ref[...]).astype(_bf16)
    proj = jnp.dot(h, w_ref[...], preferred_element_type=_f32) + b_ref[...]

    ang = pos.astype(_f32) * inv_ref[...]
    lane = lax.broadcasted_iota(jnp.int32, ang.shape, 1)
    first_half = (lane % HEAD_DIM) < (HEAD_DIM // 2)
    cos = jnp.cos(ang)
    sin = jnp.where(first_half, -jnp.sin(ang), jnp.sin(ang))

    def rope(t, scale):
        outs = []
        for c in range(t.shape[1] // LANES):
            tc = t[:, c * LANES:(c + 1) * LANES]
            partner = jnp.where(first_half, pltpu.roll(tc, LANES - HEAD_DIM // 2, 1),
                                pltpu.roll(tc, HEAD_DIM // 2, 1))
            outs.append((tc * cos + partner * sin) * scale)
        return jnp.concatenate(outs, axis=1) if len(outs) > 1 else outs[0]

    scale = HEAD_DIM ** -0.5 * LOG2E
    o = 0
    qa_ref[...] = rope(proj[:, o:o + W_DIL], scale).astype(_bf16); o += W_DIL
    ka_ref[...] = rope(proj[:, o:o + W_DIL], 1.0).astype(_bf16); o += W_DIL
    va_ref[...] = proj[:, o:o + W_DIL].astype(_bf16); o += W_DIL
    qb_ref[...] = rope(proj[:, o:o + W_SWA], scale).astype(_bf16); o += W_SWA
    kb = rope(proj[:, o:o + W_KV_SWA], 1.0); o += W_KV_SWA
    vb = proj[:, o:o + W_KV_SWA]

    def dup_heads(t):
        sw = pltpu.roll(t, HEAD_DIM, 1)
        lo = lane < HEAD_DIM
        return jnp.concatenate([jnp.where(lo, t, sw), jnp.where(lo, sw, t)], axis=1)

    kb_ref[...] = dup_heads(kb).astype(_bf16)
    vb_ref[...] = dup_heads(vb).astype(_bf16)


def _inproj_call(x2, pos, inv, g, w, b):
    T = x2.shape[0]
    tm = INPROJ_TILE
    row = lambda i: (i, 0)
    fix = lambda i: (0, 0)
    widths = (W_DIL, W_DIL, W_DIL, W_SWA, PAIR * W_KV_SWA, PAIR * W_KV_SWA)
    return pl.pallas_call(
        _inproj_kernel,
        grid=(T // tm,),
        in_specs=[pl.BlockSpec((tm, D_MODEL), row), pl.BlockSpec((tm, 1), row),
                  pl.BlockSpec((1, LANES), fix), pl.BlockSpec((1, D_MODEL), fix),
                  pl.BlockSpec((D_MODEL, D_IN), fix), pl.BlockSpec((1, D_IN), fix)],
        out_specs=[pl.BlockSpec((tm, wd), row) for wd in widths],
        out_shape=[jax.ShapeDtypeStruct((T, wd), _bf16) for wd in widths],
        compiler_params=pltpu.CompilerParams(dimension_semantics=("parallel",),
                                             vmem_limit_bytes=VMEM_LIMIT),
        name="inproj",
    )(x2, pos, inv, g, w, b)


def _attn_kernel(*refs, configs, seq, with_sink):
    if with_sink:
        q_ref, k_ref, v_ref, sink_ref, o_ref = refs[:5]
        scratch = refs[5:]
    else:
        q_ref, k_ref, v_ref, o_ref = refs[:4]
        scratch = refs[4:]
    n_cfg = len(configs)
    n_acc = n_cfg if n_cfg > 1 else 0
    obufs = scratch[:n_acc]
    mbufs = scratch[n_acc:2 * n_acc]
    lbufs = scratch[2 * n_acc:3 * n_acc]
    extra = list(scratch[3 * n_acc:])
    blk = ATTN_BLOCK
    dilated = [r for _, r in configs if r > 1]
    if dilated:
        qf, kf, vf = extra[:3]
        qf[...] = q_ref[...].astype(_f32)
        kf[...] = k_ref[...].astype(_f32)
        vf[...] = v_ref[...].astype(_f32)
        extra = extra[3:]

    lane = lax.broadcasted_iota(jnp.int32, (blk, LANES), 1)
    head0 = lane < HEAD_DIM
    lane_row = lax.broadcasted_iota(jnp.int32, (1, LANES), 1)
    head_rows = [(lane_row < HEAD_DIM).astype(_bf16), (lane_row >= HEAD_DIM).astype(_bf16)]
    qi = lax.broadcasted_iota(jnp.int32, (blk, 2 * blk), 0)
    kj = lax.broadcasted_iota(jnp.int32, (blk, 2 * blk), 1)
    dist = qi - kj + blk

    def merge(os_, ms_, ls_):
        ms_ = list(ms_)
        if with_sink:
            ms_.append(jnp.broadcast_to(sink_ref[...] * LOG2E, ms_[0].shape))
        mx = functools.reduce(jnp.maximum, ms_)
        es = [jnp.exp2(m - mx) for m in ms_]
        num = functools.reduce(lambda a, b: a + b, [e * o for e, o in zip(es, os_)])
        den = functools.reduce(lambda a, b: a + b, [e * l for e, l in zip(es, ls_)])
        if with_sink:
            den = den + es[-1]
        return num / den

    base_perm = None
    for ci, (max_dist, r) in enumerate(configs):
        nb = seq // r // blk
        band = (dist >= 0) & (dist <= max_dist)
        obuf, mbuf, lbuf = (obufs[ci], mbufs[ci], lbufs[ci]) if n_cfg > 1 else (None, None, None)
        if r == 1:
            qp, kp, vp = q_ref, k_ref, v_ref
        else:
            perm_bf16 = extra[:3]
            extra = extra[3:]
            later = [r2 for _, r2 in configs[ci + 1:] if r2 % r == 0]
            perm_f32 = []
            if later:
                perm_f32 = extra[:3]
                extra = extra[3:]
            r_src, srcs = 1, (qf, kf, vf)
            if base_perm is not None and r % base_perm[0] == 0:
                r_src, srcs = base_perm
            if perm_f32:
                base_perm = (r, perm_f32)
            qp, kp, vp = perm_bf16
            piece = 2 * blk
            step = r // r_src
            per_res = seq // r // piece
            len_src = seq // r_src

            def permute(idx, carry, r_src=r_src, srcs=srcs, step=step, per_res=per_res, len_src=len_src,
                        perm_bf16=perm_bf16, perm_f32=perm_f32):
                c = idx // per_res
                start = (c % r_src) * len_src + c // r_src + (step * piece) * (idx % per_res)
                src = pl.ds(start, piece, stride=step)
                dst = pl.ds(pl.multiple_of(idx * piece, piece), piece)
                for t in range(3):
                    val = srcs[t][src, :]
                    perm_bf16[t][dst, :] = val.astype(_bf16)
                    if perm_f32:
                        perm_f32[t][dst, :] = val
                return carry

            lax.fori_loop(0, seq // piece, permute, 0, unroll=True)

        def body(n, carry, r=r, nb=nb, band=band, obuf=obuf, lbuf=lbuf, qp=qp, kp=kp, vp=vp):
            j = n % nb
            cur = pl.ds(pl.multiple_of(n * blk, blk), blk)
            prev = pl.ds(pl.multiple_of(jnp.maximum(n - 1, 0) * blk, blk), blk)
            q = qp[cur, :]
            k2 = jnp.concatenate([kp[prev, :], kp[cur, :]], axis=0)
            v2 = jnp.concatenate([vp[prev, :], vp[cur, :]], axis=0)
            mask = band & (kj >= jnp.where(j > 0, 0, blk))
            q2 = jnp.concatenate([q * hm for hm in head_rows], axis=0)
            s2 = lax.dot_general(q2, k2, (((1,), (1,)), ((), ())), preferred_element_type=_f32)
            ps, maxs, sums = [], [], []
            for h in range(PAIR):
                s = jnp.where(mask, s2[h * blk:(h + 1) * blk], -jnp.inf)
                m = jnp.max(s, axis=-1, keepdims=True)
                p = jnp.exp2(s - m)
                ps.append(p.astype(_bf16))
                maxs.append(jnp.broadcast_to(m, (blk, LANES)))
                sums.append(jnp.broadcast_to(jnp.sum(p, axis=-1, keepdims=True), (blk, LANES)))
            o2 = jnp.dot(jnp.concatenate(ps, axis=0), v2, preferred_element_type=_f32)
            outs = [o2[:blk], o2[blk:]]
            start = n // nb + (r * blk) * j
            dst = pl.ds(pl.multiple_of(start, blk), blk) if r == 1 else pl.ds(start, blk, stride=r)
            o_pair = jnp.where(head0, outs[0], outs[1])
            m_pair = jnp.where(head0, maxs[0], maxs[1])
            l_pair = jnp.where(head0, sums[0], sums[1])
            if n_cfg == 1:
                o_ref[dst, :] = merge([o_pair], [m_pair], [l_pair]).astype(o_ref.dtype)
            else:
                obuf[dst, :] = o_pair
                mbuf[dst, :] = m_pair
                lbuf[dst, :] = l_pair
            return carry

        lax.fori_loop(0, seq // blk, body, 0, unroll=ATTN_UNROLL)

    if n_cfg > 1:
        chunk = 4 * blk

        def finish(i, carry):
            sl = pl.ds(pl.multiple_of(i * chunk, chunk), chunk)
            o_ref[sl, :] = merge([b[sl, :] for b in obufs], [b[sl, :] for b in mbufs],
                                 [b[sl, :] for b in lbufs]).astype(o_ref.dtype)
            return carry

        lax.fori_loop(0, seq // chunk, finish, 0)


def _attn_call(q, k, v, sinks, *, batch, seq, configs, kv_group):
    n_pairs = q.shape[1] // LANES
    with_sink = sinks is not None
    qmap = lambda b, p: (b, p)
    kvmap = lambda b, p: (b, p // kv_group)
    in_specs = [pl.BlockSpec((seq, LANES), qmap), pl.BlockSpec((seq, LANES), kvmap),
                pl.BlockSpec((seq, LANES), kvmap)]
    args = [q, k, v]
    if with_sink:
        in_specs.append(pl.BlockSpec((1, LANES), lambda b, p: (0, p)))
        args.append(sinks)
    buf = lambda dt: pltpu.VMEM((seq, LANES), dt)
    scratch = [buf(_f32) for _ in range(3 * len(configs) if len(configs) > 1 else 0)]
    if any(r > 1 for _, r in configs):
        scratch += [buf(_f32) for _ in range(3)]
    for ci, (_, r) in enumerate(configs):
        if r > 1:
            scratch += [buf(_bf16) for _ in range(3)]
            if any(r2 % r == 0 for _, r2 in configs[ci + 1:]):
                scratch += [buf(_f32) for _ in range(3)]
    return pl.pallas_call(
        functools.partial(_attn_kernel, configs=configs, seq=seq, with_sink=with_sink),
        grid=(batch, n_pairs),
        in_specs=in_specs,
        out_specs=pl.BlockSpec((seq, LANES), qmap),
        out_shape=jax.ShapeDtypeStruct(q.shape, _bf16),
        scratch_shapes=scratch,
        compiler_params=pltpu.CompilerParams(dimension_semantics=("parallel", "parallel"),
                                             vmem_limit_bytes=VMEM_LIMIT),
        name="attn_sink" if with_sink else "attn_dilated",
    )(*args)


def _outproj_kernel(od_ref, os_ref, x_ref, gd_ref, gs_ref, w_ref, b_ref, g2_ref, wr_ref, br_ref,
                    x1_ref, ri_ref, rg_ref, cnt_ref, carry_ref, before_ref):
    tm = x_ref.shape[0]

    @pl.when(pl.program_id(0) == 0)
    def _():
        carry_ref[...] = jnp.zeros_like(carry_ref)
        ti = lax.broadcasted_iota(jnp.int32, (RANK_CHUNK, RANK_CHUNK), 0)
        tj = lax.broadcasted_iota(jnp.int32, (RANK_CHUNK, RANK_CHUNK), 1)
        before_ref[...] = (ti < tj).astype(_bf16)

    sub = tm // OUTPROJ_CHAINS
    h2 = []
    for s in range(OUTPROJ_CHAINS):
        rows = slice(s * sub, (s + 1) * sub)
        mixed = jnp.concatenate([_rms(od_ref[rows, :].astype(_f32), gd_ref[...]),
                                 _rms(os_ref[rows, :].astype(_f32), gs_ref[...])], axis=1)
        x1 = x_ref[rows, :] + jnp.dot(mixed.astype(_bf16), w_ref[...], preferred_element_type=_f32) + b_ref[...]
        _store_token_rows(x1_ref.at[pl.ds(s * sub * ROW_TILES, sub * ROW_TILES)], x1)
        h2.append(_rms(x1, g2_ref[...]))
    h2 = jnp.concatenate(h2, axis=0)
    h_hi = h2.astype(_bf16)
    h_lo = (h2 - h_hi.astype(_f32)).astype(_bf16)
    nt_dot = lambda a, b: lax.dot_general(a, b, (((1,), (1,)), ((), ())), preferred_element_type=_f32)
    logits = (nt_dot(wr_ref[0], h_hi) + nt_dot(wr_ref[0], h_lo) + nt_dot(wr_ref[1], h_hi)) + br_ref[...]
    eiota = lax.broadcasted_iota(jnp.int32, logits.shape, 0).astype(_f32)
    vals, idxs, sels = [], [], []
    l = logits
    for _ in range(TOP_K):
        m = jnp.max(l, axis=0, keepdims=True)
        idx = jnp.min(jnp.where(l == m, eiota, float(N_EXPERTS)), axis=0, keepdims=True)
        sel = eiota == idx
        vals.append(m); idxs.append(idx); sels.append(sel)
        l = jnp.where(sel, -jnp.inf, l)
    exps = [jnp.exp(v - vals[0]) for v in vals]
    den = functools.reduce(lambda a, b: a + b, exps)
    gates = [e / den for e in exps]

    chosen = functools.reduce(jnp.logical_or, sels).astype(_f32)
    carry = carry_ref[...]
    prefix = []
    for c in range(tm // RANK_CHUNK):
        part = chosen[:, c * RANK_CHUNK:(c + 1) * RANK_CHUNK]
        prefix.append(jnp.dot(part.astype(_bf16), before_ref[...], preferred_element_type=_f32) + carry)
        carry = carry + jnp.sum(part, axis=1, keepdims=True)
    prefix = jnp.concatenate(prefix, axis=1)
    ranks = [jnp.sum(jnp.where(s, prefix, 0.0), axis=0, keepdims=True).astype(jnp.int32) for s in sels]
    carry_ref[...] = carry

    ri_ref[...] = jnp.concatenate([ix.astype(jnp.int32) for ix in idxs] + ranks, axis=0)
    rg_ref[...] = jnp.concatenate(gates + gates, axis=0)
    cnt_ref[...] = jnp.broadcast_to(carry_ref[...], cnt_ref.shape)


def _outproj_call(od, osw, x2, gd, gs, w, b, g2, wr_t, br):
    T = x2.shape[0]
    tm = TOKEN_TILE
    row = lambda i: (i, 0)
    col = lambda i: (0, i)
    fix = lambda i: (0, 0)
    return pl.pallas_call(
        _outproj_kernel,
        grid=(T // tm,),
        in_specs=[pl.BlockSpec((tm, W_DIL), row), pl.BlockSpec((tm, W_SWA), row),
                  pl.BlockSpec((tm, D_MODEL), row),
                  pl.BlockSpec((1, W_DIL), fix), pl.BlockSpec((1, W_SWA), fix),
                  pl.BlockSpec((D_MODEL, D_MODEL), fix), pl.BlockSpec((1, D_MODEL), fix),
                  pl.BlockSpec((1, D_MODEL), fix),
                  pl.BlockSpec((2, N_EXPERTS, D_MODEL), lambda i: (0, 0, 0)), pl.BlockSpec((N_EXPERTS, 1), fix)],
        out_specs=[pl.BlockSpec((tm * ROW_TILES, LANES), row), pl.BlockSpec((2 * TOP_K, tm), col),
                   pl.BlockSpec((2 * TOP_K, tm), col), pl.BlockSpec((N_EXPERTS, LANES), fix)],
        out_shape=[jax.ShapeDtypeStruct((T * ROW_TILES, LANES), _f32),
                   jax.ShapeDtypeStruct((2 * TOP_K, T), jnp.int32),
                   jax.ShapeDtypeStruct((2 * TOP_K, T), _f32),
                   jax.ShapeDtypeStruct((N_EXPERTS, LANES), _f32)],
        scratch_shapes=[pltpu.VMEM((N_EXPERTS, 1), _f32), pltpu.VMEM((RANK_CHUNK, RANK_CHUNK), _bf16)],
        compiler_params=pltpu.CompilerParams(dimension_semantics=("arbitrary",),
                                             vmem_limit_bytes=VMEM_LIMIT),
        name="outproj_router",
    )(od, osw, x2, gd, gs, w, b, g2, wr_t, br)


def _dispatch_kernel(*refs):
    d_ref = refs[:TOP_K]
    x1_ref, xs_hbm, sem = refs[TOP_K:]
    td = DISPATCH_TILE

    def start_group(g, c):
        for rr in range(DMA_GROUP):
            r = g * DMA_GROUP + rr
            src = _token_row(x1_ref, r)
            for k in range(TOP_K):
                pltpu.make_async_copy(src, _token_row(xs_hbm, d_ref[k][r]), sem).start(priority=k % 2)
        return c

    lax.fori_loop(0, td // DMA_GROUP, start_group, 0)

    def wait_group(g, c):
        for _ in range(DMA_GROUP * TOP_K):
            pltpu.make_async_copy(_token_row(x1_ref, 0), _token_row(xs_hbm, 0), sem).wait()
        return c

    lax.fori_loop(0, td // DMA_GROUP, wait_group, 0)


def _dispatch_call(dest, x1r, n_slots):
    nt = dest.shape[1] // DISPATCH_TILE
    return pl.pallas_call(
        _dispatch_kernel,
        grid=(nt,),
        in_specs=[pl.BlockSpec((DISPATCH_TILE,), lambda i: (i,), memory_space=pltpu.SMEM) for _ in range(TOP_K)]
                 + [pl.BlockSpec((DISPATCH_TILE * ROW_TILES, LANES), lambda i: (i, 0))],
        out_specs=pl.BlockSpec(memory_space=pl.ANY),
        out_shape=jax.ShapeDtypeStruct((n_slots * ROW_TILES, LANES), _f32),
        scratch_shapes=[pltpu.SemaphoreType.DMA(())],
        compiler_params=pltpu.CompilerParams(dimension_semantics=("arbitrary",),
                                             vmem_limit_bytes=VMEM_LIMIT),
        name="dispatch_rows",
    )(*[dest[k] for k in range(TOP_K)], x1r)


def _moe_kernel(be_ref, nv_ref, src_ref, grp_ref, nxt_ref, xs_ref, g2_ref, w1_hbm, b1_ref, w2_hbm, b2_ref,
                ys_ref, st1, st2, w1p, w2b, sem):
    i = pl.program_id(0)
    nv = nv_ref[i]
    tm = EXPERT_TILE

    def weight_copies(e, slot):
        return (pltpu.make_async_copy(w1_hbm.at[e], st1.at[slot], sem.at[0, slot]),
                pltpu.make_async_copy(w2_hbm.at[e], st2.at[slot], sem.at[1, slot]))

    @pl.when(nv > 0)
    def _():
        first_of_expert = jnp.logical_or(i == 0, be_ref[i] != be_ref[jnp.maximum(i - 1, 0)])

        @pl.when(first_of_expert)
        def _():
            slot = grp_ref[i] % 2

            @pl.when(i == 0)
            def _():
                for c in weight_copies(be_ref[0], 0):
                    c.start()

            for c in weight_copies(be_ref[i], slot):
                c.wait()

            @pl.when(nxt_ref[i] >= 0)
            def _():
                for c in weight_copies(nxt_ref[i], 1 - slot):
                    c.start()

            src = lax.broadcasted_iota(jnp.int32, (GLU_GROUP, GLU_GROUP), 0)
            dst = lax.broadcasted_iota(jnp.int32, (GLU_GROUP, GLU_GROUP), 1)
            half = GLU_GROUP // 2
            perm = (src == jnp.where(dst < half, 2 * dst, 2 * (dst - half) + 1)).astype(_bf16)
            for g in range(2 * D_FF // GLU_GROUP):
                cols = slice(g * GLU_GROUP, (g + 1) * GLU_GROUP)
                w1p[:, cols] = jnp.dot(st1[slot, :, cols].astype(_bf16), perm,
                                       preferred_element_type=_f32).astype(_bf16)
            w2b[...] = st2[slot].astype(_bf16)

        hm = tm // MOE_CHAINS

        def sub_block(s):
            rows = lambda c: pl.ds(s * hm * ROW_TILES + c, hm, stride=ROW_TILES)
            x = jnp.concatenate([xs_ref[rows(c), :] for c in range(ROW_TILES)], axis=1)
            x = jnp.where(lax.broadcasted_iota(jnp.int32, (hm, 1), 0) < nv - s * hm, x, 0.0)
            h = _rms(x, g2_ref[...]).astype(_bf16)
            u = jnp.dot(h, w1p[...], preferred_element_type=_f32) + b1_ref[0]
            half = GLU_GROUP // 2
            hid = []
            for g in range(2 * D_FF // GLU_GROUP):
                glu = jnp.minimum(u[:, g * GLU_GROUP:g * GLU_GROUP + half], SWIGLU_LIMIT)
                lin = jnp.clip(u[:, g * GLU_GROUP + half:(g + 1) * GLU_GROUP], -SWIGLU_LIMIT, SWIGLU_LIMIT)
                hid.append(glu * (1.0 / (1.0 + jnp.exp(-SWIGLU_ALPHA * glu))) * (lin + 1.0))
            hid = jnp.concatenate(hid, axis=1).astype(_bf16)
            y = jnp.dot(hid, w2b[...], preferred_element_type=_f32) + b2_ref[0]
            for c in range(ROW_TILES):
                ys_ref[rows(c), :] = y[:, c * LANES:(c + 1) * LANES]

        for n_sub in range(1, MOE_CHAINS + 1):
            @pl.when(jnp.logical_and(nv > (n_sub - 1) * hm, nv <= n_sub * hm))
            def _(n_sub=n_sub):
                for s in range(n_sub):
                    sub_block(s)


def _moe_call(block_e, n_valid, block_src, block_grp, next_e, xs, g2, w1, b1p, w2, b2):
    nblk = block_e.shape[0]
    tm = EXPERT_TILE
    by_e = lambda i, be, nv, src, grp, nxt: (be[i], 0, 0)
    by_src = lambda i, be, nv, src, grp, nxt: (src[i], 0)
    fix = lambda i, be, nv, src, grp, nxt: (0, 0)
    grid_spec = pltpu.PrefetchScalarGridSpec(
        num_scalar_prefetch=5,
        grid=(nblk,),
        in_specs=[pl.BlockSpec((tm * ROW_TILES, LANES), by_src), pl.BlockSpec((1, D_MODEL), fix),
                  pl.BlockSpec(memory_space=pl.ANY), pl.BlockSpec((1, 1, 2 * D_FF), by_e),
                  pl.BlockSpec(memory_space=pl.ANY), pl.BlockSpec((1, 1, D_MODEL), by_e)],
        out_specs=pl.BlockSpec((tm * ROW_TILES, LANES), by_src),
        scratch_shapes=[pltpu.VMEM((2, D_MODEL, 2 * D_FF), _f32), pltpu.VMEM((2, D_FF, D_MODEL), _f32),
                        pltpu.VMEM((D_MODEL, 2 * D_FF), _bf16), pltpu.VMEM((D_FF, D_MODEL), _bf16),
                        pltpu.SemaphoreType.DMA((2, 2))],
    )
    return pl.pallas_call(
        _moe_kernel,
        grid_spec=grid_spec,
        out_shape=jax.ShapeDtypeStruct(xs.shape, _f32),
        compiler_params=pltpu.CompilerParams(dimension_semantics=("arbitrary",),
                                             vmem_limit_bytes=VMEM_LIMIT),
        name="moe_experts",
    )(block_e, n_valid, block_src, block_grp, next_e, xs, g2, w1, b1p, w2, b2)


def _combine_kernel(*refs):
    dcur_ref, dnext_ref = refs[:TOP_K], refs[TOP_K:2 * TOP_K]
    x1_ref, ys_hbm, g_ref, gf_ref, o_ref, gbuf, sem = refs[2 * TOP_K:]
    tc = COMBINE_TILE
    i = pl.program_id(0)
    slot = i % 2

    def row_copy(d_ref, s, r, k):
        dst = gbuf.at[s, k, pl.ds(r * ROW_TILES, ROW_TILES)]
        return pltpu.make_async_copy(_token_row(ys_hbm, d_ref[k][r]), dst, sem.at[s])

    def issue(d_ref, s):
        def group(g, c):
            for rr in range(DMA_GROUP):
                for k in range(TOP_K):
                    row_copy(d_ref, s, g * DMA_GROUP + rr, k).start(priority=k % 2)
            return c
        lax.fori_loop(0, tc // DMA_GROUP, group, 0)

    @pl.when(i == 0)
    def _():
        issue(dcur_ref, 0)

    @pl.when(i + 1 < pl.num_programs(0))
    def _():
        issue(dnext_ref, 1 - slot)

    def wait_group(g, c):
        for _ in range(DMA_GROUP * TOP_K):
            row_copy(dcur_ref, slot, 0, 0).wait()
        return c

    lax.fori_loop(0, tc // DMA_GROUP, wait_group, 0)

    g = g_ref[...]
    acc = _load_token_rows(x1_ref, tc)
    for k in range(TOP_K):
        acc = acc + g[:, k:k + 1] * _load_token_rows(gbuf.at[slot, k], tc)
    o_ref[...] = _rms(acc, gf_ref[...])


def _combine_call(dest, x1r, ys, gates_t, gf):
    tc = COMBINE_TILE
    nt = dest.shape[1] // tc
    row = lambda i: (i, 0)
    slots = lambda tile: pl.BlockSpec((tc,), lambda i: (tile(i),), memory_space=pltpu.SMEM)
    return pl.pallas_call(
        _combine_kernel,
        grid=(nt,),
        in_specs=[slots(lambda i: i) for _ in range(TOP_K)]
                 + [slots(lambda i: jnp.minimum(i + 1, nt - 1)) for _ in range(TOP_K)]
                 + [pl.BlockSpec((tc * ROW_TILES, LANES), row), pl.BlockSpec(memory_space=pl.ANY),
                    pl.BlockSpec((tc, TOP_K), row), pl.BlockSpec((1, D_MODEL), lambda i: (0, 0))],
        out_specs=pl.BlockSpec((tc, D_MODEL), row),
        out_shape=jax.ShapeDtypeStruct((nt * tc, D_MODEL), _f32),
        scratch_shapes=[pltpu.VMEM((2, TOP_K, tc * ROW_TILES, LANES), _f32), pltpu.SemaphoreType.DMA((2,))],
        compiler_params=pltpu.CompilerParams(dimension_semantics=("arbitrary",),
                                             vmem_limit_bytes=VMEM_LIMIT),
        name="combine_norm",
    )(*([dest[k] for k in range(TOP_K)] * 2), x1r, ys, gates_t, gf)


def _plan_blocks(route_i, counts, n_tokens):
    tm = EXPERT_TILE
    nblk = (n_tokens * TOP_K + N_EXPERTS * tm) // tm
    counts = counts.astype(jnp.int32)
    blocks_e = (counts + tm - 1) // tm
    blk_end = jnp.cumsum(blocks_e)
    pad_start = (blk_end - blocks_e) * tm
    experts = jnp.arange(N_EXPERTS, dtype=jnp.int32)
    hot = route_i[:TOP_K, :, None] == experts
    dest = jnp.sum(jnp.where(hot, pad_start, 0), axis=-1) + route_i[TOP_K:]
    n_used = blk_end[-1]
    blk = jnp.arange(nblk, dtype=jnp.int32)
    src = jnp.minimum(blk, jnp.maximum(n_used - 1, 0))
    be = jnp.minimum(jnp.sum((blk_end[None, :] <= src[:, None]).astype(jnp.int32), axis=1), N_EXPERTS - 1)
    hot_b = be[:, None] == experts
    pick = lambda table: jnp.sum(jnp.where(hot_b, table, 0), axis=1)
    n_valid = jnp.where(blk < n_used, jnp.clip(pick(counts) - (src * tm - pick(pad_start)), 0, tm), 0)
    used = blocks_e > 0
    grp_e = jnp.cumsum(used.astype(jnp.int32)) - 1
    later = jnp.where((experts[None, :] > experts[:, None]) & used[None, :], experts[None, :], N_EXPERTS)
    nxt_e = jnp.min(later, axis=1)
    nxt_e = jnp.where(nxt_e < N_EXPERTS, nxt_e, -1)
    return be, n_valid.astype(jnp.int32), src, pick(grp_e), pick(nxt_e), dest


def _split_bf16(w):
    hi = w.astype(_bf16)
    return jnp.stack([hi, (w - hi.astype(_f32)).astype(_bf16)])


def kernel(x, positions, norm1_g, w_in, b_in, sinks, gn_dil, gn_swa, w_out, b_out,
           norm2_g, w_router, b_router, w_mlp1, b_mlp1, w_mlp2, b_mlp2, norm_f_g):
    batch, seq, d = x.shape
    T = batch * seq
    x2 = x.reshape(T, d)
    depth = norm1_g.shape[0]
    assert depth == 1, "the final norm is fused into the last layer's combine kernel"
    inv = ROPE_THETA ** (-jnp.arange(0, HEAD_DIM, 2, dtype=_f32) / HEAD_DIM)
    inv = jnp.tile(inv, LANES // (HEAD_DIM // 2)).reshape(1, LANES)
    pos = positions.reshape(T, 1)
    for layer in range(depth):
        qa, ka, va, qb, kb, vb = _inproj_call(
            x2, pos, inv, norm1_g[layer].reshape(1, d), w_in[layer].astype(_bf16), b_in[layer].reshape(1, D_IN))
        o_dil = _attn_call(qa, ka, va, None, batch=batch, seq=seq,
                           configs=tuple((w // r, r) for w, r in DILATED_CONFIGS), kv_group=1)
        sink_lanes = jnp.repeat(sinks[layer].astype(_f32), HEAD_DIM).reshape(1, W_SWA)
        o_swa = _attn_call(qb, kb, vb, sink_lanes, batch=batch, seq=seq,
                           configs=((SWA_WINDOW - 1, 1),), kv_group=N_HEADS_SWA // N_KV_SWA // PAIR)
        x1r, route_i, route_g, counts = _outproj_call(
            o_dil, o_swa, x2, gn_dil[layer].reshape(1, W_DIL), gn_swa[layer].reshape(1, W_SWA),
            w_out[layer].astype(_bf16), b_out[layer].reshape(1, d), norm2_g[layer].reshape(1, d),
            _split_bf16(w_router[layer].T), b_router[layer].reshape(N_EXPERTS, 1))
        block_e, n_valid, block_src, block_grp, next_e, dest = _plan_blocks(route_i, counts[:, 0], T)
        xs = _dispatch_call(dest, x1r, block_e.shape[0] * EXPERT_TILE)
        half = GLU_GROUP // 2
        b1p = b_mlp1[layer].reshape(N_EXPERTS, 2 * D_FF // GLU_GROUP, half, 2)
        b1p = jnp.swapaxes(b1p, 2, 3).reshape(N_EXPERTS, 1, 2 * D_FF)
        ys = _moe_call(block_e, n_valid, block_src, block_grp, next_e, xs, norm2_g[layer].reshape(1, d),
                       w_mlp1[layer], b1p, w_mlp2[layer], b_mlp2[layer].reshape(N_EXPERTS, 1, d))
        x2 = _combine_call(dest, x1r, ys, route_g[:TOP_K].T, norm_f_g.reshape(1, d))
    return x2.reshape(batch, seq, d)
```
